```python
import jax, jax.numpy as jnp
from jax import lax
import numpy as np

D_MODEL = 1024
BATCH = 8
SEQ = 4096
DEPTH = 2

MEM_LEN = 256
EXPAND = 2
D_MIX = EXPAND * D_MODEL
GLA_HEADS = 4
GLA_HEAD_K = 128
GLA_HEAD_V = 256
GLA_KEY_WIDTH = GLA_HEADS * GLA_HEAD_K
GLA_VAL_WIDTH = GLA_HEADS * GLA_HEAD_V
GLA_LOWRANK = 16
GLA_GATE_TAU = 16.0
HGRN_HEADS = 4
HGRN_HEAD_DIM = 128
HGRN_WIDTH = HGRN_HEADS * HGRN_HEAD_DIM
LB_FLOOR = 1e-30
XATTN_HEADS = 4
XATTN_HEAD_DIM = 128
XATTN_WIDTH = XATTN_HEADS * XATTN_HEAD_DIM
SPLIT_SIZES = (GLA_KEY_WIDTH, GLA_KEY_WIDTH, GLA_VAL_WIDTH, GLA_LOWRANK,
               HGRN_WIDTH, HGRN_WIDTH, HGRN_WIDTH,
               XATTN_WIDTH,
               D_MIX)
D_IN_PROJ = 6160
CHUNK = 16
NORM_EPS = 1e-6

kernel_name = "hybrid_gla_hgrn2_memxattn_parallel_heads"


def rms_norm(x, w):
    xf = x.astype(jnp.float32)
    y = xf * lax.rsqrt(jnp.mean(xf * xf, axis=-1, keepdims=True) + NORM_EPS)
    return (y * w.astype(jnp.float32)).astype(x.dtype)


def split_heads(t, n_heads):
    b, s, w = t.shape
    return t.reshape(b, s, n_heads, w // n_heads).transpose(0, 2, 1, 3)


def merge_heads(t):
    b, h, s, d = t.shape
    return t.transpose(0, 2, 1, 3).reshape(b, s, h * d)


def chunked_gated_linear_attention(q, k, v, log_g):
    b_, h_, s_, dk = q.shape
    dv = v.shape[-1]
    n = s_ // CHUNK
    qf = q.astype(jnp.float32).reshape(b_, h_, n, CHUNK, dk)
    kf = k.astype(jnp.float32).reshape(b_, h_, n, CHUNK, dk)
    vf = v.astype(jnp.float32).reshape(b_, h_, n, CHUNK, dv)
    bcum = jnp.cumsum(log_g.astype(jnp.float32).reshape(b_, h_, n, CHUNK, dk), axis=3)
    b_last = bcum[..., CHUNK - 1:CHUNK, :]
    causal = jnp.tril(jnp.ones((CHUNK, CHUNK), dtype=bool))[:, :, None]
    diff = bcum[..., :, None, :] - bcum[..., None, :, :]
    decay = jnp.where(causal, jnp.exp(jnp.where(causal, diff, 0.0)), 0.0)
    scores = jnp.einsum('bhnid,bhnjd,bhnijd->bhnij', qf, kf, decay)
    o_intra = jnp.einsum('bhnij,bhnjv->bhniv', scores, vf)
    chunk_kv = jnp.einsum('bhncd,bhncv->bhndv', kf * jnp.exp(b_last - bcum), vf)
    chunk_decay = jnp.exp(b_last[..., 0, :])

    def step(state, inp):
        dec, kv = inp
        return dec[..., None] * state + kv, state

    init = jnp.zeros((b_, h_, dk, dv), jnp.float32)
    _, states_prev = lax.scan(step, init, (jnp.moveaxis(chunk_decay, 2, 0), jnp.moveaxis(chunk_kv, 2, 0)))
    o_inter = jnp.einsum('bhncd,nbhdv->bhncv', qf * jnp.exp(bcum), states_prev)
    return (o_intra + o_inter).reshape(b_, h_, s_, dv)


def setup_inputs(seed: int = 0) -> dict:
    key = jax.random.key(seed)
    ks = jax.random.split(key, 14)
    f32 = jnp.float32
    nrm = lambda k, shape, scale: (jax.random.normal(k, shape, f32) * scale).astype(f32)
    return {
        "x": nrm(ks[0], (BATCH, SEQ, D_MODEL), 1.0),
        "mem": nrm(ks[1], (BATCH, MEM_LEN, D_MODEL), 1.0),
        "norm_w": 1.0 + nrm(ks[2], (DEPTH, D_MODEL), 0.1),
        "w_in": nrm(ks[3], (DEPTH, D_MODEL, D_IN_PROJ), D_MODEL ** -0.5),
        "gla_w_gate_up": nrm(ks[4], (DEPTH, GLA_LOWRANK, GLA_KEY_WIDTH), GLA_LOWRANK ** -0.5),
        "gla_b_gate": nrm(ks[5], (DEPTH, GLA_KEY_WIDTH), 0.1),
        "gla_norm_w": 1.0 + nrm(ks[6], (DEPTH, GLA_HEAD_V), 0.1),
        "hgrn_lower_bounds": nrm(ks[7], (DEPTH, HGRN_WIDTH), 0.5),
        "hgrn_norm_w": 1.0 + nrm(ks[8], (DEPTH, HGRN_HEAD_DIM), 0.1),
        "mem_norm_w": 1.0 + nrm(ks[9], (DEPTH, D_MODEL), 0.1),
        "w_mem_kv": nrm(ks[10], (DEPTH, D_MODEL, 2 * XATTN_WIDTH), D_MODEL ** -0.5),
        "xattn_norm_w": 1.0 + nrm(ks[11], (DEPTH, XATTN_HEAD_DIM), 0.1),
        "w_out": nrm(ks[12], (DEPTH, D_MIX, D_MODEL), D_MIX ** -0.5),
        "final_norm_w": 1.0 + nrm(ks[13], (D_MODEL,), 0.1),
    }


def reference(x, mem, norm_w, w_in, gla_w_gate_up, gla_b_gate, gla_norm_w, hgrn_lower_bounds,
              hgrn_norm_w, mem_norm_w, w_mem_kv, xattn_norm_w, w_out, final_norm_w):
    dtype = x.dtype
    split_idx = [int(i) for i in np.cumsum(SPLIT_SIZES)[:-1]]
    lb_p = jax.nn.softmax(hgrn_lower_bounds.astype(jnp.float32), axis=0)
    lower_bounds = jnp.cumsum(lb_p, axis=0) - lb_p[0]

    for layer in range(DEPTH):
        h = rms_norm(x, norm_w[layer])
        proj = jnp.einsum('bsd,de->bse', h, w_in[layer])
        (gq, gk, gv, glr, hq, hf, hi, xq, gate) = jnp.split(proj, split_idx, axis=-1)

        glr_up = jnp.einsum('bsr,rk->bsk', glr, gla_w_gate_up[layer]) + gla_b_gate[layer]
        gla_log_a = jax.nn.log_sigmoid(glr_up.astype(jnp.float32)) / GLA_GATE_TAU
        gla_o = chunked_gated_linear_attention(
            split_heads(gq, GLA_HEADS) * (GLA_HEAD_K ** -0.5), split_heads(gk, GLA_HEADS),
            split_heads(gv, GLA_HEADS), split_heads(gla_log_a, GLA_HEADS))
        gla_o = merge_heads(rms_norm(gla_o, gla_norm_w[layer]))

        lb = lower_bounds[layer]
        zf = hf.astype(jnp.float32)
        hgrn_log_f = jnp.logaddexp(jnp.log(jnp.maximum(lb, LB_FLOOR)),
                                   jnp.log1p(-lb) + jax.nn.log_sigmoid(zf))
        hgrn_k = (1.0 - lb) * jax.nn.sigmoid(-zf)
        hgrn_o = chunked_gated_linear_attention(
            split_heads(hq, HGRN_HEADS), split_heads(hgrn_k, HGRN_HEADS),
            split_heads(hi, HGRN_HEADS), split_heads(hgrn_log_f, HGRN_HEADS))
        hgrn_o = merge_heads(rms_norm(hgrn_o, hgrn_norm_w[layer]))

        m = rms_norm(mem, mem_norm_w[layer])
        mkv = jnp.einsum('bmd,de->bme', m, w_mem_kv[layer])
        mk, mv = jnp.split(mkv, 2, axis=-1)
        s = jnp.einsum('bhsd,bhmd->bhsm', split_heads(xq, XATTN_HEADS).astype(jnp.float32),
                       split_heads(mk, XATTN_HEADS).astype(jnp.float32)) * (XATTN_HEAD_DIM ** -0.5)
        p = jax.nn.softmax(s, axis=-1)
        xo = jnp.einsum('bhsm,bhmd->bhsd', p, split_heads(mv, XATTN_HEADS).astype(jnp.float32))
        xo = merge_heads(rms_norm(xo, xattn_norm_w[layer]))

        mixed = jnp.concatenate([gla_o, hgrn_o, xo], axis=-1).astype(dtype)
        mixed = mixed * jax.nn.silu(gate)
        x = x + jnp.einsum('bse,ed->bsd', mixed, w_out[layer]).astype(dtype)

    return rms_norm(x, final_norm_w)
```

```python
import functools

import jax
import jax.numpy as jnp
from jax import lax
from jax.experimental import pallas as pl
from jax.experimental.pallas import tpu as pltpu

F32 = jnp.float32
BF16 = jnp.bfloat16

D_MODEL = 1024
DEPTH = 2
MEM_LEN = 256
N_HEADS = 4
HEAD_K = 128
GLA_HEAD_V = 256
HGRN_HEAD_V = 128
KEY_WIDTH = N_HEADS * HEAD_K
GLA_VAL_WIDTH = N_HEADS * GLA_HEAD_V
HGRN_VAL_WIDTH = N_HEADS * HGRN_HEAD_V
XATTN_WIDTH = N_HEADS * HEAD_K
D_MIX = 2048
GLA_LOWRANK = 16
GLA_GATE_TAU = 16.0
LB_FLOOR = 1e-30
NORM_EPS = 1e-6

OFF_GQ, OFF_GK, OFF_GV = 0, 512, 1024
OFF_HQ, OFF_HF, OFF_HI = 2048, 2560, 3072
OFF_XQ, OFF_GATE = 3584, 4096
W_MAIN_COLS = 6144
LOWRANK_PAD = 128

SEQ_TILE = 256
CHUNK = 64
SAFE_BLOCK = 16
FAST_PATH_LIMIT = 60.0

VMEM_LIMIT_BYTES = 56 * 1024 * 1024


def _dot(a, b):
    return jnp.dot(a, b, preferred_element_type=F32)


def _dot_nt(a, b):
    return lax.dot_general(a, b, (((1,), (1,)), ((), ())), preferred_element_type=F32)


def _dot_tn(a, b):
    return lax.dot_general(a, b, (((0,), (0,)), ((), ())), preferred_element_type=F32)


def _rms(x, w):
    return x * lax.rsqrt(jnp.mean(x * x, axis=-1, keepdims=True) + NORM_EPS) * w


def _log_sigmoid(z):
    return jnp.minimum(z, 0.0) - jnp.log1p(jnp.exp(-jnp.abs(z)))


def _chunk_cumsum(log_g, tri_blocks):
    hi = log_g.astype(BF16)
    lo = (log_g - hi.astype(F32)).astype(BF16)
    return _dot(tri_blocks, hi) + _dot(tri_blocks, lo)


def _recurrent_mixer(q_ref, k_ref, v_ref, b_ref, st_ref, norm_w, mixed_ref, out_off, dv):
    n_chunks = SEQ_TILE // CHUNK
    row = lax.broadcasted_iota(jnp.int32, (CHUNK, CHUNK), 0)
    col = lax.broadcasted_iota(jnp.int32, (CHUNK, CHUNK), 1)
    causal = row >= col
    row16 = lax.broadcasted_iota(jnp.int32, (SAFE_BLOCK, 1), 0)

    def finish(o, rows, h):
        mixed_ref[rows, out_off + h * dv:out_off + (h + 1) * dv] = _rms(o, norm_w)

    def chunk_body(c, carry):
        base = pl.multiple_of(c * CHUNK, CHUNK)
        rows = pl.ds(base, CHUNK)
        ref_all = b_ref[pl.ds(base + CHUNK // 2 - 1, 1), :]
        last_all = b_ref[pl.ds(base + CHUNK - 1, 1), :]
        worst = jnp.max(jnp.maximum(-ref_all, ref_all - last_all))
        fast_ok = worst < FAST_PATH_LIMIT

        @pl.when(fast_ok)
        def _fast():
            for h in range(N_HEADS):
                lanes = slice(h * HEAD_K, (h + 1) * HEAD_K)
                b = b_ref[rows, lanes]
                r = ref_all[:, lanes]
                bl = last_all[:, lanes]
                qs = q_ref[rows, lanes] * jnp.exp(b - r)
                ks = k_ref[rows, lanes] * jnp.exp(r - b)
                s = _dot_nt(qs.astype(BF16), ks.astype(BF16))
                s = jnp.where(causal, s, 0.0).astype(BF16)
                v = v_ref[rows, h * dv:(h + 1) * dv].astype(BF16)
                o = _dot(s, v)
                st = st_ref[h]
                o = o + _dot_nt((qs * jnp.exp(r)).astype(BF16), st.astype(BF16))
                kl = (ks * jnp.exp(bl - r)).astype(BF16)
                st_ref[h] = st * jnp.exp(bl) + _dot_tn(v, kl)
                finish(o, rows, h)

        @pl.when(jnp.logical_not(fast_ok))
        def _safe():
            def block_body(sb, carry2):
                s0 = pl.multiple_of(base + sb * SAFE_BLOCK, SAFE_BLOCK)
                rows_b = pl.ds(s0, SAFE_BLOCK)
                before = pl.multiple_of(jnp.maximum(s0 - 8, 0), 8)
                prev = b_ref[pl.ds(before, 8), :][7:8, :]
                prev = jnp.where(sb == 0, 0.0, prev)
                for h in range(N_HEADS):
                    lanes = slice(h * HEAD_K, (h + 1) * HEAD_K)
                    vl = slice(h * dv, (h + 1) * dv)
                    pv = prev[:, lanes]
                    bl = b_ref[rows_b, lanes] - pv
                    q = q_ref[rows_b, lanes]
                    k = k_ref[rows_b, lanes]
                    v = v_ref[rows_b, vl]

                    o = jnp.zeros((SAFE_BLOCK, dv), F32)
                    for j in range(SAFE_BLOCK):
                        w = jnp.exp(jnp.minimum(bl - bl[j:j + 1, :], 0.0))
                        sc = jnp.sum(q * k[j:j + 1, :] * w, axis=-1, keepdims=True)
                        o = o + jnp.where(row16 >= j, sc, 0.0) * v[j:j + 1, :]
                    st = st_ref[h]
                    o = o + _dot_nt((q * jnp.exp(bl)).astype(BF16), st.astype(BF16))
                    b_end = bl[SAFE_BLOCK - 1:SAFE_BLOCK, :]
                    kl = (k * jnp.exp(b_end - bl)).astype(BF16)
                    st_ref[h] = st * jnp.exp(b_end) + _dot_tn(v.astype(BF16), kl)
                    finish(o, rows_b, h)
                return carry2

            lax.fori_loop(0, CHUNK // SAFE_BLOCK, block_body, 0)

        return carry

    lax.fori_loop(0, n_chunks, chunk_body, 0)


def _layer_kernel(x_ref, mem_ref, norm_w_ref, w_main_ref, w_lr_ref, w_up_ref, b_gate_ref,
                  gla_norm_ref, lb_ref, hgrn_norm_ref, mem_norm_ref, w_mem_ref, xattn_norm_ref,
                  w_out_ref, final_norm_ref,
                  out_ref,
                  h_ref, q_ref, k_ref, b_ref, v_ref, mixed_ref, gated_ref,
                  mk_ref, mv_ref, gla_st_ref, hgrn_st_ref, *, layer, final_norm):
    t = pl.program_id(1)

    @pl.when(t == 0)
    def _start_of_sequence():
        gla_st_ref[...] = jnp.zeros_like(gla_st_ref)
        hgrn_st_ref[...] = jnp.zeros_like(hgrn_st_ref)
        m = _rms(mem_ref[0], mem_norm_ref[...]).astype(BF16)
        mk_ref[...] = _dot(m, w_mem_ref[:, :XATTN_WIDTH]).astype(BF16)
        mv_ref[...] = _dot(m, w_mem_ref[:, XATTN_WIDTH:]).astype(BF16)

    x = x_ref[0]
    h_ref[...] = _rms(x, norm_w_ref[...]).astype(BF16)

    def proj(off, width):
        return _dot(h_ref[...], w_main_ref[:, off:off + width])

    row = lax.broadcasted_iota(jnp.int32, (SEQ_TILE, SEQ_TILE), 0)
    col = lax.broadcasted_iota(jnp.int32, (SEQ_TILE, SEQ_TILE), 1)
    same_chunk = (row // CHUNK) == (col // CHUNK)
    tri_blocks = jnp.where(same_chunk & (col <= row), 1.0, 0.0).astype(BF16)

    low_rank = _dot(h_ref[...], w_lr_ref[...]).astype(BF16)
    z = _dot(low_rank, w_up_ref[...]) + b_gate_ref[...]
    b_ref[...] = _chunk_cumsum(_log_sigmoid(z) * (1.0 / GLA_GATE_TAU), tri_blocks)
    q_ref[...] = proj(OFF_GQ, KEY_WIDTH) * (HEAD_K ** -0.5)
    k_ref[...] = proj(OFF_GK, KEY_WIDTH)
    v_ref[...] = proj(OFF_GV, GLA_VAL_WIDTH)
    _recurrent_mixer(q_ref, k_ref, v_ref, b_ref, gla_st_ref, gla_norm_ref[...], mixed_ref,
                     0, GLA_HEAD_V)

    lbw = lb_ref[...]
    lb_max = jnp.max(lbw, axis=0, keepdims=True)
    lb_e = jnp.exp(lbw - lb_max)
    lb_p = lb_e / jnp.sum(lb_e, axis=0, keepdims=True)
    lb = jnp.sum(lb_p[:layer + 1], axis=0, keepdims=True) - lb_p[0:1]
    zf = proj(OFF_HF, KEY_WIDTH)
    log_f = jnp.logaddexp(jnp.log(jnp.maximum(lb, LB_FLOOR)),
                          jnp.log1p(-lb) + _log_sigmoid(zf))
    b_ref[...] = _chunk_cumsum(log_f, tri_blocks)
    k_ref[...] = (1.0 - lb) * jax.nn.sigmoid(-zf)
    q_ref[...] = proj(OFF_HQ, KEY_WIDTH)
    v_ref[:, :HGRN_VAL_WIDTH] = proj(OFF_HI, HGRN_VAL_WIDTH)
    _recurrent_mixer(q_ref, k_ref, v_ref, b_ref, hgrn_st_ref, hgrn_norm_ref[...], mixed_ref,
                     GLA_VAL_WIDTH, HGRN_HEAD_V)

    xoff = GLA_VAL_WIDTH + HGRN_VAL_WIDTH
    for hd in range(N_HEADS):
        lanes = slice(hd * HEAD_K, (hd + 1) * HEAD_K)
        qh = proj(OFF_XQ + hd * HEAD_K, HEAD_K).astype(BF16)
        s = _dot_nt(qh, mk_ref[:, lanes]) * (HEAD_K ** -0.5)
        p = jnp.exp(s - jnp.max(s, axis=-1, keepdims=True))
        o = _dot(p.astype(BF16), mv_ref[:, lanes]) / jnp.sum(p, axis=-1, keepdims=True)
        mixed_ref[:, xoff + hd * HEAD_K:xoff + (hd + 1) * HEAD_K] = _rms(o, xattn_norm_ref[...])

    gate_block = 512
    for c0 in range(0, D_MIX, gate_block):
        g = proj(OFF_GATE + c0, gate_block)
        gated_ref[:, c0:c0 + gate_block] = (
            mixed_ref[:, c0:c0 + gate_block] * (g * jax.nn.sigmoid(g))).astype(BF16)
    y = x + _dot(gated_ref[...], w_out_ref[...])
    if final_norm:
        y = _rms(y, final_norm_ref[...])
    out_ref[0] = y


def _layer_call(x, mem, params, layer, final_norm):
    batch, seq, _ = x.shape
    n_tiles = seq // SEQ_TILE
    const2 = lambda b, t: (0, 0)

    def whole(arr):
        return pl.BlockSpec(arr.shape, const2, pipeline_mode=pl.Buffered(1))

    in_specs = [
        pl.BlockSpec((1, SEQ_TILE, D_MODEL), lambda b, t: (b, t, 0)),
        pl.BlockSpec((1, MEM_LEN, D_MODEL), lambda b, t: (b, 0, 0)),
    ] + [whole(p) for p in params]
    scratch = [
        pltpu.VMEM((SEQ_TILE, D_MODEL), BF16),
        pltpu.VMEM((SEQ_TILE, KEY_WIDTH), F32),
        pltpu.VMEM((SEQ_TILE, KEY_WIDTH), F32),
        pltpu.VMEM((SEQ_TILE, KEY_WIDTH), F32),
        pltpu.VMEM((SEQ_TILE, GLA_VAL_WIDTH), F32),
        pltpu.VMEM((SEQ_TILE, D_MIX), F32),
        pltpu.VMEM((SEQ_TILE, D_MIX), BF16),
        pltpu.VMEM((MEM_LEN, XATTN_WIDTH), BF16),
        pltpu.VMEM((MEM_LEN, XATTN_WIDTH), BF16),
        pltpu.VMEM((N_HEADS, GLA_HEAD_V, HEAD_K), F32),
        pltpu.VMEM((N_HEADS, HGRN_HEAD_V, HEAD_K), F32),
    ]
    return pl.pallas_call(
        functools.partial(_layer_kernel, layer=layer, final_norm=final_norm),
        out_shape=jax.ShapeDtypeStruct(x.shape, x.dtype),
        grid=(batch, n_tiles),
        in_specs=in_specs,
        out_specs=pl.BlockSpec((1, SEQ_TILE, D_MODEL), lambda b, t: (b, t, 0)),
        scratch_shapes=scratch,
        compiler_params=pltpu.CompilerParams(
            dimension_semantics=("arbitrary", "arbitrary"),
            vmem_limit_bytes=VMEM_LIMIT_BYTES),
        name=f"hybrid_layer_{layer}",
    )(x, mem, *params)


def kernel(x, mem, norm_w, w_in, gla_w_gate_up, gla_b_gate, gla_norm_w, hgrn_lower_bounds,
           hgrn_norm_w, mem_norm_w, w_mem_kv, xattn_norm_w, w_out, final_norm_w):
    assert x.shape[1] % SEQ_TILE == 0 and SEQ_TILE % CHUNK == 0 and CHUNK % SAFE_BLOCK == 0
    lr0 = OFF_HQ
    lr1 = lr0 + GLA_LOWRANK
    row2 = lambda v: v.reshape(1, -1).astype(F32)
    for layer in range(DEPTH):
        w = w_in[layer]
        w_main = jnp.concatenate([w[:, :lr0], w[:, lr1:]], axis=1).astype(BF16)
        w_lr = jnp.pad(w[:, lr0:lr1], ((0, 0), (0, LOWRANK_PAD - GLA_LOWRANK))).astype(BF16)
        w_up = jnp.pad(gla_w_gate_up[layer],
                       ((0, LOWRANK_PAD - GLA_LOWRANK), (0, 0))).astype(BF16)
        params = (
            row2(norm_w[layer]), w_main, w_lr, w_up, row2(gla_b_gate[layer]),
            row2(gla_norm_w[layer]), hgrn_lower_bounds.astype(F32), row2(hgrn_norm_w[layer]),
            row2(mem_norm_w[layer]), w_mem_kv[layer].astype(BF16), row2(xattn_norm_w[layer]),
            w_out[layer].astype(BF16), row2(final_norm_w),
        )
        x = _layer_call(x, mem, params, layer, final_norm=(layer == DEPTH - 1))
    return x
```

```python
import functools

import jax
import jax.numpy as jnp
from jax import lax
from jax.experimental import pallas as pl
from jax.experimental.pallas import tpu as pltpu

F32 = jnp.float32
BF16 = jnp.bfloat16

D_MODEL = 1024
DEPTH = 2
MEM_LEN = 256
N_HEADS = 4
HEAD_K = 128
GLA_HEAD_V = 256
HGRN_HEAD_V = 128
KEY_WIDTH = N_HEADS * HEAD_K
GLA_VAL_WIDTH = N_HEADS * GLA_HEAD_V
HGRN_VAL_WIDTH = N_HEADS * HGRN_HEAD_V
XATTN_WIDTH = N_HEADS * HEAD_K
D_MIX = 2048
GLA_LOWRANK = 16
GLA_GATE_TAU = 16.0
LB_FLOOR = 1e-30
NORM_EPS = 1e-6

OFF_GQ, OFF_GK, OFF_GV = 0, 512, 1024
OFF_HQ, OFF_HF, OFF_HI = 2048, 2560, 3072
OFF_XQ, OFF_GATE = 3584, 4096
LOWRANK_PAD = 128

SEQ_TILE = 256
GLA_CHUNKS = 1
HGRN_CHUNKS = 4
SAFE_BLOCK = 16
FAST_PATH_LIMIT = 60.0

VMEM_LIMIT_BYTES = 56 * 1024 * 1024


def _dot(a, b):
    return jnp.dot(a, b, preferred_element_type=F32)


def _dot_nt(a, b):
    return lax.dot_general(a, b, (((1,), (1,)), ((), ())), preferred_element_type=F32)


def _dot_tn(a, b):
    return lax.dot_general(a, b, (((0,), (0,)), ((), ())), preferred_element_type=F32)


def _rms(x, w):
    return x * lax.rsqrt(jnp.mean(x * x, axis=-1, keepdims=True) + NORM_EPS) * w


def _log_sigmoid(z):
    return jnp.minimum(z, 0.0) - jnp.log1p(jnp.exp(-jnp.abs(z)))


def _segment_cumsum(log_g, segment):
    row = lax.broadcasted_iota(jnp.int32, (SEQ_TILE, SEQ_TILE), 0)
    col = lax.broadcasted_iota(jnp.int32, (SEQ_TILE, SEQ_TILE), 1)
    keep = col <= row
    if segment != SEQ_TILE:
        keep = keep & ((row // segment) == (col // segment))
    tri = jnp.where(keep, 1.0, 0.0).astype(BF16)
    hi = log_g.astype(BF16)
    lo = (log_g - hi.astype(F32)).astype(BF16)
    return _dot(tri, hi) + _dot(tri, lo)


def _mixer_fast(q_ref, k_ref, v_ref, b_ref, st_ref, norm_w, mixed_ref, out_off, dv, n_chunks,
                qt_ref, kt_ref, qo_ref, ko_ref, qi_ref, ks_ref, ql_ref, kl_ref, vb_ref):
    assert n_chunks in (1, 4)
    chunk = SEQ_TILE // n_chunks
    half = SEQ_TILE // 2
    tot = [b_ref[(c + 1) * chunk - 1:(c + 1) * chunk, :] for c in range(n_chunks)]
    ref = [b_ref[c * chunk + chunk // 2 - 1:c * chunk + chunk // 2, :] for c in range(n_chunks)]
    before = [jnp.zeros_like(tot[0])]
    for c in range(n_chunks):
        before.append(before[-1] + tot[c])
    tile_total = before[n_chunks]
    if n_chunks == 4:
        e_in = [None] + [jnp.exp(before[c]) for c in range(1, 4)]
        e_out = [jnp.exp(tile_total - before[c + 1]) for c in range(3)] + [None]
        e_tot1 = jnp.exp(tot[1])
        e_tot2 = jnp.exp(tot[2])

    for c in range(n_chunks):
        rows = slice(c * chunk, (c + 1) * chunk)
        for h in range(N_HEADS):
            lanes = slice(h * HEAD_K, (h + 1) * HEAD_K)
            b = b_ref[rows, lanes]
            q = q_ref[rows, lanes]
            k = k_ref[rows, lanes]
            r = ref[c][:, lanes]
            qe = q * jnp.exp(b)
            ke = k * jnp.exp(tot[c][:, lanes] - b)
            qt_ref[rows, lanes] = (q * jnp.exp(b - r)).astype(BF16)
            kt_ref[rows, lanes] = (k * jnp.exp(r - b)).astype(BF16)
            if n_chunks == 1:
                qi_ref[rows, lanes] = qe.astype(BF16)
                ks_ref[rows, lanes] = ke.astype(BF16)
            else:
                qo_ref[rows, lanes] = qe.astype(BF16)
                ko_ref[rows, lanes] = ke.astype(BF16)
                qi = qe if c == 0 else qe * e_in[c][:, lanes]
                ks = ke if c == 3 else ke * e_out[c][:, lanes]
                qi_ref[rows, lanes] = qi.astype(BF16)
                ks_ref[rows, lanes] = ks.astype(BF16)
                if c == 0:
                    kl_ref[0:chunk, lanes] = (ke * e_tot1[:, lanes]).astype(BF16)
                elif c == 1:
                    kl_ref[chunk:half, lanes] = ke.astype(BF16)
                elif c == 2:
                    ql_ref[0:chunk, lanes] = qe.astype(BF16)
                else:
                    ql_ref[chunk:half, lanes] = (qe * e_tot2[:, lanes]).astype(BF16)
    vb_ref[:, :N_HEADS * dv] = v_ref[:, :N_HEADS * dv].astype(BF16)

    if n_chunks == 1:
        row = lax.broadcasted_iota(jnp.int32, (SEQ_TILE, SEQ_TILE), 0)
        col = lax.broadcasted_iota(jnp.int32, (SEQ_TILE, SEQ_TILE), 1)
        causal = row >= col
    else:
        row = lax.broadcasted_iota(jnp.int32, (half, half), 0)
        col = lax.broadcasted_iota(jnp.int32, (half, half), 1)
        same_chunk = ((row >= chunk) == (col >= chunk)) & (row >= col)
        later_chunk = (row >= chunk) & (col < chunk)
    for h in range(N_HEADS):
        lanes = slice(h * HEAD_K, (h + 1) * HEAD_K)
        vl = slice(h * dv, (h + 1) * dv)
        if n_chunks == 1:
            s = _dot_nt(qt_ref[:, lanes], kt_ref[:, lanes])
            p = jnp.where(causal, s, 0.0).astype(BF16)
            o = _dot(p, vb_ref[:, vl])
        else:
            parts = []
            for hf in range(2):
                rows = slice(hf * half, (hf + 1) * half)
                d = _dot_nt(qt_ref[rows, lanes], kt_ref[rows, lanes])
                od = _dot_nt(qo_ref[rows, lanes], ko_ref[rows, lanes])
                p = jnp.where(same_chunk, d, jnp.where(later_chunk, od, 0.0)).astype(BF16)
                oh = _dot(p, vb_ref[rows, vl])
                if hf == 1:
                    cross = _dot_nt(ql_ref[:, lanes], kl_ref[:, lanes]).astype(BF16)
                    oh = oh + _dot(cross, vb_ref[0:half, vl])
                parts.append(oh)
            o = jnp.concatenate(parts, axis=0)
        st = st_ref[h]
        o = o + _dot_nt(qi_ref[:, lanes], st.astype(BF16))
        st_ref[h] = st * jnp.exp(tile_total[:, lanes]) + _dot_tn(vb_ref[:, vl], ks_ref[:, lanes])
        mixed_ref[:, out_off + h * dv:out_off + (h + 1) * dv] = _rms(o, norm_w)


def _mixer_safe(q_ref, k_ref, v_ref, b_ref, st_ref, norm_w, mixed_ref, out_off, dv, segment):
    row16 = lax.broadcasted_iota(jnp.int32, (SAFE_BLOCK, 1), 0)

    def block_body(sb, carry):
        s0 = pl.multiple_of(sb * SAFE_BLOCK, SAFE_BLOCK)
        rows_b = pl.ds(s0, SAFE_BLOCK)
        before = pl.multiple_of(jnp.maximum(s0 - 8, 0), 8)
        prev = b_ref[pl.ds(before, 8), :][7:8, :]
        prev = jnp.where(s0 % segment == 0, 0.0, prev)
        for h in range(N_HEADS):
            lanes = slice(h * HEAD_K, (h + 1) * HEAD_K)
            vl = slice(h * dv, (h + 1) * dv)
            bl = b_ref[rows_b, lanes] - prev[:, lanes]
            q = q_ref[rows_b, lanes]
            k = k_ref[rows_b, lanes]
            v = v_ref[rows_b, vl]
            o = jnp.zeros((SAFE_BLOCK, dv), F32)
            for j in range(SAFE_BLOCK):
                w = jnp.exp(jnp.minimum(bl - bl[j:j + 1, :], 0.0))
                sc = jnp.sum(q * k[j:j + 1, :] * w, axis=-1, keepdims=True)
                o = o + jnp.where(row16 >= j, sc, 0.0) * v[j:j + 1, :]
            st = st_ref[h]
            o = o + _dot_nt((q * jnp.exp(bl)).astype(BF16), st.astype(BF16))
            b_end = bl[SAFE_BLOCK - 1:SAFE_BLOCK, :]
            kl = (k * jnp.exp(b_end - bl)).astype(BF16)
            st_ref[h] = st * jnp.exp(b_end) + _dot_tn(v.astype(BF16), kl)
            mixed_ref[rows_b, out_off + h * dv:out_off + (h + 1) * dv] = _rms(o, norm_w)
        return carry

    lax.fori_loop(0, SEQ_TILE // SAFE_BLOCK, block_body, 0)


def _recurrent_mixer(q_ref, k_ref, v_ref, b_ref, st_ref, norm_w, mixed_ref, out_off, dv,
                     n_chunks, operand_refs):
    chunk = SEQ_TILE // n_chunks
    worst = jnp.zeros((1, KEY_WIDTH), F32)
    for c in range(n_chunks):
        r = b_ref[c * chunk + chunk // 2 - 1:c * chunk + chunk // 2, :]
        total = b_ref[(c + 1) * chunk - 1:(c + 1) * chunk, :]
        worst = jnp.maximum(worst, jnp.maximum(-r, r - total))
    fast_ok = jnp.max(worst) < FAST_PATH_LIMIT

    @pl.when(fast_ok)
    def _fast():
        _mixer_fast(q_ref, k_ref, v_ref, b_ref, st_ref, norm_w, mixed_ref, out_off, dv,
                    n_chunks, *operand_refs)

    @pl.when(jnp.logical_not(fast_ok))
    def _safe():
        _mixer_safe(q_ref, k_ref, v_ref, b_ref, st_ref, norm_w, mixed_ref, out_off, dv, chunk)


def _layer_kernel(x_ref, mem_ref, norm_w_ref, w_main_ref, w_lr_ref, w_up_ref, b_gate_ref,
                  gla_norm_ref, lb_ref, hgrn_norm_ref, mem_norm_ref, w_mem_ref, xattn_norm_ref,
                  w_out_ref, final_norm_ref,
                  out_ref,
                  h_ref, q_ref, k_ref, b_ref, v_ref, mixed_ref, gated_ref,
                  mk_ref, mv_ref, gla_st_ref, hgrn_st_ref,
                  qt_ref, kt_ref, qo_ref, ko_ref, qi_ref, ks_ref, ql_ref, kl_ref, vb_ref,
                  *, layer, final_norm):
    t = pl.program_id(1)
    operand_refs = (qt_ref, kt_ref, qo_ref, ko_ref, qi_ref, ks_ref, ql_ref, kl_ref, vb_ref)

    @pl.when(t == 0)
    def _start_of_sequence():
        gla_st_ref[...] = jnp.zeros_like(gla_st_ref)
        hgrn_st_ref[...] = jnp.zeros_like(hgrn_st_ref)
        m = _rms(mem_ref[0], mem_norm_ref[...]).astype(BF16)
        mk_ref[...] = _dot(m, w_mem_ref[:, :XATTN_WIDTH]).astype(BF16)
        mv_ref[...] = _dot(m, w_mem_ref[:, XATTN_WIDTH:]).astype(BF16)

    x = x_ref[0]
    h_ref[...] = _rms(x, norm_w_ref[...]).astype(BF16)

    def proj(off, width):
        return _dot(h_ref[...], w_main_ref[:, off:off + width])

    low_rank = _dot(h_ref[...], w_lr_ref[...]).astype(BF16)
    z = _dot(low_rank, w_up_ref[...]) + b_gate_ref[...]
    b_ref[...] = _segment_cumsum(_log_sigmoid(z) * (1.0 / GLA_GATE_TAU),
                                 SEQ_TILE // GLA_CHUNKS)
    q_ref[...] = proj(OFF_GQ, KEY_WIDTH) * (HEAD_K ** -0.5)
    k_ref[...] = proj(OFF_GK, KEY_WIDTH)
    v_ref[...] = proj(OFF_GV, GLA_VAL_WIDTH)
    _recurrent_mixer(q_ref, k_ref, v_ref, b_ref, gla_st_ref, gla_norm_ref[...], mixed_ref,
                     0, GLA_HEAD_V, GLA_CHUNKS, operand_refs)

    lbw = lb_ref[...]
    lb_max = jnp.max(lbw, axis=0, keepdims=True)
    lb_e = jnp.exp(lbw - lb_max)
    lb_p = lb_e / jnp.sum(lb_e, axis=0, keepdims=True)
    lb = jnp.sum(lb_p[:layer + 1], axis=0, keepdims=True) - lb_p[0:1]
    zf = proj(OFF_HF, KEY_WIDTH)
    log_f = jnp.logaddexp(jnp.log(jnp.maximum(lb, LB_FLOOR)),
                          jnp.log1p(-lb) + _log_sigmoid(zf))
    b_ref[...] = _segment_cumsum(log_f, SEQ_TILE // HGRN_CHUNKS)
    k_ref[...] = (1.0 - lb) * jax.nn.sigmoid(-zf)
    q_ref[...] = proj(OFF_HQ, KEY_WIDTH)
    v_ref[:, :HGRN_VAL_WIDTH] = proj(OFF_HI, HGRN_VAL_WIDTH)
    _recurrent_mixer(q_ref, k_ref, v_ref, b_ref, hgrn_st_ref, hgrn_norm_ref[...], mixed_ref,
                     GLA_VAL_WIDTH, HGRN_HEAD_V, HGRN_CHUNKS, operand_refs)

    xoff = GLA_VAL_WIDTH + HGRN_VAL_WIDTH
    qt_ref[...] = proj(OFF_XQ, XATTN_WIDTH).astype(BF16)
    for hd in range(N_HEADS):
        lanes = slice(hd * HEAD_K, (hd + 1) * HEAD_K)
        s = _dot_nt(qt_ref[:, lanes], mk_ref[:, lanes]) * (HEAD_K ** -0.5)
        p = jnp.exp(s - jnp.max(s, axis=-1, keepdims=True))
        o = _dot(p.astype(BF16), mv_ref[:, lanes]) / jnp.sum(p, axis=-1, keepdims=True)
        mixed_ref[:, xoff + hd * HEAD_K:xoff + (hd + 1) * HEAD_K] = _rms(o, xattn_norm_ref[...])

    gate_block = 512
    for c0 in range(0, D_MIX, gate_block):
        g = proj(OFF_GATE + c0, gate_block)
        gated_ref[:, c0:c0 + gate_block] = (
            mixed_ref[:, c0:c0 + gate_block] * (g * jax.nn.sigmoid(g))).astype(BF16)
    y = x + _dot(gated_ref[...], w_out_ref[...])
    if final_norm:
        y = _rms(y, final_norm_ref[...])
    out_ref[0] = y


def _layer_call(x, mem, params, layer, final_norm):
    batch, seq, _ = x.shape
    n_tiles = seq // SEQ_TILE
    const2 = lambda b, t: (0, 0)

    def whole(arr):
        return pl.BlockSpec(arr.shape, const2, pipeline_mode=pl.Buffered(1))

    in_specs = [
        pl.BlockSpec((1, SEQ_TILE, D_MODEL), lambda b, t: (b, t, 0)),
        pl.BlockSpec((1, MEM_LEN, D_MODEL), lambda b, t: (b, 0, 0)),
    ] + [whole(p) for p in params]
    operand = pltpu.VMEM((SEQ_TILE, KEY_WIDTH), BF16)
    half_operand = pltpu.VMEM((SEQ_TILE // 2, KEY_WIDTH), BF16)
    scratch = [
        pltpu.VMEM((SEQ_TILE, D_MODEL), BF16),
        pltpu.VMEM((SEQ_TILE, KEY_WIDTH), F32),
        pltpu.VMEM((SEQ_TILE, KEY_WIDTH), F32),
        pltpu.VMEM((SEQ_TILE, KEY_WIDTH), F32),
        pltpu.VMEM((SEQ_TILE, GLA_VAL_WIDTH), F32),
        pltpu.VMEM((SEQ_TILE, D_MIX), F32),
        pltpu.VMEM((SEQ_TILE, D_MIX), BF16),
        pltpu.VMEM((MEM_LEN, XATTN_WIDTH), BF16),
        pltpu.VMEM((MEM_LEN, XATTN_WIDTH), BF16),
        pltpu.VMEM((N_HEADS, GLA_HEAD_V, HEAD_K), F32),
        pltpu.VMEM((N_HEADS, HGRN_HEAD_V, HEAD_K), F32),
        operand, operand, operand, operand, operand, operand,
        half_operand, half_operand,
        pltpu.VMEM((SEQ_TILE, GLA_VAL_WIDTH), BF16),
    ]
    return pl.pallas_call(
        functools.partial(_layer_kernel, layer=layer, final_norm=final_norm),
        out_shape=jax.ShapeDtypeStruct(x.shape, x.dtype),
        grid=(batch, n_tiles),
        in_specs=in_specs,
        out_specs=pl.BlockSpec((1, SEQ_TILE, D_MODEL), lambda b, t: (b, t, 0)),
        scratch_shapes=scratch,
        compiler_params=pltpu.CompilerParams(
            dimension_semantics=("arbitrary", "arbitrary"),
            vmem_limit_bytes=VMEM_LIMIT_BYTES),
        name=f"hybrid_layer_{layer}",
    )(x, mem, *params)


def kernel(x, mem, norm_w, w_in, gla_w_gate_up, gla_b_gate, gla_norm_w, hgrn_lower_bounds,
           hgrn_norm_w, mem_norm_w, w_mem_kv, xattn_norm_w, w_out, final_norm_w):
    assert x.shape[1] % SEQ_TILE == 0
    lr0 = OFF_HQ
    lr1 = lr0 + GLA_LOWRANK
    row2 = lambda v: v.reshape(1, -1).astype(F32)
    for layer in range(DEPTH):
        w = w_in[layer]
        w_main = jnp.concatenate([w[:, :lr0], w[:, lr1:]], axis=1).astype(BF16)
        w_lr = jnp.pad(w[:, lr0:lr1], ((0, 0), (0, LOWRANK_PAD - GLA_LOWRANK))).astype(BF16)
        w_up = jnp.pad(gla_w_gate_up[layer],
                       ((0, LOWRANK_PAD - GLA_LOWRANK), (0, 0))).astype(BF16)
        params = (
            row2(norm_w[layer]), w_main, w_lr, w_up, row2(gla_b_gate[layer]),
            row2(gla_norm_w[layer]), hgrn_lower_bounds.astype(F32), row2(hgrn_norm_w[layer]),
            row2(mem_norm_w[layer]), w_mem_kv[layer].astype(BF16), row2(xattn_norm_w[layer]),
            w_out[layer].astype(BF16), row2(final_norm_w),
        )
        x = _layer_call(x, mem, params, layer, final_norm=(layer == DEPTH - 1))
    return x
```

```python
import functools

import jax
import jax.numpy as jnp
from jax import lax
from jax.experimental import pallas as pl
from jax.experimental.pallas import tpu as pltpu

F32 = jnp.float32
BF16 = jnp.bfloat16

D_MODEL = 1024
DEPTH = 2
MEM_LEN = 256
N_HEADS = 4
HEAD_K = 128
GLA_HEAD_V = 256
HGRN_HEAD_V = 128
KEY_WIDTH = N_HEADS * HEAD_K
GLA_VAL_WIDTH = N_HEADS * GLA_HEAD_V
HGRN_VAL_WIDTH = N_HEADS * HGRN_HEAD_V
XATTN_WIDTH = N_HEADS * HEAD_K
D_MIX = 2048
GLA_LOWRANK = 16
GLA_GATE_TAU = 16.0
LB_FLOOR = 1e-30
NORM_EPS = 1e-6

OFF_GQ, OFF_GK, OFF_GV = 0, 512, 1024
OFF_HQ, OFF_HF, OFF_HI = 2048, 2560, 3072
OFF_XQ, OFF_GATE = 3584, 4096
LOWRANK_PAD = 128

SEQ_TILE = 256
GLA_CHUNKS = 1
HGRN_CHUNKS = 4
DECAY_ROWS = 64
SAFE_BLOCK = 16
FAST_PATH_LIMIT = 60.0

VMEM_LIMIT_BYTES = 56 * 1024 * 1024


def _dot(a, b):
    return jnp.dot(a, b, preferred_element_type=F32)


def _dot_nt(a, b):
    return lax.dot_general(a, b, (((1,), (1,)), ((), ())), preferred_element_type=F32)


def _dot_tn(a, b):
    return lax.dot_general(a, b, (((0,), (0,)), ((), ())), preferred_element_type=F32)


def _rms(x, w):
    return x * lax.rsqrt(jnp.mean(x * x, axis=-1, keepdims=True) + NORM_EPS) * w


def _store_split(x, hi_ref, lo_ref, rows):
    hi = x.astype(BF16)
    hi_ref[rows, :] = hi
    lo_ref[rows, :] = (x - hi.astype(F32)).astype(BF16)


def _mixer_fast(q_ref, k_ref, vb_ref, b_ref, st_ref, norm_w, mixed_ref, out_off, dv, n_chunks,
                operand_refs, fill):
    assert n_chunks in (1, 4)
    qt_ref, kt_ref, qo_ref, ko_ref, qi_ref, ks_ref, ql_ref, kl_ref, p_ref = operand_refs
    chunk = SEQ_TILE // n_chunks
    half = SEQ_TILE // 2
    tot = [b_ref[(c + 1) * chunk - 1:(c + 1) * chunk, :] for c in range(n_chunks)]
    ref = [b_ref[c * chunk + chunk // 2 - 1:c * chunk + chunk // 2, :] for c in range(n_chunks)]
    before = [jnp.zeros_like(tot[0])]
    for c in range(n_chunks):
        before.append(before[-1] + tot[c])
    tile_total = before[n_chunks]
    if n_chunks == 4:
        e_in = [None] + [jnp.exp(before[c]) for c in range(1, 4)]
        e_out = [jnp.exp(tile_total - before[c + 1]) for c in range(3)] + [None]
        e_tot1 = jnp.exp(tot[1])
        e_tot2 = jnp.exp(tot[2])

    for c in range(n_chunks):
        rows = slice(c * chunk, (c + 1) * chunk)
        for h in range(N_HEADS):
            lanes = slice(h * HEAD_K, (h + 1) * HEAD_K)
            b = b_ref[rows, lanes]
            q = q_ref[rows, lanes]
            k = k_ref[rows, lanes]
            r = ref[c][:, lanes]
            qe = q * jnp.exp(b)
            ke = k * jnp.exp(tot[c][:, lanes] - b)
            qt_ref[rows, lanes] = (q * jnp.exp(b - r)).astype(BF16)
            kt_ref[rows, lanes] = (k * jnp.exp(r - b)).astype(BF16)
            if n_chunks == 1:
                qi_ref[rows, lanes] = qe.astype(BF16)
                ks_ref[rows, lanes] = ke.astype(BF16)
            else:
                qo_ref[rows, lanes] = qe.astype(BF16)
                ko_ref[rows, lanes] = ke.astype(BF16)
                qi = qe if c == 0 else qe * e_in[c][:, lanes]
                ks = ke if c == 3 else ke * e_out[c][:, lanes]
                qi_ref[rows, lanes] = qi.astype(BF16)
                ks_ref[rows, lanes] = ks.astype(BF16)
                if c == 0:
                    kl_ref[0:chunk, lanes] = (ke * e_tot1[:, lanes]).astype(BF16)
                elif c == 1:
                    kl_ref[chunk:half, lanes] = ke.astype(BF16)
                elif c == 2:
                    ql_ref[0:chunk, lanes] = qe.astype(BF16)
                else:
                    ql_ref[chunk:half, lanes] = (qe * e_tot2[:, lanes]).astype(BF16)
    fill()

    if n_chunks == 1:
        row = lax.broadcasted_iota(jnp.int32, (SEQ_TILE, SEQ_TILE), 0)
        col = lax.broadcasted_iota(jnp.int32, (SEQ_TILE, SEQ_TILE), 1)
        causal = row >= col
        for h in range(N_HEADS):
            lanes = slice(h * HEAD_K, (h + 1) * HEAD_K)
            vl = slice(h * dv, (h + 1) * dv)
            s = _dot_nt(qt_ref[:, lanes], kt_ref[:, lanes])
            p = jnp.where(causal, s, 0.0).astype(BF16)
            st = st_ref[h]
            o = _dot(p, vb_ref[:, vl]) + _dot_nt(qi_ref[:, lanes], st.astype(BF16))
            st_ref[h] = (st * jnp.exp(tile_total[:, lanes])
                         + _dot_tn(vb_ref[:, vl], ks_ref[:, lanes]))
            mixed_ref[:, out_off + h * dv:out_off + (h + 1) * dv] = _rms(o, norm_w)
            if h % 2 == 1:
                fill()
        return

    row = lax.broadcasted_iota(jnp.int32, (half, half), 0)
    col = lax.broadcasted_iota(jnp.int32, (half, half), 1)
    same_chunk = ((row >= chunk) == (col >= chunk)) & (row >= col)
    later_chunk = (row >= chunk) & (col < chunk)
    for h in range(N_HEADS):
        lanes = slice(h * HEAD_K, (h + 1) * HEAD_K)
        for hf in range(2):
            rows = slice(hf * half, (hf + 1) * half)
            d = _dot_nt(qt_ref[rows, lanes], kt_ref[rows, lanes])
            od = _dot_nt(qo_ref[rows, lanes], ko_ref[rows, lanes])
            p_ref[h, rows, rows] = jnp.where(
                same_chunk, d, jnp.where(later_chunk, od, 0.0)).astype(BF16)
        p_ref[h, half:, 0:half] = _dot_nt(ql_ref[:, lanes], kl_ref[:, lanes]).astype(BF16)
    fill()
    for h in range(N_HEADS):
        lanes = slice(h * HEAD_K, (h + 1) * HEAD_K)
        vl = slice(h * dv, (h + 1) * dv)
        ol = slice(out_off + h * dv, out_off + (h + 1) * dv)
        st = st_ref[h]
        st_b = st.astype(BF16)
        mixed_ref[0:half, ol] = (_dot(p_ref[h, 0:half, 0:half], vb_ref[0:half, vl])
                                 + _dot_nt(qi_ref[0:half, lanes], st_b))
        mixed_ref[half:, ol] = (_dot(p_ref[h, half:, :], vb_ref[:, vl])
                                + _dot_nt(qi_ref[half:, lanes], st_b))
        st_ref[h] = st * jnp.exp(tile_total[:, lanes]) + _dot_tn(vb_ref[:, vl], ks_ref[:, lanes])
    fill()
    for h in range(N_HEADS):
        ol = slice(out_off + h * dv, out_off + (h + 1) * dv)
        mixed_ref[:, ol] = _rms(mixed_ref[:, ol], norm_w)


def _mixer_safe(q_ref, k_ref, vb_ref, b_ref, st_ref, norm_w, mixed_ref, out_off, dv, segment):
    row16 = lax.broadcasted_iota(jnp.int32, (SAFE_BLOCK, 1), 0)

    def block_body(sb, carry):
        s0 = pl.multiple_of(sb * SAFE_BLOCK, SAFE_BLOCK)
        rows_b = pl.ds(s0, SAFE_BLOCK)
        before = pl.multiple_of(jnp.maximum(s0 - 8, 0), 8)
        prev = b_ref[pl.ds(before, 8), :][7:8, :]
        prev = jnp.where(s0 % segment == 0, 0.0, prev)
        for h in range(N_HEADS):
            lanes = slice(h * HEAD_K, (h + 1) * HEAD_K)
            vl = slice(h * dv, (h + 1) * dv)
            bl = b_ref[rows_b, lanes] - prev[:, lanes]
            q = q_ref[rows_b, lanes]
            k = k_ref[rows_b, lanes]
            vb = vb_ref[rows_b, vl]
            v = vb.astype(F32)
            o = jnp.zeros((SAFE_BLOCK, dv), F32)
            for j in range(SAFE_BLOCK):
                w = jnp.exp(jnp.minimum(bl - bl[j:j + 1, :], 0.0))
                sc = jnp.sum(q * k[j:j + 1, :] * w, axis=-1, keepdims=True)
                o = o + jnp.where(row16 >= j, sc, 0.0) * v[j:j + 1, :]
            st = st_ref[h]
            o = o + _dot_nt((q * jnp.exp(bl)).astype(BF16), st.astype(BF16))
            b_end = bl[SAFE_BLOCK - 1:SAFE_BLOCK, :]
            kl = (k * jnp.exp(b_end - bl)).astype(BF16)
            st_ref[h] = st * jnp.exp(b_end) + _dot_tn(vb, kl)
            mixed_ref[rows_b, out_off + h * dv:out_off + (h + 1) * dv] = _rms(o, norm_w)
        return carry

    lax.fori_loop(0, SEQ_TILE // SAFE_BLOCK, block_body, 0)


def _fast_path_ok(b_ref, n_chunks):
    chunk = SEQ_TILE // n_chunks
    worst = jnp.zeros((1, KEY_WIDTH), F32)
    for c in range(n_chunks):
        r = b_ref[c * chunk + chunk // 2 - 1:c * chunk + chunk // 2, :]
        total = b_ref[(c + 1) * chunk - 1:(c + 1) * chunk, :]
        worst = jnp.maximum(worst, jnp.maximum(-r, r - total))
    return jnp.max(worst) < FAST_PATH_LIMIT


def _layer_kernel(x_ref, mem_ref, tri_tile_ref, tri_chunk_ref,
                  norm_w_ref, w_main_ref, w_lr_ref, w_up_ref, b_gate_ref,
                  gla_norm_ref, lb_ref, hgrn_norm_ref, mem_norm_ref, w_mem_ref, xattn_norm_ref,
                  w_out_ref, final_norm_ref,
                  out_ref,
                  h_ref, zg_ref, zf_ref, gq_ref, gk_ref, gb_ref, hq_ref, hk_ref, hb_ref,
                  gv_ref, hv_ref, ghi_ref, glo_ref, hhi_ref, hlo_ref,
                  mixed_ref, gs_ref, xq_ref, mk_ref, mv_ref,
                  gla_st_ref, hgrn_st_ref, gla_prev_ref, hgrn_prev_ref,
                  g_qt_ref, g_kt_ref, g_qi_ref, g_ks_ref,
                  h_qt_ref, h_kt_ref, h_qo_ref, h_ko_ref, h_qi_ref, h_ks_ref, h_ql_ref, h_kl_ref,
                  p_ref,
                  *, layer, final_norm):
    t = pl.program_id(1)
    gla_operands = (g_qt_ref, g_kt_ref, None, None, g_qi_ref, g_ks_ref, None, None, None)
    hgrn_operands = (h_qt_ref, h_kt_ref, h_qo_ref, h_ko_ref, h_qi_ref, h_ks_ref, h_ql_ref,
                     h_kl_ref, p_ref)

    @pl.when(t == 0)
    def _start_of_sequence():
        gla_st_ref[...] = jnp.zeros_like(gla_st_ref)
        hgrn_st_ref[...] = jnp.zeros_like(hgrn_st_ref)
        m = _rms(mem_ref[0], mem_norm_ref[...]).astype(BF16)
        mk_ref[...] = _dot(m, w_mem_ref[:, :XATTN_WIDTH]).astype(BF16)
        mv_ref[...] = _dot(m, w_mem_ref[:, XATTN_WIDTH:]).astype(BF16)

    x = x_ref[0]
    h_ref[...] = _rms(x, norm_w_ref[...]).astype(BF16)

    def proj(off, width):
        return _dot(h_ref[...], w_main_ref[:, off:off + width])

    low_rank = _dot(h_ref[...], w_lr_ref[...]).astype(BF16)
    zg_ref[...] = _dot(low_rank, w_up_ref[...]) + b_gate_ref[...]
    zf_ref[...] = proj(OFF_HF, KEY_WIDTH)

    lbw = lb_ref[...]
    lb_e = jnp.exp(lbw - jnp.max(lbw, axis=0, keepdims=True))
    lb_p = lb_e / jnp.sum(lb_e, axis=0, keepdims=True)
    lb = jnp.sum(lb_p[:layer + 1], axis=0, keepdims=True) - lb_p[0:1]
    log_lb = jnp.log(jnp.maximum(lb, LB_FLOOR))
    log_1m_lb = jnp.log1p(-lb)
    one_m_lb = 1.0 - lb

    def gla_decay(j):
        rows = slice(j * DECAY_ROWS, (j + 1) * DECAY_ROWS)
        z = zg_ref[rows, :]
        ls = jnp.minimum(z, 0.0) - jnp.log(1.0 + jnp.exp(-jnp.abs(z)))
        _store_split(ls * (1.0 / GLA_GATE_TAU), ghi_ref, glo_ref, rows)

    def hgrn_decay(j):
        rows = slice(j * DECAY_ROWS, (j + 1) * DECAY_ROWS)
        z = zf_ref[rows, :]
        e = jnp.exp(-jnp.abs(z))
        u = 1.0 + e
        c = log_1m_lb + (jnp.minimum(z, 0.0) - jnp.log(u))
        log_f = jnp.maximum(log_lb, c) + jnp.log(1.0 + jnp.exp(-jnp.abs(log_lb - c)))
        _store_split(log_f, hhi_ref, hlo_ref, rows)
        hk_ref[rows, :] = one_m_lb * (jnp.where(z >= 0.0, e, 1.0) / u)

    gq_ref[...] = proj(OFF_GQ, KEY_WIDTH) * (HEAD_K ** -0.5)
    gla_decay(0)
    hgrn_decay(0)
    gk_ref[...] = proj(OFF_GK, KEY_WIDTH)
    gla_decay(1)
    hgrn_decay(1)
    gv_ref[...] = proj(OFF_GV, GLA_VAL_WIDTH).astype(BF16)
    gla_decay(2)
    hgrn_decay(2)
    hq_ref[...] = proj(OFF_HQ, KEY_WIDTH)
    gla_decay(3)
    hgrn_decay(3)
    hv_ref[...] = proj(OFF_HI, HGRN_VAL_WIDTH).astype(BF16)

    gb_ref[...] = _dot(tri_tile_ref[...], ghi_ref[...]) + _dot(tri_tile_ref[...], glo_ref[...])
    hb_ref[...] = _dot(tri_chunk_ref[...], hhi_ref[...]) + _dot(tri_chunk_ref[...], hlo_ref[...])
    gla_ok = _fast_path_ok(gb_ref, GLA_CHUNKS)
    hgrn_ok = _fast_path_ok(hb_ref, HGRN_CHUNKS)

    xoff = GLA_VAL_WIDTH + HGRN_VAL_WIDTH
    gate_block = 512

    def gate_piece(c0):
        def emit():
            g = proj(OFF_GATE + c0, gate_block)
            gs_ref[:, c0:c0 + gate_block] = g * jax.nn.sigmoid(g)
        return emit

    def xq_piece():
        xq_ref[...] = proj(OFF_XQ, XATTN_WIDTH).astype(BF16)

    def xattn_piece(hd):
        def emit():
            lanes = slice(hd * HEAD_K, (hd + 1) * HEAD_K)
            s = _dot_nt(xq_ref[:, lanes], mk_ref[:, lanes]) * (HEAD_K ** -0.5)
            p = jnp.exp(s - jnp.max(s, axis=-1, keepdims=True))
            o = _dot(p.astype(BF16), mv_ref[:, lanes]) / jnp.sum(p, axis=-1, keepdims=True)
            mixed_ref[:, xoff + hd * HEAD_K:xoff + (hd + 1) * HEAD_K] = _rms(
                o, xattn_norm_ref[...])
        return emit

    pieces = iter([gate_piece(c0) for c0 in range(0, D_MIX, gate_block)] + [xq_piece]
                  + [xattn_piece(hd) for hd in range(N_HEADS)])

    def fill():
        piece = next(pieces, None)
        if piece is not None:
            piece()

    gla_prev_ref[...] = gla_st_ref[...]
    hgrn_prev_ref[...] = hgrn_st_ref[...]
    _mixer_fast(gq_ref, gk_ref, gv_ref, gb_ref, gla_st_ref, gla_norm_ref[...], mixed_ref,
                0, GLA_HEAD_V, GLA_CHUNKS, gla_operands, fill)
    _mixer_fast(hq_ref, hk_ref, hv_ref, hb_ref, hgrn_st_ref, hgrn_norm_ref[...], mixed_ref,
                GLA_VAL_WIDTH, HGRN_HEAD_V, HGRN_CHUNKS, hgrn_operands, fill)
    for piece in pieces:
        piece()

    @pl.when(jnp.logical_not(gla_ok))
    def _gla_pairwise():
        gla_st_ref[...] = gla_prev_ref[...]
        _mixer_safe(gq_ref, gk_ref, gv_ref, gb_ref, gla_st_ref, gla_norm_ref[...], mixed_ref,
                    0, GLA_HEAD_V, SEQ_TILE // GLA_CHUNKS)

    @pl.when(jnp.logical_not(hgrn_ok))
    def _hgrn_pairwise():
        hgrn_st_ref[...] = hgrn_prev_ref[...]
        _mixer_safe(hq_ref, hk_ref, hv_ref, hb_ref, hgrn_st_ref, hgrn_norm_ref[...], mixed_ref,
                    GLA_VAL_WIDTH, HGRN_HEAD_V, SEQ_TILE // HGRN_CHUNKS)

    y = x
    for c0 in range(0, D_MIX, gate_block):
        cols = slice(c0, c0 + gate_block)
        gated = (mixed_ref[:, cols] * gs_ref[:, cols]).astype(BF16)
        y = y + _dot(gated, w_out_ref[cols, :])
    if final_norm:
        y = _rms(y, final_norm_ref[...])
    out_ref[0] = y


def _layer_call(x, mem, params, layer, final_norm):
    batch, seq, _ = x.shape
    n_tiles = seq // SEQ_TILE
    const2 = lambda b, t: (0, 0)

    def whole(arr):
        return pl.BlockSpec(arr.shape, const2, pipeline_mode=pl.Buffered(1))

    in_specs = [
        pl.BlockSpec((1, SEQ_TILE, D_MODEL), lambda b, t: (b, t, 0)),
        pl.BlockSpec((1, MEM_LEN, D_MODEL), lambda b, t: (b, 0, 0)),
    ] + [whole(p) for p in params]
    key_f32 = pltpu.VMEM((SEQ_TILE, KEY_WIDTH), F32)
    key_bf16 = pltpu.VMEM((SEQ_TILE, KEY_WIDTH), BF16)
    half_bf16 = pltpu.VMEM((SEQ_TILE // 2, KEY_WIDTH), BF16)
    gla_state = pltpu.VMEM((N_HEADS, GLA_HEAD_V, HEAD_K), F32)
    hgrn_state = pltpu.VMEM((N_HEADS, HGRN_HEAD_V, HEAD_K), F32)
    scratch = [
        pltpu.VMEM((SEQ_TILE, D_MODEL), BF16),
        key_f32, key_f32,
        key_f32, key_f32, key_f32,
        key_f32, key_f32, key_f32,
        pltpu.VMEM((SEQ_TILE, GLA_VAL_WIDTH), BF16),
        pltpu.VMEM((SEQ_TILE, HGRN_VAL_WIDTH), BF16),
        key_bf16, key_bf16, key_bf16, key_bf16,
        pltpu.VMEM((SEQ_TILE, D_MIX), F32),
        pltpu.VMEM((SEQ_TILE, D_MIX), F32),
        key_bf16,
        pltpu.VMEM((MEM_LEN, XATTN_WIDTH), BF16),
        pltpu.VMEM((MEM_LEN, XATTN_WIDTH), BF16),
        gla_state, hgrn_state,
        gla_state, hgrn_state,
        key_bf16, key_bf16, key_bf16, key_bf16,
        key_bf16, key_bf16, key_bf16, key_bf16, key_bf16, key_bf16,
        half_bf16, half_bf16,
        pltpu.VMEM((N_HEADS, SEQ_TILE, SEQ_TILE), BF16),
    ]
    return pl.pallas_call(
        functools.partial(_layer_kernel, layer=layer, final_norm=final_norm),
        out_shape=jax.ShapeDtypeStruct(x.shape, x.dtype),
        grid=(batch, n_tiles),
        in_specs=in_specs,
        out_specs=pl.BlockSpec((1, SEQ_TILE, D_MODEL), lambda b, t: (b, t, 0)),
        scratch_shapes=scratch,
        compiler_params=pltpu.CompilerParams(
            dimension_semantics=("arbitrary", "arbitrary"),
            vmem_limit_bytes=VMEM_LIMIT_BYTES),
        name=f"hybrid_layer_{layer}",
    )(x, mem, *params)


def _lower_tri_ones(segment):
    idx = jnp.arange(SEQ_TILE)
    keep = (idx[None, :] <= idx[:, None]) & ((idx[None, :] // segment) == (idx[:, None] // segment))
    return keep.astype(BF16)


def kernel(x, mem, norm_w, w_in, gla_w_gate_up, gla_b_gate, gla_norm_w, hgrn_lower_bounds,
           hgrn_norm_w, mem_norm_w, w_mem_kv, xattn_norm_w, w_out, final_norm_w):
    assert x.shape[1] % SEQ_TILE == 0
    lr0 = OFF_HQ
    lr1 = lr0 + GLA_LOWRANK
    row2 = lambda v: v.reshape(1, -1).astype(F32)
    tri_tile = _lower_tri_ones(SEQ_TILE // GLA_CHUNKS)
    tri_chunk = _lower_tri_ones(SEQ_TILE // HGRN_CHUNKS)
    for layer in range(DEPTH):
        w = w_in[layer]
        w_main = jnp.concatenate([w[:, :lr0], w[:, lr1:]], axis=1).astype(BF16)
        w_lr = jnp.pad(w[:, lr0:lr1], ((0, 0), (0, LOWRANK_PAD - GLA_LOWRANK))).astype(BF16)
        w_up = jnp.pad(gla_w_gate_up[layer],
                       ((0, LOWRANK_PAD - GLA_LOWRANK), (0, 0))).astype(BF16)
        params = (
            tri_tile, tri_chunk,
            row2(norm_w[layer]), w_main, w_lr, w_up, row2(gla_b_gate[layer]),
            row2(gla_norm_w[layer]), hgrn_lower_bounds.astype(F32), row2(hgrn_norm_w[layer]),
            row2(mem_norm_w[layer]), w_mem_kv[layer].astype(BF16), row2(xattn_norm_w[layer]),
            w_out[layer].astype(BF16), row2(final_norm_w),
        )
        x = _layer_call(x, mem, params, layer, final_norm=(layer == DEPTH - 1))
    return x
```

```python
import functools

import jax
import jax.numpy as jnp
from jax import lax
from jax.experimental import pallas as pl
from jax.experimental.pallas import tpu as pltpu

F32 = jnp.float32
BF16 = jnp.bfloat16

D_MODEL = 1024
DEPTH = 2
MEM_LEN = 256
N_HEADS = 4
HEAD_K = 128
GLA_HEAD_V = 256
HGRN_HEAD_V = 128
KEY_WIDTH = N_HEADS * HEAD_K
GLA_VAL_WIDTH = N_HEADS * GLA_HEAD_V
HGRN_VAL_WIDTH = N_HEADS * HGRN_HEAD_V
XATTN_WIDTH = N_HEADS * HEAD_K
D_MIX = 2048
GLA_LOWRANK = 16
GLA_GATE_TAU = 16.0
LB_FLOOR = 1e-30
NORM_EPS = 1e-6

LOWRANK_PAD = 128
OFF_GQ, OFF_GK, OFF_GV = 0, 512, 1024
OFF_HQ, OFF_HF, OFF_LR = 2048, 2560, 3072
OFF_HI = OFF_LR + LOWRANK_PAD
OFF_XQ = OFF_HI + HGRN_VAL_WIDTH
OFF_GATE = OFF_XQ + XATTN_WIDTH
SRC_GLR, SRC_HQ, SRC_HI = 2048, 2064, 3088

SEQ_TILE = 256
GLA_CHUNKS = 1
HGRN_CHUNKS = 4
DECAY_ROWS = SEQ_TILE // 4
SAFE_BLOCK = 16
FAST_PATH_LIMIT = 60.0

VMEM_LIMIT_BYTES = 56 * 1024 * 1024


def _dot(a, b):
    return jnp.dot(a, b, preferred_element_type=F32)


def _dot_nt(a, b):
    return lax.dot_general(a, b, (((1,), (1,)), ((), ())), preferred_element_type=F32)


def _dot_tn(a, b):
    return lax.dot_general(a, b, (((0,), (0,)), ((), ())), preferred_element_type=F32)


def _rms(x, w):
    return x * lax.rsqrt(jnp.mean(x * x, axis=-1, keepdims=True) + NORM_EPS) * w


def _store_split(x, hi_ref, lo_ref, rows):
    hi = x.astype(BF16)
    hi_ref[rows, :] = hi
    lo_ref[rows, :] = (x - hi.astype(F32)).astype(BF16)


def _mixer_fast(q_ref, k_ref, vb_ref, b_ref, st_ref, norm_w, mixed_ref, out_off, dv, n_chunks,
                operand_refs, fill):
    assert n_chunks in (1, 4)
    qt_ref, kt_ref, qo_ref, ko_ref, qi_ref, ks_ref, ql_ref, kl_ref, p_ref = operand_refs
    chunk = SEQ_TILE // n_chunks
    half = SEQ_TILE // 2
    tot = [b_ref[(c + 1) * chunk - 1:(c + 1) * chunk, :] for c in range(n_chunks)]
    ref = [b_ref[c * chunk + chunk // 2 - 1:c * chunk + chunk // 2, :] for c in range(n_chunks)]
    before = [jnp.zeros_like(tot[0])]
    for c in range(n_chunks):
        before.append(before[-1] + tot[c])
    tile_total = before[n_chunks]
    if n_chunks == 4:
        e_in = [None] + [jnp.exp(before[c]) for c in range(1, 4)]
        e_out = [jnp.exp(tile_total - before[c + 1]) for c in range(3)] + [None]
        e_tot1 = jnp.exp(tot[1])
        e_tot2 = jnp.exp(tot[2])

    for c in range(n_chunks):
        rows = slice(c * chunk, (c + 1) * chunk)
        for h in range(N_HEADS):
            lanes = slice(h * HEAD_K, (h + 1) * HEAD_K)
            b = b_ref[rows, lanes]
            q = q_ref[rows, lanes]
            k = k_ref[rows, lanes]
            r = ref[c][:, lanes]
            qe = q * jnp.exp(b)
            ke = k * jnp.exp(tot[c][:, lanes] - b)
            qt_ref[rows, lanes] = (q * jnp.exp(b - r)).astype(BF16)
            kt_ref[rows, lanes] = (k * jnp.exp(r - b)).astype(BF16)
            if n_chunks == 1:
                qi_ref[rows, lanes] = qe.astype(BF16)
                ks_ref[rows, lanes] = ke.astype(BF16)
            else:
                qo_ref[rows, lanes] = qe.astype(BF16)
                ko_ref[rows, lanes] = ke.astype(BF16)
                qi = qe if c == 0 else qe * e_in[c][:, lanes]
                ks = ke if c == 3 else ke * e_out[c][:, lanes]
                qi_ref[rows, lanes] = qi.astype(BF16)
                ks_ref[rows, lanes] = ks.astype(BF16)
                if c == 0:
                    kl_ref[0:chunk, lanes] = (ke * e_tot1[:, lanes]).astype(BF16)
                elif c == 1:
                    kl_ref[chunk:half, lanes] = ke.astype(BF16)
                elif c == 2:
                    ql_ref[0:chunk, lanes] = qe.astype(BF16)
                else:
                    ql_ref[chunk:half, lanes] = (qe * e_tot2[:, lanes]).astype(BF16)
    fill()

    if n_chunks == 1:
        row = lax.broadcasted_iota(jnp.int32, (SEQ_TILE, SEQ_TILE), 0)
        col = lax.broadcasted_iota(jnp.int32, (SEQ_TILE, SEQ_TILE), 1)
        causal = row >= col
        for h in range(N_HEADS):
            lanes = slice(h * HEAD_K, (h + 1) * HEAD_K)
            vl = slice(h * dv, (h + 1) * dv)
            s = _dot_nt(qt_ref[:, lanes], kt_ref[:, lanes])
            p = jnp.where(causal, s, 0.0).astype(BF16)
            st = st_ref[h]
            o = _dot(p, vb_ref[:, vl]) + _dot_nt(qi_ref[:, lanes], st.astype(BF16))
            st_ref[h] = (st * jnp.exp(tile_total[:, lanes])
                         + _dot_tn(vb_ref[:, vl], ks_ref[:, lanes]))
            mixed_ref[:, out_off + h * dv:out_off + (h + 1) * dv] = _rms(o, norm_w)
            if h % 2 == 1:
                fill()
        return

    row = lax.broadcasted_iota(jnp.int32, (half, half), 0)
    col = lax.broadcasted_iota(jnp.int32, (half, half), 1)
    same_chunk = ((row >= chunk) == (col >= chunk)) & (row >= col)
    later_chunk = (row >= chunk) & (col < chunk)
    for h in range(N_HEADS):
        lanes = slice(h * HEAD_K, (h + 1) * HEAD_K)
        for hf in range(2):
            rows = slice(hf * half, (hf + 1) * half)
            d = _dot_nt(qt_ref[rows, lanes], kt_ref[rows, lanes])
            od = _dot_nt(qo_ref[rows, lanes], ko_ref[rows, lanes])
            p_ref[h, rows, rows] = jnp.where(
                same_chunk, d, jnp.where(later_chunk, od, 0.0)).astype(BF16)
        p_ref[h, half:, 0:half] = _dot_nt(ql_ref[:, lanes], kl_ref[:, lanes]).astype(BF16)
    fill()
    for h in range(N_HEADS):
        lanes = slice(h * HEAD_K, (h + 1) * HEAD_K)
        vl = slice(h * dv, (h + 1) * dv)
        ol = slice(out_off + h * dv, out_off + (h + 1) * dv)
        st = st_ref[h]
        st_b = st.astype(BF16)
        mixed_ref[0:half, ol] = (_dot(p_ref[h, 0:half, 0:half], vb_ref[0:half, vl])
                                 + _dot_nt(qi_ref[0:half, lanes], st_b))
        mixed_ref[half:, ol] = (_dot(p_ref[h, half:, :], vb_ref[:, vl])
                                + _dot_nt(qi_ref[half:, lanes], st_b))
        st_ref[h] = st * jnp.exp(tile_total[:, lanes]) + _dot_tn(vb_ref[:, vl], ks_ref[:, lanes])
    fill()
    for h in range(N_HEADS):
        ol = slice(out_off + h * dv, out_off + (h + 1) * dv)
        mixed_ref[:, ol] = _rms(mixed_ref[:, ol], norm_w)


def _mixer_safe(q_ref, k_ref, vb_ref, b_ref, st_ref, norm_w, mixed_ref, out_off, dv, segment):
    row16 = lax.broadcasted_iota(jnp.int32, (SAFE_BLOCK, 1), 0)

    def block_body(sb, carry):
        s0 = pl.multiple_of(sb * SAFE_BLOCK, SAFE_BLOCK)
        rows_b = pl.ds(s0, SAFE_BLOCK)
        before = pl.multiple_of(jnp.maximum(s0 - 8, 0), 8)
        prev = b_ref[pl.ds(before, 8), :][7:8, :]
        prev = jnp.where(s0 % segment == 0, 0.0, prev)
        for h in range(N_HEADS):
            lanes = slice(h * HEAD_K, (h + 1) * HEAD_K)
            vl = slice(h * dv, (h + 1) * dv)
            bl = b_ref[rows_b, lanes] - prev[:, lanes]
            q = q_ref[rows_b, lanes]
            k = k_ref[rows_b, lanes]
            vb = vb_ref[rows_b, vl]
            v = vb.astype(F32)
            o = jnp.zeros((SAFE_BLOCK, dv), F32)
            for j in range(SAFE_BLOCK):
                w = jnp.exp(jnp.minimum(bl - bl[j:j + 1, :], 0.0))
                sc = jnp.sum(q * k[j:j + 1, :] * w, axis=-1, keepdims=True)
                o = o + jnp.where(row16 >= j, sc, 0.0) * v[j:j + 1, :]
            st = st_ref[h]
            o = o + _dot_nt((q * jnp.exp(bl)).astype(BF16), st.astype(BF16))
            b_end = bl[SAFE_BLOCK - 1:SAFE_BLOCK, :]
            kl = (k * jnp.exp(b_end - bl)).astype(BF16)
            st_ref[h] = st * jnp.exp(b_end) + _dot_tn(vb, kl)
            mixed_ref[rows_b, out_off + h * dv:out_off + (h + 1) * dv] = _rms(o, norm_w)
        return carry

    lax.fori_loop(0, SEQ_TILE // SAFE_BLOCK, block_body, 0)


def _fast_path_ok(b_ref, n_chunks):
    chunk = SEQ_TILE // n_chunks
    worst = jnp.zeros((1, KEY_WIDTH), F32)
    for c in range(n_chunks):
        r = b_ref[c * chunk + chunk // 2 - 1:c * chunk + chunk // 2, :]
        total = b_ref[(c + 1) * chunk - 1:(c + 1) * chunk, :]
        worst = jnp.maximum(worst, jnp.maximum(-r, r - total))
    return jnp.max(worst) < FAST_PATH_LIMIT


def _piece_order(gates, heads, outs, xq, epilogue):
    return [[gates[0]], [gates[1]], [gates[2]], [outs[0]], [gates[3]], [outs[1]],
            [xq] + heads + [outs[2], outs[3], epilogue]]


def _layer_kernel(x_ref, mem_ref, tri_tile_ref, tri_chunk_ref,
                  norm_w_ref, w_main_ref, w_up_ref, b_gate_ref,
                  gla_norm_ref, lb_ref, hgrn_norm_ref, mem_norm_ref, w_mem_ref, xattn_norm_ref,
                  w_out_ref, final_norm_ref,
                  out_ref,
                  h_ref, zg_ref, zf_ref, gq_ref, gk_ref, gb_ref, hq_ref, hk_ref, hb_ref,
                  gv_ref, hv_ref, ghi_ref, glo_ref, hhi_ref, hlo_ref,
                  mixed_ref, gs_ref, xq_ref, mk_ref, mv_ref,
                  gla_st_ref, hgrn_st_ref, gla_prev_ref, hgrn_prev_ref,
                  g_qt_ref, g_kt_ref, g_qi_ref, g_ks_ref,
                  h_qt_ref, h_kt_ref, h_qo_ref, h_ko_ref, h_qi_ref, h_ks_ref, h_ql_ref, h_kl_ref,
                  p_ref,
                  *, layer, final_norm):
    t = pl.program_id(1)
    gla_operands = (g_qt_ref, g_kt_ref, None, None, g_qi_ref, g_ks_ref, None, None, None)
    hgrn_operands = (h_qt_ref, h_kt_ref, h_qo_ref, h_ko_ref, h_qi_ref, h_ks_ref, h_ql_ref,
                     h_kl_ref, p_ref)

    @pl.when(t == 0)
    def _start_of_sequence():
        gla_st_ref[...] = jnp.zeros_like(gla_st_ref)
        hgrn_st_ref[...] = jnp.zeros_like(hgrn_st_ref)
        m = _rms(mem_ref[0], mem_norm_ref[...]).astype(BF16)
        mk_ref[...] = _dot(m, w_mem_ref[:, :XATTN_WIDTH]).astype(BF16)
        mv_ref[...] = _dot(m, w_mem_ref[:, XATTN_WIDTH:]).astype(BF16)

    x = x_ref[0]
    h_ref[...] = _rms(x, norm_w_ref[...]).astype(BF16)

    def proj(off, width):
        return _dot(h_ref[...], w_main_ref[:, off:off + width])

    zf_lr = proj(OFF_HF, KEY_WIDTH + LOWRANK_PAD)
    zf_ref[...] = zf_lr[:, :KEY_WIDTH]
    low_rank = zf_lr[:, KEY_WIDTH:].astype(BF16)
    zg_ref[...] = _dot(low_rank, w_up_ref[...]) + b_gate_ref[...]

    lbw = lb_ref[...]
    lb_e = jnp.exp(lbw - jnp.max(lbw, axis=0, keepdims=True))
    lb_p = lb_e / jnp.sum(lb_e, axis=0, keepdims=True)
    lb = jnp.sum(lb_p[:layer + 1], axis=0, keepdims=True) - lb_p[0:1]
    log_lb = jnp.log(jnp.maximum(lb, LB_FLOOR))
    log_1m_lb = jnp.log1p(-lb)
    one_m_lb = 1.0 - lb

    def gla_decay(j):
        rows = slice(j * DECAY_ROWS, (j + 1) * DECAY_ROWS)
        z = zg_ref[rows, :]
        ls = jnp.minimum(z, 0.0) - jnp.log(1.0 + jnp.exp(-jnp.abs(z)))
        _store_split(ls * (1.0 / GLA_GATE_TAU), ghi_ref, glo_ref, rows)

    def hgrn_decay(j):
        rows = slice(j * DECAY_ROWS, (j + 1) * DECAY_ROWS)
        z = zf_ref[rows, :]
        e = jnp.exp(-jnp.abs(z))
        u = 1.0 + e
        c = log_1m_lb + (jnp.minimum(z, 0.0) - jnp.log(u))
        log_f = jnp.maximum(log_lb, c) + jnp.log(1.0 + jnp.exp(-jnp.abs(log_lb - c)))
        _store_split(log_f, hhi_ref, hlo_ref, rows)
        hk_ref[rows, :] = one_m_lb * (jnp.where(z >= 0.0, e, 1.0) / u)

    gq_ref[...] = proj(OFF_GQ, KEY_WIDTH) * (HEAD_K ** -0.5)
    gla_decay(0)
    hgrn_decay(0)
    gk_ref[...] = proj(OFF_GK, KEY_WIDTH)
    gla_decay(1)
    hgrn_decay(1)
    gv_ref[...] = proj(OFF_GV, GLA_VAL_WIDTH).astype(BF16)
    gla_decay(2)
    hgrn_decay(2)
    hq_ref[...] = proj(OFF_HQ, KEY_WIDTH)
    gla_decay(3)
    hgrn_decay(3)
    hv_ref[...] = proj(OFF_HI, HGRN_VAL_WIDTH).astype(BF16)

    gb_ref[...] = _dot(tri_tile_ref[...], ghi_ref[...]) + _dot(tri_tile_ref[...], glo_ref[...])
    hb_ref[...] = _dot(tri_chunk_ref[...], hhi_ref[...]) + _dot(tri_chunk_ref[...], hlo_ref[...])
    gla_ok = _fast_path_ok(gb_ref, GLA_CHUNKS)
    hgrn_ok = _fast_path_ok(hb_ref, HGRN_CHUNKS)

    xoff = GLA_VAL_WIDTH + HGRN_VAL_WIDTH
    gate_block = 512

    def gate_piece(c0):
        def emit():
            g = proj(OFF_GATE + c0, gate_block)
            gs_ref[:, c0:c0 + gate_block] = g * jax.nn.sigmoid(g)
        return emit

    def xq_piece():
        xq_ref[...] = proj(OFF_XQ, XATTN_WIDTH).astype(BF16)

    def xattn_piece(hd):
        def emit():
            lanes = slice(hd * HEAD_K, (hd + 1) * HEAD_K)
            s = _dot_nt(xq_ref[:, lanes], mk_ref[:, lanes]) * (HEAD_K ** -0.5)
            p = jnp.exp(s - jnp.max(s, axis=-1, keepdims=True))
            o = _dot(p.astype(BF16), mv_ref[:, lanes]) / jnp.sum(p, axis=-1, keepdims=True)
            mixed_ref[:, xoff + hd * HEAD_K:xoff + (hd + 1) * HEAD_K] = _rms(
                o, xattn_norm_ref[...])
        return emit

    def out_piece(c0):
        def emit():
            cols = slice(c0, c0 + gate_block)
            gated = (mixed_ref[:, cols] * gs_ref[:, cols]).astype(BF16)
            part = _dot(gated, w_out_ref[cols, :])
            out_ref[0] = (x_ref[0] if c0 == 0 else out_ref[0]) + part
        return emit

    def epilogue():
        if final_norm:
            out_ref[0] = _rms(out_ref[0], final_norm_ref[...])

    gates = [gate_piece(c0) for c0 in range(0, D_MIX, gate_block)]
    heads = [xattn_piece(hd) for hd in range(N_HEADS)]
    outs = [out_piece(c0) for c0 in range(0, D_MIX, gate_block)]
    groups = iter(_piece_order(gates, heads, outs, xq_piece, epilogue))

    def fill():
        for piece in next(groups, []):
            piece()

    gla_prev_ref[...] = gla_st_ref[...]
    hgrn_prev_ref[...] = hgrn_st_ref[...]
    _mixer_fast(gq_ref, gk_ref, gv_ref, gb_ref, gla_st_ref, gla_norm_ref[...], mixed_ref,
                0, GLA_HEAD_V, GLA_CHUNKS, gla_operands, fill)
    _mixer_fast(hq_ref, hk_ref, hv_ref, hb_ref, hgrn_st_ref, hgrn_norm_ref[...], mixed_ref,
                GLA_VAL_WIDTH, HGRN_HEAD_V, HGRN_CHUNKS, hgrn_operands, fill)
    for group in groups:
        for piece in group:
            piece()

    @pl.when(jnp.logical_not(jnp.logical_and(gla_ok, hgrn_ok)))
    def _redo_with_pairwise_path():
        @pl.when(jnp.logical_not(gla_ok))
        def _gla_pairwise():
            gla_st_ref[...] = gla_prev_ref[...]
            _mixer_safe(gq_ref, gk_ref, gv_ref, gb_ref, gla_st_ref, gla_norm_ref[...],
                        mixed_ref, 0, GLA_HEAD_V, SEQ_TILE // GLA_CHUNKS)

        @pl.when(jnp.logical_not(hgrn_ok))
        def _hgrn_pairwise():
            hgrn_st_ref[...] = hgrn_prev_ref[...]
            _mixer_safe(hq_ref, hk_ref, hv_ref, hb_ref, hgrn_st_ref, hgrn_norm_ref[...],
                        mixed_ref, GLA_VAL_WIDTH, HGRN_HEAD_V, SEQ_TILE // HGRN_CHUNKS)

        for piece in outs:
            piece()
        epilogue()


def _layer_call(x, mem, params, layer, final_norm):
    batch, seq, _ = x.shape
    n_tiles = seq // SEQ_TILE

    def resident(arr, stacked):
        if stacked:
            shape = (None,) + arr.shape[1:]
            index = (layer,) + (0,) * (arr.ndim - 1)
        else:
            shape = arr.shape
            index = (0,) * arr.ndim
        return pl.BlockSpec(shape, lambda b, t: index, pipeline_mode=pl.Buffered(1))

    in_specs = [
        pl.BlockSpec((1, SEQ_TILE, D_MODEL), lambda b, t: (b, t, 0)),
        pl.BlockSpec((1, MEM_LEN, D_MODEL), lambda b, t: (b, 0, 0)),
    ] + [resident(arr, stacked) for arr, stacked in params]
    params = [arr for arr, _ in params]
    key_f32 = pltpu.VMEM((SEQ_TILE, KEY_WIDTH), F32)
    key_bf16 = pltpu.VMEM((SEQ_TILE, KEY_WIDTH), BF16)
    half_bf16 = pltpu.VMEM((SEQ_TILE // 2, KEY_WIDTH), BF16)
    gla_state = pltpu.VMEM((N_HEADS, GLA_HEAD_V, HEAD_K), F32)
    hgrn_state = pltpu.VMEM((N_HEADS, HGRN_HEAD_V, HEAD_K), F32)
    scratch = [
        pltpu.VMEM((SEQ_TILE, D_MODEL), BF16),
        key_f32, key_f32,
        key_f32, key_f32, key_f32,
        key_f32, key_f32, key_f32,
        pltpu.VMEM((SEQ_TILE, GLA_VAL_WIDTH), BF16),
        pltpu.VMEM((SEQ_TILE, HGRN_VAL_WIDTH), BF16),
        key_bf16, key_bf16, key_bf16, key_bf16,
        pltpu.VMEM((SEQ_TILE, D_MIX), F32),
        pltpu.VMEM((SEQ_TILE, D_MIX), F32),
        key_bf16,
        pltpu.VMEM((MEM_LEN, XATTN_WIDTH), BF16),
        pltpu.VMEM((MEM_LEN, XATTN_WIDTH), BF16),
        gla_state, hgrn_state,
        gla_state, hgrn_state,
        key_bf16, key_bf16, key_bf16, key_bf16,
        key_bf16, key_bf16, key_bf16, key_bf16, key_bf16, key_bf16,
        half_bf16, half_bf16,
        pltpu.VMEM((N_HEADS, SEQ_TILE, SEQ_TILE), BF16),
    ]
    return pl.pallas_call(
        functools.partial(_layer_kernel, layer=layer, final_norm=final_norm),
        out_shape=jax.ShapeDtypeStruct(x.shape, x.dtype),
        grid=(batch, n_tiles),
        in_specs=in_specs,
        out_specs=pl.BlockSpec((1, SEQ_TILE, D_MODEL), lambda b, t: (b, t, 0)),
        scratch_shapes=scratch,
        compiler_params=pltpu.CompilerParams(
            dimension_semantics=("arbitrary", "arbitrary"),
            vmem_limit_bytes=VMEM_LIMIT_BYTES),
        name=f"hybrid_layer_{layer}",
    )(x, mem, *params)


def _lower_tri_ones(segment):
    idx = jnp.arange(SEQ_TILE)
    keep = (idx[None, :] <= idx[:, None]) & ((idx[None, :] // segment) == (idx[:, None] // segment))
    return keep.astype(BF16)


def kernel(x, mem, norm_w, w_in, gla_w_gate_up, gla_b_gate, gla_norm_w, hgrn_lower_bounds,
           hgrn_norm_w, mem_norm_w, w_mem_kv, xattn_norm_w, w_out, final_norm_w):
    assert x.shape[1] % SEQ_TILE == 0
    rows = lambda v: (v.reshape(DEPTH, 1, -1).astype(F32), True)
    pad_lr = LOWRANK_PAD - GLA_LOWRANK
    w_main = jnp.concatenate([
        w_in[:, :, :SRC_GLR], w_in[:, :, SRC_HQ:SRC_HI],
        jnp.pad(w_in[:, :, SRC_GLR:SRC_HQ], ((0, 0), (0, 0), (0, pad_lr))),
        w_in[:, :, SRC_HI:]], axis=2).astype(BF16)
    w_up = jnp.pad(gla_w_gate_up, ((0, 0), (0, pad_lr), (0, 0))).astype(BF16)
    params = (
        (_lower_tri_ones(SEQ_TILE // GLA_CHUNKS), False),
        (_lower_tri_ones(SEQ_TILE // HGRN_CHUNKS), False),
        rows(norm_w), (w_main, True), (w_up, True), rows(gla_b_gate),
        rows(gla_norm_w), (hgrn_lower_bounds.astype(F32), False), rows(hgrn_norm_w),
        rows(mem_norm_w), (w_mem_kv.astype(BF16), True), rows(xattn_norm_w),
        (w_out.astype(BF16), True), (final_norm_w.reshape(1, -1).astype(F32), False),
    )
    for layer in range(DEPTH):
        x = _layer_call(x, mem, params, layer, final_norm=(layer == DEPTH - 1))
    return x
```

```python
import functools

import jax
import jax.numpy as jnp
from jax import lax
from jax.experimental import pallas as pl
from jax.experimental.pallas import tpu as pltpu

F32 = jnp.float32
BF16 = jnp.bfloat16

D_MODEL = 1024
DEPTH = 2
MEM_LEN = 256
N_HEADS = 4
HEAD_K = 128
GLA_HEAD_V = 256
HGRN_HEAD_V = 128
KEY_WIDTH = N_HEADS * HEAD_K
GLA_VAL_WIDTH = N_HEADS * GLA_HEAD_V
HGRN_VAL_WIDTH = N_HEADS * HGRN_HEAD_V
XATTN_WIDTH = N_HEADS * HEAD_K
D_MIX = 2048
GLA_LOWRANK = 16
GLA_GATE_TAU = 16.0
LB_FLOOR = 1e-30
NORM_EPS = 1e-6

LANE = 128
LOWRANK_LEAD = LANE - GLA_LOWRANK
WIN_A_WIDTH = 2 * KEY_WIDTH + GLA_VAL_WIDTH
WIN_B_START = WIN_A_WIDTH - LOWRANK_LEAD
OFF_GQ, OFF_GK, OFF_GV = 0, KEY_WIDTH, 2 * KEY_WIDTH
OFF_LR = 0
OFF_HQ = OFF_LR + LANE
OFF_HF = OFF_HQ + KEY_WIDTH
OFF_HI = OFF_HF + KEY_WIDTH
OFF_XQ = OFF_HI + HGRN_VAL_WIDTH
OFF_GATE = OFF_XQ + XATTN_WIDTH

SEQ_TILE = 128
GLA_CHUNKS = 1
HGRN_CHUNKS = 4
DECAY_ROWS = SEQ_TILE // 4
SAFE_BLOCK = 16
FAST_PATH_LIMIT = 60.0

VMEM_LIMIT_BYTES = 56 * 1024 * 1024


def _dot(a, b):
    return jnp.dot(a, b, preferred_element_type=F32)


def _dot_nt(a, b):
    return lax.dot_general(a, b, (((1,), (1,)), ((), ())), preferred_element_type=F32)


def _dot_tn(a, b):
    return lax.dot_general(a, b, (((0,), (0,)), ((), ())), preferred_element_type=F32)


def _rms(x, w):
    return x * lax.rsqrt(jnp.mean(x * x, axis=-1, keepdims=True) + NORM_EPS) * w


def _store_split(x, hi_ref, lo_ref, rows):
    hi = x.astype(BF16)
    hi_ref[rows, :] = hi
    lo_ref[rows, :] = (x - hi.astype(F32)).astype(BF16)


def _mixer_fast(q_ref, k_ref, vb_ref, b_ref, st_ref, norm_w, mixed_ref, out_off, dv, n_chunks,
                operand_refs, fill):
    assert n_chunks in (1, 4)
    qt_ref, kt_ref, qo_ref, ko_ref, qi_ref, ks_ref, ql_ref, kl_ref, p_ref = operand_refs
    chunk = SEQ_TILE // n_chunks
    half = SEQ_TILE // 2
    tot = [b_ref[(c + 1) * chunk - 1:(c + 1) * chunk, :] for c in range(n_chunks)]
    ref = [b_ref[c * chunk + chunk // 2 - 1:c * chunk + chunk // 2, :] for c in range(n_chunks)]
    before = [jnp.zeros_like(tot[0])]
    for c in range(n_chunks):
        before.append(before[-1] + tot[c])
    tile_total = before[n_chunks]
    if n_chunks == 4:
        e_in = [None] + [jnp.exp(before[c]) for c in range(1, 4)]
        e_out = [jnp.exp(tile_total - before[c + 1]) for c in range(3)] + [None]
        e_tot1 = jnp.exp(tot[1])
        e_tot2 = jnp.exp(tot[2])

    for c in range(n_chunks):
        rows = slice(c * chunk, (c + 1) * chunk)
        for h in range(N_HEADS):
            lanes = slice(h * HEAD_K, (h + 1) * HEAD_K)
            b = b_ref[rows, lanes]
            q = q_ref[rows, lanes]
            k = k_ref[rows, lanes]
            r = ref[c][:, lanes]
            qe = q * jnp.exp(b)
            ke = k * jnp.exp(tot[c][:, lanes] - b)
            qt_ref[rows, lanes] = (q * jnp.exp(b - r)).astype(BF16)
            kt_ref[rows, lanes] = (k * jnp.exp(r - b)).astype(BF16)
            if n_chunks == 1:
                qi_ref[rows, lanes] = qe.astype(BF16)
                ks_ref[rows, lanes] = ke.astype(BF16)
            else:
                qo_ref[rows, lanes] = qe.astype(BF16)
                ko_ref[rows, lanes] = ke.astype(BF16)
                qi = qe if c == 0 else qe * e_in[c][:, lanes]
                ks = ke if c == 3 else ke * e_out[c][:, lanes]
                qi_ref[rows, lanes] = qi.astype(BF16)
                ks_ref[rows, lanes] = ks.astype(BF16)
                if c == 0:
                    kl_ref[0:chunk, lanes] = (ke * e_tot1[:, lanes]).astype(BF16)
                elif c == 1:
                    kl_ref[chunk:half, lanes] = ke.astype(BF16)
                elif c == 2:
                    ql_ref[0:chunk, lanes] = qe.astype(BF16)
                else:
                    ql_ref[chunk:half, lanes] = (qe * e_tot2[:, lanes]).astype(BF16)
    fill()

    if n_chunks == 1:
        row = lax.broadcasted_iota(jnp.int32, (SEQ_TILE, SEQ_TILE), 0)
        col = lax.broadcasted_iota(jnp.int32, (SEQ_TILE, SEQ_TILE), 1)
        causal = row >= col
        for h in range(N_HEADS):
            lanes = slice(h * HEAD_K, (h + 1) * HEAD_K)
            vl = slice(h * dv, (h + 1) * dv)
            s = _dot_nt(qt_ref[:, lanes], kt_ref[:, lanes])
            p = jnp.where(causal, s, 0.0).astype(BF16)
            st = st_ref[h]
            o = _dot(p, vb_ref[:, vl]) + _dot_nt(qi_ref[:, lanes], st.astype(BF16))
            st_ref[h] = (st * jnp.exp(tile_total[:, lanes])
                         + _dot_tn(vb_ref[:, vl], ks_ref[:, lanes]))
            mixed_ref[:, out_off + h * dv:out_off + (h + 1) * dv] = _rms(o, norm_w)
            if h % 2 == 1:
                fill()
        return

    row = lax.broadcasted_iota(jnp.int32, (half, half), 0)
    col = lax.broadcasted_iota(jnp.int32, (half, half), 1)
    same_chunk = ((row >= chunk) == (col >= chunk)) & (row >= col)
    later_chunk = (row >= chunk) & (col < chunk)
    for h in range(N_HEADS):
        lanes = slice(h * HEAD_K, (h + 1) * HEAD_K)
        for hf in range(2):
            rows = slice(hf * half, (hf + 1) * half)
            d = _dot_nt(qt_ref[rows, lanes], kt_ref[rows, lanes])
            od = _dot_nt(qo_ref[rows, lanes], ko_ref[rows, lanes])
            p_ref[h, rows, rows] = jnp.where(
                same_chunk, d, jnp.where(later_chunk, od, 0.0)).astype(BF16)
        p_ref[h, half:, 0:half] = _dot_nt(ql_ref[:, lanes], kl_ref[:, lanes]).astype(BF16)
    fill()
    for h in range(N_HEADS):
        lanes = slice(h * HEAD_K, (h + 1) * HEAD_K)
        vl = slice(h * dv, (h + 1) * dv)
        ol = slice(out_off + h * dv, out_off + (h + 1) * dv)
        st = st_ref[h]
        st_b = st.astype(BF16)
        mixed_ref[0:half, ol] = (_dot(p_ref[h, 0:half, 0:half], vb_ref[0:half, vl])
                                 + _dot_nt(qi_ref[0:half, lanes], st_b))
        mixed_ref[half:, ol] = (_dot(p_ref[h, half:, :], vb_ref[:, vl])
                                + _dot_nt(qi_ref[half:, lanes], st_b))
        st_ref[h] = st * jnp.exp(tile_total[:, lanes]) + _dot_tn(vb_ref[:, vl], ks_ref[:, lanes])
    fill()
    for h in range(N_HEADS):
        ol = slice(out_off + h * dv, out_off + (h + 1) * dv)
        mixed_ref[:, ol] = _rms(mixed_ref[:, ol], norm_w)


def _mixer_safe(q_ref, k_ref, vb_ref, b_ref, st_ref, norm_w, mixed_ref, out_off, dv, segment):
    row16 = lax.broadcasted_iota(jnp.int32, (SAFE_BLOCK, 1), 0)

    def block_body(sb, carry):
        s0 = pl.multiple_of(sb * SAFE_BLOCK, SAFE_BLOCK)
        rows_b = pl.ds(s0, SAFE_BLOCK)
        before = pl.multiple_of(jnp.maximum(s0 - 8, 0), 8)
        prev = b_ref[pl.ds(before, 8), :][7:8, :]
        prev = jnp.where(s0 % segment == 0, 0.0, prev)
        for h in range(N_HEADS):
            lanes = slice(h * HEAD_K, (h + 1) * HEAD_K)
            vl = slice(h * dv, (h + 1) * dv)
            bl = b_ref[rows_b, lanes] - prev[:, lanes]
            q = q_ref[rows_b, lanes]
            k = k_ref[rows_b, lanes]
            vb = vb_ref[rows_b, vl]
            v = vb.astype(F32)
            o = jnp.zeros((SAFE_BLOCK, dv), F32)
            for j in range(SAFE_BLOCK):
                w = jnp.exp(jnp.minimum(bl - bl[j:j + 1, :], 0.0))
                sc = jnp.sum(q * k[j:j + 1, :] * w, axis=-1, keepdims=True)
                o = o + jnp.where(row16 >= j, sc, 0.0) * v[j:j + 1, :]
            st = st_ref[h]
            o = o + _dot_nt((q * jnp.exp(bl)).astype(BF16), st.astype(BF16))
            b_end = bl[SAFE_BLOCK - 1:SAFE_BLOCK, :]
            kl = (k * jnp.exp(b_end - bl)).astype(BF16)
            st_ref[h] = st * jnp.exp(b_end) + _dot_tn(vb, kl)
            mixed_ref[rows_b, out_off + h * dv:out_off + (h + 1) * dv] = _rms(o, norm_w)
        return carry

    lax.fori_loop(0, SEQ_TILE // SAFE_BLOCK, block_body, 0)


def _fast_path_ok(b_ref, n_chunks):
    chunk = SEQ_TILE // n_chunks
    worst = jnp.zeros((1, KEY_WIDTH), F32)
    for c in range(n_chunks):
        r = b_ref[c * chunk + chunk // 2 - 1:c * chunk + chunk // 2, :]
        total = b_ref[(c + 1) * chunk - 1:(c + 1) * chunk, :]
        worst = jnp.maximum(worst, jnp.maximum(-r, r - total))
    return jnp.max(worst) < FAST_PATH_LIMIT


def _piece_order(gates, heads, outs, xq, epilogue):
    return [[gates[0]], [gates[1]], [gates[2]], [outs[0]], [gates[3]], [outs[1]],
            [xq] + heads + [outs[2], outs[3], epilogue]]


def _layer_kernel(x_ref, mem_ref, tri_tile_ref, tri_chunk_ref,
                  norm_w_ref, w_a_ref, w_b_ref, w_up_ref, b_gate_ref,
                  gla_norm_ref, lb_ref, hgrn_norm_ref, mem_norm_ref, w_mem_ref, xattn_norm_ref,
                  w_out_ref, final_norm_ref,
                  out_ref,
                  h_ref, zg_ref, zf_ref, gq_ref, gk_ref, gb_ref, hq_ref, hk_ref, hb_ref,
                  gv_ref, hv_ref, ghi_ref, glo_ref, hhi_ref, hlo_ref,
                  mixed_ref, gs_ref, xq_ref, mk_ref, mv_ref,
                  gla_st_ref, hgrn_st_ref, gla_prev_ref, hgrn_prev_ref,
                  g_qt_ref, g_kt_ref, g_qi_ref, g_ks_ref,
                  h_qt_ref, h_kt_ref, h_qo_ref, h_ko_ref, h_qi_ref, h_ks_ref, h_ql_ref, h_kl_ref,
                  p_ref,
                  *, layer, final_norm):
    t = pl.program_id(1)
    gla_operands = (g_qt_ref, g_kt_ref, None, None, g_qi_ref, g_ks_ref, None, None, None)
    hgrn_operands = (h_qt_ref, h_kt_ref, h_qo_ref, h_ko_ref, h_qi_ref, h_ks_ref, h_ql_ref,
                     h_kl_ref, p_ref)

    @pl.when(t == 0)
    def _start_of_sequence():
        gla_st_ref[...] = jnp.zeros_like(gla_st_ref)
        hgrn_st_ref[...] = jnp.zeros_like(hgrn_st_ref)
        m = _rms(mem_ref[0], mem_norm_ref[...]).astype(BF16)
        mk_ref[...] = _dot(m, w_mem_ref[:, :XATTN_WIDTH]).astype(BF16)
        mv_ref[...] = _dot(m, w_mem_ref[:, XATTN_WIDTH:]).astype(BF16)

    x = x_ref[0]
    h_ref[...] = _rms(x, norm_w_ref[...]).astype(BF16)

    def proj_a(off, width):
        return _dot(h_ref[...], w_a_ref[:, off:off + width])

    def proj(off, width):
        return _dot(h_ref[...], w_b_ref[:, off:off + width])

    lr_hq = proj(OFF_LR, LANE + KEY_WIDTH)
    low_rank = lr_hq[:, :LANE].astype(BF16)
    hq_ref[...] = lr_hq[:, LANE:]
    zg_ref[...] = _dot(low_rank, w_up_ref[...]) + b_gate_ref[...]
    zf_ref[...] = proj(OFF_HF, KEY_WIDTH)

    lbw = lb_ref[...]
    lb_e = jnp.exp(lbw - jnp.max(lbw, axis=0, keepdims=True))
    lb_p = lb_e / jnp.sum(lb_e, axis=0, keepdims=True)
    lb = jnp.sum(lb_p[:layer + 1], axis=0, keepdims=True) - lb_p[0:1]
    log_lb = jnp.log(jnp.maximum(lb, LB_FLOOR))
    log_1m_lb = jnp.log1p(-lb)
    one_m_lb = 1.0 - lb

    def gla_decay(j):
        rows = slice(j * DECAY_ROWS, (j + 1) * DECAY_ROWS)
        z = zg_ref[rows, :]
        ls = jnp.minimum(z, 0.0) - jnp.log(1.0 + jnp.exp(-jnp.abs(z)))
        _store_split(ls * (1.0 / GLA_GATE_TAU), ghi_ref, glo_ref, rows)

    def hgrn_decay(j):
        rows = slice(j * DECAY_ROWS, (j + 1) * DECAY_ROWS)
        z = zf_ref[rows, :]
        e = jnp.exp(-jnp.abs(z))
        u = 1.0 + e
        c = log_1m_lb + (jnp.minimum(z, 0.0) - jnp.log(u))
        log_f = jnp.maximum(log_lb, c) + jnp.log(1.0 + jnp.exp(-jnp.abs(log_lb - c)))
        _store_split(log_f, hhi_ref, hlo_ref, rows)
        hk_ref[rows, :] = one_m_lb * (jnp.where(z >= 0.0, e, 1.0) / u)

    gq_ref[...] = proj_a(OFF_GQ, KEY_WIDTH) * (HEAD_K ** -0.5)
    gla_decay(0)
    hgrn_decay(0)
    gk_ref[...] = proj_a(OFF_GK, KEY_WIDTH)
    gla_decay(1)
    hgrn_decay(1)
    gv_ref[:, :KEY_WIDTH] = proj_a(OFF_GV, KEY_WIDTH).astype(BF16)
    gla_decay(2)
    hgrn_decay(2)
    gv_ref[:, KEY_WIDTH:] = proj_a(OFF_GV + KEY_WIDTH, KEY_WIDTH).astype(BF16)
    gla_decay(3)
    hgrn_decay(3)
    hv_ref[...] = proj(OFF_HI, HGRN_VAL_WIDTH).astype(BF16)

    gb_ref[...] = _dot(tri_tile_ref[...], ghi_ref[...]) + _dot(tri_tile_ref[...], glo_ref[...])
    hb_ref[...] = _dot(tri_chunk_ref[...], hhi_ref[...]) + _dot(tri_chunk_ref[...], hlo_ref[...])
    gla_ok = _fast_path_ok(gb_ref, GLA_CHUNKS)
    hgrn_ok = _fast_path_ok(hb_ref, HGRN_CHUNKS)

    xoff = GLA_VAL_WIDTH + HGRN_VAL_WIDTH
    gate_block = 512

    def gate_piece(c0):
        def emit():
            g = proj(OFF_GATE + c0, gate_block)
            gs_ref[:, c0:c0 + gate_block] = g * jax.nn.sigmoid(g)
        return emit

    def xq_piece():
        xq_ref[...] = proj(OFF_XQ, XATTN_WIDTH).astype(BF16)

    def xattn_piece(hd):
        def emit():
            lanes = slice(hd * HEAD_K, (hd + 1) * HEAD_K)
            s = _dot_nt(xq_ref[:, lanes], mk_ref[:, lanes]) * (HEAD_K ** -0.5)
            p = jnp.exp(s - jnp.max(s, axis=-1, keepdims=True))
            o = _dot(p.astype(BF16), mv_ref[:, lanes]) / jnp.sum(p, axis=-1, keepdims=True)
            mixed_ref[:, xoff + hd * HEAD_K:xoff + (hd + 1) * HEAD_K] = _rms(
                o, xattn_norm_ref[...])
        return emit

    def out_piece(c0):
        def emit():
            cols = slice(c0, c0 + gate_block)
            gated = (mixed_ref[:, cols] * gs_ref[:, cols]).astype(BF16)
            part = _dot(gated, w_out_ref[cols, :])
            out_ref[0] = (x_ref[0] if c0 == 0 else out_ref[0]) + part
        return emit

    def epilogue():
        if final_norm:
            out_ref[0] = _rms(out_ref[0], final_norm_ref[...])

    gates = [gate_piece(c0) for c0 in range(0, D_MIX, gate_block)]
    heads = [xattn_piece(hd) for hd in range(N_HEADS)]
    outs = [out_piece(c0) for c0 in range(0, D_MIX, gate_block)]
    groups = iter(_piece_order(gates, heads, outs, xq_piece, epilogue))

    def fill():
        for piece in next(groups, []):
            piece()

    gla_prev_ref[...] = gla_st_ref[...]
    hgrn_prev_ref[...] = hgrn_st_ref[...]
    _mixer_fast(gq_ref, gk_ref, gv_ref, gb_ref, gla_st_ref, gla_norm_ref[...], mixed_ref,
                0, GLA_HEAD_V, GLA_CHUNKS, gla_operands, fill)
    _mixer_fast(hq_ref, hk_ref, hv_ref, hb_ref, hgrn_st_ref, hgrn_norm_ref[...], mixed_ref,
                GLA_VAL_WIDTH, HGRN_HEAD_V, HGRN_CHUNKS, hgrn_operands, fill)
    for group in groups:
        for piece in group:
            piece()

    @pl.when(jnp.logical_not(jnp.logical_and(gla_ok, hgrn_ok)))
    def _redo_with_pairwise_path():
        @pl.when(jnp.logical_not(gla_ok))
        def _gla_pairwise():
            gla_st_ref[...] = gla_prev_ref[...]
            _mixer_safe(gq_ref, gk_ref, gv_ref, gb_ref, gla_st_ref, gla_norm_ref[...],
                        mixed_ref, 0, GLA_HEAD_V, SEQ_TILE // GLA_CHUNKS)

        @pl.when(jnp.logical_not(hgrn_ok))
        def _hgrn_pairwise():
            hgrn_st_ref[...] = hgrn_prev_ref[...]
            _mixer_safe(hq_ref, hk_ref, hv_ref, hb_ref, hgrn_st_ref, hgrn_norm_ref[...],
                        mixed_ref, GLA_VAL_WIDTH, HGRN_HEAD_V, SEQ_TILE // HGRN_CHUNKS)

        for piece in outs:
            piece()
        epilogue()


def _layer_call(x, mem, params, layer, final_norm):
    batch, seq, _ = x.shape
    n_tiles = seq // SEQ_TILE

    def resident(arr, stacked):
        if stacked:
            shape = (None,) + arr.shape[1:]
            index = (layer,) + (0,) * (arr.ndim - 1)
        else:
            shape = arr.shape
            index = (0,) * arr.ndim
        return pl.BlockSpec(shape, lambda b, t: index, pipeline_mode=pl.Buffered(1))

    in_specs = [
        pl.BlockSpec((1, SEQ_TILE, D_MODEL), lambda b, t: (b, t, 0)),
        pl.BlockSpec((1, MEM_LEN, D_MODEL), lambda b, t: (b, 0, 0)),
    ] + [resident(arr, stacked) for arr, stacked in params]
    params = [arr for arr, _ in params]
    key_f32 = pltpu.VMEM((SEQ_TILE, KEY_WIDTH), F32)
    key_bf16 = pltpu.VMEM((SEQ_TILE, KEY_WIDTH), BF16)
    half_bf16 = pltpu.VMEM((SEQ_TILE // 2, KEY_WIDTH), BF16)
    gla_state = pltpu.VMEM((N_HEADS, GLA_HEAD_V, HEAD_K), F32)
    hgrn_state = pltpu.VMEM((N_HEADS, HGRN_HEAD_V, HEAD_K), F32)
    scratch = [
        pltpu.VMEM((SEQ_TILE, D_MODEL), BF16),
        key_f32, key_f32,
        key_f32, key_f32, key_f32,
        key_f32, key_f32, key_f32,
        pltpu.VMEM((SEQ_TILE, GLA_VAL_WIDTH), BF16),
        pltpu.VMEM((SEQ_TILE, HGRN_VAL_WIDTH), BF16),
        key_bf16, key_bf16, key_bf16, key_bf16,
        pltpu.VMEM((SEQ_TILE, D_MIX), F32),
        pltpu.VMEM((SEQ_TILE, D_MIX), F32),
        key_bf16,
        pltpu.VMEM((MEM_LEN, XATTN_WIDTH), BF16),
        pltpu.VMEM((MEM_LEN, XATTN_WIDTH), BF16),
        gla_state, hgrn_state,
        gla_state, hgrn_state,
        key_bf16, key_bf16, key_bf16, key_bf16,
        key_bf16, key_bf16, key_bf16, key_bf16, key_bf16, key_bf16,
        half_bf16, half_bf16,
        pltpu.VMEM((N_HEADS, SEQ_TILE, SEQ_TILE), BF16),
    ]
    return pl.pallas_call(
        functools.partial(_layer_kernel, layer=layer, final_norm=final_norm),
        out_shape=jax.ShapeDtypeStruct(x.shape, x.dtype),
        grid=(batch, n_tiles),
        in_specs=in_specs,
        out_specs=pl.BlockSpec((1, SEQ_TILE, D_MODEL), lambda b, t: (b, t, 0)),
        scratch_shapes=scratch,
        compiler_params=pltpu.CompilerParams(
            dimension_semantics=("arbitrary", "arbitrary"),
            vmem_limit_bytes=VMEM_LIMIT_BYTES),
        name=f"hybrid_layer_{layer}",
    )(x, mem, *params)


def _lower_tri_ones(segment):
    idx = jnp.arange(SEQ_TILE)
    keep = (idx[None, :] <= idx[:, None]) & ((idx[None, :] // segment) == (idx[:, None] // segment))
    return keep.astype(BF16)


def kernel(x, mem, norm_w, w_in, gla_w_gate_up, gla_b_gate, gla_norm_w, hgrn_lower_bounds,
           hgrn_norm_w, mem_norm_w, w_mem_kv, xattn_norm_w, w_out, final_norm_w):
    assert x.shape[1] % SEQ_TILE == 0
    rows = lambda v: (v.reshape(DEPTH, 1, -1).astype(F32), True)
    w_a = w_in[:, :, :WIN_A_WIDTH].astype(BF16)
    w_b = w_in[:, :, WIN_B_START:].astype(BF16)
    w_up = jnp.pad(gla_w_gate_up, ((0, 0), (LOWRANK_LEAD, 0), (0, 0))).astype(BF16)
    params = (
        (_lower_tri_ones(SEQ_TILE // GLA_CHUNKS), False),
        (_lower_tri_ones(SEQ_TILE // HGRN_CHUNKS), False),
        rows(norm_w), (w_a, True), (w_b, True), (w_up, True), rows(gla_b_gate),
        rows(gla_norm_w), (hgrn_lower_bounds.astype(F32), False), rows(hgrn_norm_w),
        rows(mem_norm_w), (w_mem_kv.astype(BF16), True), rows(xattn_norm_w),
        (w_out.astype(BF16), True), (final_norm_w.reshape(1, -1).astype(F32), False),
    )
    for layer in range(DEPTH):
        x = _layer_call(x, mem, params, layer, final_norm=(layer == DEPTH - 1))
    return x
```

```python
import functools

import jax
import jax.numpy as jnp
from jax import lax
from jax.experimental import pallas as pl
from jax.experimental.pallas import tpu as pltpu

F32 = jnp.float32
BF16 = jnp.bfloat16

D_MODEL = 1024
DEPTH = 2
MEM_LEN = 256
N_HEADS = 4
HEAD_K = 128
GLA_HEAD_V = 256
HGRN_HEAD_V = 128
KEY_WIDTH = N_HEADS * HEAD_K
GLA_VAL_WIDTH = N_HEADS * GLA_HEAD_V
HGRN_VAL_WIDTH = N_HEADS * HGRN_HEAD_V
XATTN_WIDTH = N_HEADS * HEAD_K
D_MIX = 2048
GLA_LOWRANK = 16
GLA_GATE_TAU = 16.0
LB_FLOOR = 1e-30
NORM_EPS = 1e-6

LANE = 128
LOWRANK_LEAD = LANE - GLA_LOWRANK
WIN_A_WIDTH = 2 * KEY_WIDTH + GLA_VAL_WIDTH
WIN_B_START = WIN_A_WIDTH - LOWRANK_LEAD
OFF_GQ, OFF_GK, OFF_GV = 0, KEY_WIDTH, 2 * KEY_WIDTH
OFF_LR = 0
OFF_HQ = OFF_LR + LANE
OFF_HF = OFF_HQ + KEY_WIDTH
OFF_HI = OFF_HF + KEY_WIDTH
OFF_XQ = OFF_HI + HGRN_VAL_WIDTH
OFF_GATE = OFF_XQ + XATTN_WIDTH

SEQ_TILE = 512
MIX_TILE = 256
GLA_CHUNKS = 1
HGRN_CHUNKS = 4
DECAY_ROWS = 64
SAFE_BLOCK = 16
FAST_PATH_LIMIT = 60.0

VMEM_LIMIT_BYTES = 58 * 1024 * 1024


def _dot(a, b):
    return jnp.dot(a, b, preferred_element_type=F32)


def _dot_nt(a, b):
    return lax.dot_general(a, b, (((1,), (1,)), ((), ())), preferred_element_type=F32)


def _dot_tn(a, b):
    return lax.dot_general(a, b, (((0,), (0,)), ((), ())), preferred_element_type=F32)


def _rms(x, w):
    return x * lax.rsqrt(jnp.mean(x * x, axis=-1, keepdims=True) + NORM_EPS) * w


def _store_split(x, hi_ref, lo_ref, rows):
    hi = x.astype(BF16)
    hi_ref[rows, :] = hi
    lo_ref[rows, :] = (x - hi.astype(F32)).astype(BF16)


def _mixer_fast(q_ref, k_ref, vb_ref, b_ref, st_ref, norm_w, mixed_ref, out_off, dv, n_chunks,
                operand_refs, fill):
    assert n_chunks in (1, 4)
    qt_ref, kt_ref, qo_ref, ko_ref, qi_ref, ks_ref, ql_ref, kl_ref, p_ref = operand_refs
    chunk = MIX_TILE // n_chunks
    half = MIX_TILE // 2
    tot = [b_ref[(c + 1) * chunk - 1:(c + 1) * chunk, :] for c in range(n_chunks)]
    ref = [b_ref[c * chunk + chunk // 2 - 1:c * chunk + chunk // 2, :] for c in range(n_chunks)]
    before = [jnp.zeros_like(tot[0])]
    for c in range(n_chunks):
        before.append(before[-1] + tot[c])
    tile_total = before[n_chunks]
    if n_chunks == 4:
        e_in = [None] + [jnp.exp(before[c]) for c in range(1, 4)]
        e_out = [jnp.exp(tile_total - before[c + 1]) for c in range(3)] + [None]
        e_tot1 = jnp.exp(tot[1])
        e_tot2 = jnp.exp(tot[2])

    for c in range(n_chunks):
        rows = slice(c * chunk, (c + 1) * chunk)
        for h in range(N_HEADS):
            lanes = slice(h * HEAD_K, (h + 1) * HEAD_K)
            b = b_ref[rows, lanes]
            q = q_ref[rows, lanes]
            k = k_ref[rows, lanes]
            r = ref[c][:, lanes]
            qe = q * jnp.exp(b)
            ke = k * jnp.exp(tot[c][:, lanes] - b)
            qt_ref[rows, lanes] = (q * jnp.exp(b - r)).astype(BF16)
            kt_ref[rows, lanes] = (k * jnp.exp(r - b)).astype(BF16)
            if n_chunks == 1:
                qi_ref[rows, lanes] = qe.astype(BF16)
                ks_ref[rows, lanes] = ke.astype(BF16)
            else:
                qo_ref[rows, lanes] = qe.astype(BF16)
                ko_ref[rows, lanes] = ke.astype(BF16)
                qi = qe if c == 0 else qe * e_in[c][:, lanes]
                ks = ke if c == 3 else ke * e_out[c][:, lanes]
                qi_ref[rows, lanes] = qi.astype(BF16)
                ks_ref[rows, lanes] = ks.astype(BF16)
                if c == 0:
                    kl_ref[0:chunk, lanes] = (ke * e_tot1[:, lanes]).astype(BF16)
                elif c == 1:
                    kl_ref[chunk:half, lanes] = ke.astype(BF16)
                elif c == 2:
                    ql_ref[0:chunk, lanes] = qe.astype(BF16)
                else:
                    ql_ref[chunk:half, lanes] = (qe * e_tot2[:, lanes]).astype(BF16)
    fill()

    if n_chunks == 1:
        row = lax.broadcasted_iota(jnp.int32, (MIX_TILE, MIX_TILE), 0)
        col = lax.broadcasted_iota(jnp.int32, (MIX_TILE, MIX_TILE), 1)
        causal = row >= col
        for h in range(N_HEADS):
            lanes = slice(h * HEAD_K, (h + 1) * HEAD_K)
            vl = slice(h * dv, (h + 1) * dv)
            s = _dot_nt(qt_ref[:, lanes], kt_ref[:, lanes])
            p = jnp.where(causal, s, 0.0).astype(BF16)
            st = st_ref[h]
            o = _dot(p, vb_ref[:, vl]) + _dot_nt(qi_ref[:, lanes], st.astype(BF16))
            st_ref[h] = (st * jnp.exp(tile_total[:, lanes])
                         + _dot_tn(vb_ref[:, vl], ks_ref[:, lanes]))
            mixed_ref[:, out_off + h * dv:out_off + (h + 1) * dv] = _rms(o, norm_w)
            if h % 2 == 1:
                fill()
        return

    row = lax.broadcasted_iota(jnp.int32, (half, half), 0)
    col = lax.broadcasted_iota(jnp.int32, (half, half), 1)
    same_chunk = ((row >= chunk) == (col >= chunk)) & (row >= col)
    later_chunk = (row >= chunk) & (col < chunk)
    for h in range(N_HEADS):
        lanes = slice(h * HEAD_K, (h + 1) * HEAD_K)
        for hf in range(2):
            rows = slice(hf * half, (hf + 1) * half)
            d = _dot_nt(qt_ref[rows, lanes], kt_ref[rows, lanes])
            od = _dot_nt(qo_ref[rows, lanes], ko_ref[rows, lanes])
            p_ref[h, rows, rows] = jnp.where(
                same_chunk, d, jnp.where(later_chunk, od, 0.0)).astype(BF16)
        p_ref[h, half:, 0:half] = _dot_nt(ql_ref[:, lanes], kl_ref[:, lanes]).astype(BF16)
    fill()
    for h in range(N_HEADS):
        lanes = slice(h * HEAD_K, (h + 1) * HEAD_K)
        vl = slice(h * dv, (h + 1) * dv)
        ol = slice(out_off + h * dv, out_off + (h + 1) * dv)
        st = st_ref[h]
        st_b = st.astype(BF16)
        mixed_ref[0:half, ol] = (_dot(p_ref[h, 0:half, 0:half], vb_ref[0:half, vl])
                                 + _dot_nt(qi_ref[0:half, lanes], st_b))
        mixed_ref[half:, ol] = (_dot(p_ref[h, half:, :], vb_ref[:, vl])
                                + _dot_nt(qi_ref[half:, lanes], st_b))
        st_ref[h] = st * jnp.exp(tile_total[:, lanes]) + _dot_tn(vb_ref[:, vl], ks_ref[:, lanes])
    fill()
    for h in range(N_HEADS):
        ol = slice(out_off + h * dv, out_off + (h + 1) * dv)
        mixed_ref[:, ol] = _rms(mixed_ref[:, ol], norm_w)


def _mixer_safe(q_ref, k_ref, vb_ref, b_ref, st_ref, norm_w, mixed_ref, out_off, dv, segment):
    row16 = lax.broadcasted_iota(jnp.int32, (SAFE_BLOCK, 1), 0)

    def block_body(sb, carry):
        s0 = pl.multiple_of(sb * SAFE_BLOCK, SAFE_BLOCK)
        rows_b = pl.ds(s0, SAFE_BLOCK)
        before = pl.multiple_of(jnp.maximum(s0 - 8, 0), 8)
        prev = b_ref[pl.ds(before, 8), :][7:8, :]
        prev = jnp.where(s0 % segment == 0, 0.0, prev)
        for h in range(N_HEADS):
            lanes = slice(h * HEAD_K, (h + 1) * HEAD_K)
            vl = slice(h * dv, (h + 1) * dv)
            bl = b_ref[rows_b, lanes] - prev[:, lanes]
            q = q_ref[rows_b, lanes]
            k = k_ref[rows_b, lanes]
            vb = vb_ref[rows_b, vl]
            v = vb.astype(F32)
            o = jnp.zeros((SAFE_BLOCK, dv), F32)
            for j in range(SAFE_BLOCK):
                w = jnp.exp(jnp.minimum(bl - bl[j:j + 1, :], 0.0))
                sc = jnp.sum(q * k[j:j + 1, :] * w, axis=-1, keepdims=True)
                o = o + jnp.where(row16 >= j, sc, 0.0) * v[j:j + 1, :]
            st = st_ref[h]
            o = o + _dot_nt((q * jnp.exp(bl)).astype(BF16), st.astype(BF16))
            b_end = bl[SAFE_BLOCK - 1:SAFE_BLOCK, :]
            kl = (k * jnp.exp(b_end - bl)).astype(BF16)
            st_ref[h] = st * jnp.exp(b_end) + _dot_tn(vb, kl)
            mixed_ref[rows_b, out_off + h * dv:out_off + (h + 1) * dv] = _rms(o, norm_w)
        return carry

    lax.fori_loop(0, SEQ_TILE // SAFE_BLOCK, block_body, 0)


def _fast_path_ok(b_ref, n_chunks):
    chunk = MIX_TILE // n_chunks
    worst = jnp.zeros((1, KEY_WIDTH), F32)
    for c in range(n_chunks):
        r = b_ref[c * chunk + chunk // 2 - 1:c * chunk + chunk // 2, :]
        total = b_ref[(c + 1) * chunk - 1:(c + 1) * chunk, :]
        worst = jnp.maximum(worst, jnp.maximum(-r, r - total))
    return jnp.max(worst) < FAST_PATH_LIMIT


def _piece_order(gates, heads, outs, xq, epilogue):
    h0, h1 = heads
    return [[gates[0]], [gates[1]], [xq],
            [gates[2]], [gates[3]], [h0[0], h0[1]],
            [h0[2]], [h0[3]], [h1[0]],
            [outs[0]], [outs[1]], [h1[1]],
            [h1[2], h1[3], outs[2], outs[3], epilogue]]


def _layer_kernel(x_ref, mem_ref, tri_tile_ref, tri_chunk_ref,
                  norm_w_ref, w_a_ref, w_b_ref, w_up_ref, b_gate_ref,
                  gla_norm_ref, lb_ref, hgrn_norm_ref, mem_norm_ref, w_mem_ref, xattn_norm_ref,
                  w_out_ref, final_norm_ref,
                  out_ref,
                  h_ref, gq_ref, gk_ref, gb_ref, hq_ref, hk_ref, hb_ref,
                  gv_ref, hv_ref, ghi_ref, glo_ref, hhi_ref, hlo_ref,
                  mixed_ref, gs_ref, xq_ref, mk_ref, mv_ref,
                  gla_st_ref, hgrn_st_ref, gla_prev_ref, hgrn_prev_ref,
                  g_qt_ref, g_kt_ref, g_qi_ref, g_ks_ref,
                  h_qt_ref, h_kt_ref, h_qo_ref, h_ko_ref, h_qi_ref, h_ks_ref, h_ql_ref, h_kl_ref,
                  p_ref,
                  *, layer, final_norm):
    t = pl.program_id(1)
    n_pass = SEQ_TILE // MIX_TILE
    gla_operands = (g_qt_ref, g_kt_ref, None, None, g_qi_ref, g_ks_ref, None, None, None)
    hgrn_operands = (h_qt_ref, h_kt_ref, h_qo_ref, h_ko_ref, h_qi_ref, h_ks_ref, h_ql_ref,
                     h_kl_ref, p_ref)
    zg_ref, zf_ref = gb_ref, hk_ref

    @pl.when(t == 0)
    def _start_of_sequence():
        gla_st_ref[...] = jnp.zeros_like(gla_st_ref)
        hgrn_st_ref[...] = jnp.zeros_like(hgrn_st_ref)
        m = _rms(mem_ref[0], mem_norm_ref[...]).astype(BF16)
        mk_ref[...] = _dot(m, w_mem_ref[:, :XATTN_WIDTH]).astype(BF16)
        mv_ref[...] = _dot(m, w_mem_ref[:, XATTN_WIDTH:]).astype(BF16)

    h_ref[...] = _rms(x_ref[0], norm_w_ref[...]).astype(BF16)

    def proj_a(off, width):
        return _dot(h_ref[...], w_a_ref[:, off:off + width])

    def proj(off, width):
        return _dot(h_ref[...], w_b_ref[:, off:off + width])

    lr_hq = proj(OFF_LR, LANE + KEY_WIDTH)
    low_rank = lr_hq[:, :LANE].astype(BF16)
    hq_ref[...] = lr_hq[:, LANE:]
    zg_ref[...] = _dot(low_rank, w_up_ref[...]) + b_gate_ref[...]
    zf_ref[...] = proj(OFF_HF, KEY_WIDTH)

    lbw = lb_ref[...]
    lb_e = jnp.exp(lbw - jnp.max(lbw, axis=0, keepdims=True))
    lb_p = lb_e / jnp.sum(lb_e, axis=0, keepdims=True)
    lb = jnp.sum(lb_p[:layer + 1], axis=0, keepdims=True) - lb_p[0:1]
    log_lb = jnp.log(jnp.maximum(lb, LB_FLOOR))
    log_1m_lb = jnp.log1p(-lb)
    one_m_lb = 1.0 - lb

    def decay(j):
        rows = slice(j * DECAY_ROWS, (j + 1) * DECAY_ROWS)
        z = zg_ref[rows, :]
        ls = jnp.minimum(z, 0.0) - jnp.log(1.0 + jnp.exp(-jnp.abs(z)))
        _store_split(ls * (1.0 / GLA_GATE_TAU), ghi_ref, glo_ref, rows)
        z = zf_ref[rows, :]
        e = jnp.exp(-jnp.abs(z))
        u = 1.0 + e
        c = log_1m_lb + (jnp.minimum(z, 0.0) - jnp.log(u))
        log_f = jnp.maximum(log_lb, c) + jnp.log(1.0 + jnp.exp(-jnp.abs(log_lb - c)))
        _store_split(log_f, hhi_ref, hlo_ref, rows)
        hk_ref[rows, :] = one_m_lb * (jnp.where(z >= 0.0, e, 1.0) / u)

    def gq_piece():
        gq_ref[...] = proj_a(OFF_GQ, KEY_WIDTH) * (HEAD_K ** -0.5)

    def gk_piece():
        gk_ref[...] = proj_a(OFF_GK, KEY_WIDTH)

    def gv_piece(c0):
        def emit():
            gv_ref[:, c0:c0 + KEY_WIDTH] = proj_a(OFF_GV + c0, KEY_WIDTH).astype(BF16)
        return emit

    def hv_piece():
        hv_ref[...] = proj(OFF_HI, HGRN_VAL_WIDTH).astype(BF16)

    matmul_pieces = [gq_piece, gk_piece, gv_piece(0), gv_piece(KEY_WIDTH), hv_piece]
    n_decay = SEQ_TILE // DECAY_ROWS
    for i, piece in enumerate(matmul_pieces):
        piece()
        for j in range(i * n_decay // len(matmul_pieces), (i + 1) * n_decay // len(matmul_pieces)):
            decay(j)

    gla_ok = hgrn_ok = None
    for s in range(n_pass):
        rows = slice(s * MIX_TILE, (s + 1) * MIX_TILE)
        gb_ref[rows, :] = (_dot(tri_tile_ref[...], ghi_ref[rows, :])
                           + _dot(tri_tile_ref[...], glo_ref[rows, :]))
        hb_ref[rows, :] = (_dot(tri_chunk_ref[...], hhi_ref[rows, :])
                           + _dot(tri_chunk_ref[...], hlo_ref[rows, :]))
        g_ok = _fast_path_ok(gb_ref.at[rows], GLA_CHUNKS)
        h_ok = _fast_path_ok(hb_ref.at[rows], HGRN_CHUNKS)
        gla_ok = g_ok if gla_ok is None else jnp.logical_and(gla_ok, g_ok)
        hgrn_ok = h_ok if hgrn_ok is None else jnp.logical_and(hgrn_ok, h_ok)

    xoff = GLA_VAL_WIDTH + HGRN_VAL_WIDTH
    gate_block = 512

    def gate_piece(c0):
        def emit():
            g = proj(OFF_GATE + c0, gate_block)
            gs_ref[:, c0:c0 + gate_block] = g * jax.nn.sigmoid(g)
        return emit

    def xq_piece():
        xq_ref[...] = proj(OFF_XQ, XATTN_WIDTH).astype(BF16)

    def xattn_piece(s, hd):
        def emit():
            rows = slice(s * MIX_TILE, (s + 1) * MIX_TILE)
            lanes = slice(hd * HEAD_K, (hd + 1) * HEAD_K)
            sc = _dot_nt(xq_ref[rows, lanes], mk_ref[:, lanes]) * (HEAD_K ** -0.5)
            p = jnp.exp(sc - jnp.max(sc, axis=-1, keepdims=True))
            o = _dot(p.astype(BF16), mv_ref[:, lanes]) / jnp.sum(p, axis=-1, keepdims=True)
            mixed_ref[rows, xoff + hd * HEAD_K:xoff + (hd + 1) * HEAD_K] = _rms(
                o, xattn_norm_ref[...])
        return emit

    def out_piece(c0):
        def emit():
            cols = slice(c0, c0 + gate_block)
            gated = (mixed_ref[:, cols] * gs_ref[:, cols]).astype(BF16)
            part = _dot(gated, w_out_ref[cols, :])
            out_ref[0] = (x_ref[0] if c0 == 0 else out_ref[0]) + part
        return emit

    def epilogue():
        if final_norm:
            out_ref[0] = _rms(out_ref[0], final_norm_ref[...])

    gates = [gate_piece(c0) for c0 in range(0, D_MIX, gate_block)]
    heads = [[xattn_piece(s, hd) for hd in range(N_HEADS)] for s in range(n_pass)]
    outs = [out_piece(c0) for c0 in range(0, D_MIX, gate_block)]
    groups = iter(_piece_order(gates, heads, outs, xq_piece, epilogue))

    def fill():
        for piece in next(groups, []):
            piece()

    gla_prev_ref[...] = gla_st_ref[...]
    hgrn_prev_ref[...] = hgrn_st_ref[...]
    for s in range(n_pass):
        rows = slice(s * MIX_TILE, (s + 1) * MIX_TILE)
        _mixer_fast(gq_ref.at[rows], gk_ref.at[rows], gv_ref.at[rows], gb_ref.at[rows],
                    gla_st_ref, gla_norm_ref[...], mixed_ref.at[rows],
                    0, GLA_HEAD_V, GLA_CHUNKS, gla_operands, fill)
        _mixer_fast(hq_ref.at[rows], hk_ref.at[rows], hv_ref.at[rows], hb_ref.at[rows],
                    hgrn_st_ref, hgrn_norm_ref[...], mixed_ref.at[rows],
                    GLA_VAL_WIDTH, HGRN_HEAD_V, HGRN_CHUNKS, hgrn_operands, fill)
    for group in groups:
        for piece in group:
            piece()

    @pl.when(jnp.logical_not(jnp.logical_and(gla_ok, hgrn_ok)))
    def _redo_with_pairwise_path():
        @pl.when(jnp.logical_not(gla_ok))
        def _gla_pairwise():
            gla_st_ref[...] = gla_prev_ref[...]
            _mixer_safe(gq_ref, gk_ref, gv_ref, gb_ref, gla_st_ref, gla_norm_ref[...],
                        mixed_ref, 0, GLA_HEAD_V, MIX_TILE // GLA_CHUNKS)

        @pl.when(jnp.logical_not(hgrn_ok))
        def _hgrn_pairwise():
            hgrn_st_ref[...] = hgrn_prev_ref[...]
            _mixer_safe(hq_ref, hk_ref, hv_ref, hb_ref, hgrn_st_ref, hgrn_norm_ref[...],
                        mixed_ref, GLA_VAL_WIDTH, HGRN_HEAD_V, MIX_TILE // HGRN_CHUNKS)

        for piece in outs:
            piece()
        epilogue()


def _layer_call(x, mem, params, layer, final_norm):
    batch, seq, _ = x.shape
    n_tiles = seq // SEQ_TILE

    def resident(arr, stacked):
        if stacked:
            shape = (None,) + arr.shape[1:]
            index = (layer,) + (0,) * (arr.ndim - 1)
        else:
            shape = arr.shape
            index = (0,) * arr.ndim
        return pl.BlockSpec(shape, lambda b, t: index, pipeline_mode=pl.Buffered(1))

    in_specs = [
        pl.BlockSpec((1, SEQ_TILE, D_MODEL), lambda b, t: (b, t, 0)),
        pl.BlockSpec((1, MEM_LEN, D_MODEL), lambda b, t: (b, 0, 0), pipeline_mode=pl.Buffered(1)),
    ] + [resident(arr, stacked) for arr, stacked in params]
    params = [arr for arr, _ in params]
    key_f32 = pltpu.VMEM((SEQ_TILE, KEY_WIDTH), F32)
    key_bf16 = pltpu.VMEM((SEQ_TILE, KEY_WIDTH), BF16)
    operand = pltpu.VMEM((MIX_TILE, KEY_WIDTH), BF16)
    half_operand = pltpu.VMEM((MIX_TILE // 2, KEY_WIDTH), BF16)
    gla_state = pltpu.VMEM((N_HEADS, GLA_HEAD_V, HEAD_K), F32)
    hgrn_state = pltpu.VMEM((N_HEADS, HGRN_HEAD_V, HEAD_K), F32)
    scratch = [
        pltpu.VMEM((SEQ_TILE, D_MODEL), BF16),
        key_f32, key_f32, key_f32,
        key_f32, key_f32, key_f32,
        pltpu.VMEM((SEQ_TILE, GLA_VAL_WIDTH), BF16),
        pltpu.VMEM((SEQ_TILE, HGRN_VAL_WIDTH), BF16),
        key_bf16, key_bf16, key_bf16, key_bf16,
        pltpu.VMEM((SEQ_TILE, D_MIX), F32),
        pltpu.VMEM((SEQ_TILE, D_MIX), F32),
        key_bf16,
        pltpu.VMEM((MEM_LEN, XATTN_WIDTH), BF16),
        pltpu.VMEM((MEM_LEN, XATTN_WIDTH), BF16),
        gla_state, hgrn_state,
        gla_state, hgrn_state,
        operand, operand, operand, operand,
        operand, operand, operand, operand, operand, operand,
        half_operand, half_operand,
        pltpu.VMEM((N_HEADS, MIX_TILE, MIX_TILE), BF16),
    ]
    return pl.pallas_call(
        functools.partial(_layer_kernel, layer=layer, final_norm=final_norm),
        out_shape=jax.ShapeDtypeStruct(x.shape, x.dtype),
        grid=(batch, n_tiles),
        in_specs=in_specs,
        out_specs=pl.BlockSpec((1, SEQ_TILE, D_MODEL), lambda b, t: (b, t, 0)),
        scratch_shapes=scratch,
        compiler_params=pltpu.CompilerParams(
            dimension_semantics=("arbitrary", "arbitrary"),
            vmem_limit_bytes=VMEM_LIMIT_BYTES),
        name=f"hybrid_layer_{layer}",
    )(x, mem, *params)


def _lower_tri_ones(segment):
    idx = jnp.arange(MIX_TILE)
    keep = (idx[None, :] <= idx[:, None]) & ((idx[None, :] // segment) == (idx[:, None] // segment))
    return keep.astype(BF16)


def kernel(x, mem, norm_w, w_in, gla_w_gate_up, gla_b_gate, gla_norm_w, hgrn_lower_bounds,
           hgrn_norm_w, mem_norm_w, w_mem_kv, xattn_norm_w, w_out, final_norm_w):
    assert x.shape[1] % SEQ_TILE == 0 and SEQ_TILE % MIX_TILE == 0
    rows = lambda v: (v.reshape(DEPTH, 1, -1).astype(F32), True)
    w_in_b = w_in.astype(BF16)
    w_a = w_in_b[:, :, :WIN_A_WIDTH]
    w_b = w_in_b[:, :, WIN_B_START:]
    w_up = jnp.pad(gla_w_gate_up, ((0, 0), (LOWRANK_LEAD, 0), (0, 0))).astype(BF16)
    params = (
        (_lower_tri_ones(MIX_TILE // GLA_CHUNKS), False),
        (_lower_tri_ones(MIX_TILE // HGRN_CHUNKS), False),
        rows(norm_w), (w_a, True), (w_b, True), (w_up, True), rows(gla_b_gate),
        rows(gla_norm_w), (hgrn_lower_bounds.astype(F32), False), rows(hgrn_norm_w),
        rows(mem_norm_w), (w_mem_kv.astype(BF16), True), rows(xattn_norm_w),
        (w_out.astype(BF16), True), (final_norm_w.reshape(1, -1).astype(F32), False),
    )
    for layer in range(DEPTH):
        x = _layer_call(x, mem, params, layer, final_norm=(layer == DEPTH - 1))
    return x
```

```python
import functools

import jax
import jax.numpy as jnp
from jax import lax
from jax.experimental import pallas as pl
from jax.experimental.pallas import tpu as pltpu

F32 = jnp.float32
BF16 = jnp.bfloat16

D_MODEL = 1024
DEPTH = 2
MEM_LEN = 256
N_HEADS = 4
HEAD_K = 128
GLA_HEAD_V = 256
HGRN_HEAD_V = 128
KEY_WIDTH = N_HEADS * HEAD_K
GLA_VAL_WIDTH = N_HEADS * GLA_HEAD_V
HGRN_VAL_WIDTH = N_HEADS * HGRN_HEAD_V
XATTN_WIDTH = N_HEADS * HEAD_K
D_MIX = 2048
GLA_LOWRANK = 16
GLA_GATE_TAU = 16.0
LB_FLOOR = 1e-30
NORM_EPS = 1e-6

LANE = 128
LOWRANK_LEAD = LANE - GLA_LOWRANK
WIN_A_WIDTH = 2 * KEY_WIDTH + GLA_VAL_WIDTH
WIN_B_START = WIN_A_WIDTH - LOWRANK_LEAD
OFF_GQ, OFF_GK, OFF_GV = 0, KEY_WIDTH, 2 * KEY_WIDTH
OFF_LR = 0
OFF_HQ = OFF_LR + LANE
OFF_HF = OFF_HQ + KEY_WIDTH
OFF_HI = OFF_HF + KEY_WIDTH
OFF_XQ = OFF_HI + HGRN_VAL_WIDTH
OFF_GATE = OFF_XQ + XATTN_WIDTH

SEQ_TILE = 512
MIX_TILE = 256
GLA_CHUNKS = 1
HGRN_CHUNKS = 4
DECAY_ROWS = 64
SAFE_BLOCK = 16
FAST_PATH_LIMIT = 60.0

VMEM_LIMIT_BYTES = 58 * 1024 * 1024


def _dot(a, b):
    return jnp.dot(a, b, preferred_element_type=F32)


def _dot_nt(a, b):
    return lax.dot_general(a, b, (((1,), (1,)), ((), ())), preferred_element_type=F32)


def _dot_tn(a, b):
    return lax.dot_general(a, b, (((0,), (0,)), ((), ())), preferred_element_type=F32)


def _rms(x, w):
    return x * lax.rsqrt(jnp.mean(x * x, axis=-1, keepdims=True) + NORM_EPS) * w


def _store_split(x, hi_ref, lo_ref, rows):
    hi = x.astype(BF16)
    hi_ref[rows, :] = hi
    lo_ref[rows, :] = (x - hi.astype(F32)).astype(BF16)


def _column(row):
    return jnp.broadcast_to(row, (8, row.shape[1])).T[:, 0:1]


def _mixer_fast(q_ref, k_ref, vb_ref, b_ref, st_ref, norm_w, mixed_ref, out_off, dv, n_chunks,
                operand_refs, fill):
    assert n_chunks in (1, 4)
    qt_ref, kt_ref, qo_ref, ko_ref, qi_ref, ks_ref, ql_ref, kl_ref, p_ref = operand_refs
    chunk = MIX_TILE // n_chunks
    half = MIX_TILE // 2
    tot = [b_ref[(c + 1) * chunk - 1:(c + 1) * chunk, :] for c in range(n_chunks)]
    ref = [b_ref[c * chunk + chunk // 2 - 1:c * chunk + chunk // 2, :] for c in range(n_chunks)]
    before = [jnp.zeros_like(tot[0])]
    for c in range(n_chunks):
        before.append(before[-1] + tot[c])
    tile_total = before[n_chunks]
    decay_col = _column(jnp.exp(tile_total))
    if n_chunks == 4:
        e_in = [None] + [jnp.exp(before[c]) for c in range(1, 4)]
        e_out = [jnp.exp(tile_total - before[c + 1]) for c in range(3)] + [None]
        e_tot1 = jnp.exp(tot[1])
        e_tot2 = jnp.exp(tot[2])

    for c in range(n_chunks):
        rows = slice(c * chunk, (c + 1) * chunk)
        for h in range(N_HEADS):
            lanes = slice(h * HEAD_K, (h + 1) * HEAD_K)
            b = b_ref[rows, lanes]
            q = q_ref[rows, lanes]
            k = k_ref[rows, lanes]
            r = ref[c][:, lanes]
            qe = q * jnp.exp(b)
            ke = k * jnp.exp(tot[c][:, lanes] - b)
            qt_ref[rows, lanes] = (q * jnp.exp(b - r)).astype(BF16)
            kt_ref[rows, lanes] = (k * jnp.exp(r - b)).astype(BF16)
            if n_chunks == 1:
                qi_ref[rows, lanes] = qe.astype(BF16)
                ks_ref[rows, lanes] = ke.astype(BF16)
            else:
                pair = slice((c // 2) * chunk, (c // 2 + 1) * chunk)
                if c % 2 == 1:
                    qo_ref[pair, lanes] = qe.astype(BF16)
                else:
                    ko_ref[pair, lanes] = ke.astype(BF16)
                qi = qe if c == 0 else qe * e_in[c][:, lanes]
                ks = ke if c == 3 else ke * e_out[c][:, lanes]
                qi_ref[rows, lanes] = qi.astype(BF16)
                ks_ref[rows, lanes] = ks.astype(BF16)
                if c == 0:
                    kl_ref[0:chunk, lanes] = (ke * e_tot1[:, lanes]).astype(BF16)
                elif c == 1:
                    kl_ref[chunk:half, lanes] = ke.astype(BF16)
                elif c == 2:
                    ql_ref[0:chunk, lanes] = qe.astype(BF16)
                else:
                    ql_ref[chunk:half, lanes] = (qe * e_tot2[:, lanes]).astype(BF16)
    fill()

    if n_chunks == 1:
        row = lax.broadcasted_iota(jnp.int32, (MIX_TILE, MIX_TILE), 0)
        col = lax.broadcasted_iota(jnp.int32, (MIX_TILE, MIX_TILE), 1)
        causal = row >= col
        for h in range(N_HEADS):
            lanes = slice(h * HEAD_K, (h + 1) * HEAD_K)
            vl = slice(h * dv, (h + 1) * dv)
            s = _dot_nt(qt_ref[:, lanes], kt_ref[:, lanes])
            p = jnp.where(causal, s, 0.0).astype(BF16)
            st = st_ref[h]
            o = _dot(p, vb_ref[:, vl]) + _dot(qi_ref[:, lanes], st.astype(BF16))
            st_ref[h] = st * decay_col[lanes, :] + _dot_tn(ks_ref[:, lanes], vb_ref[:, vl])
            mixed_ref[:, out_off + h * dv:out_off + (h + 1) * dv] = _rms(o, norm_w)
            if h % 2 == 1:
                fill()
        return

    row = lax.broadcasted_iota(jnp.int32, (half, half), 0)
    col = lax.broadcasted_iota(jnp.int32, (half, half), 1)
    same_chunk = ((row >= chunk) == (col >= chunk)) & (row >= col)
    for h in range(N_HEADS):
        lanes = slice(h * HEAD_K, (h + 1) * HEAD_K)
        for hf in range(2):
            rows = slice(hf * half, (hf + 1) * half)
            d = _dot_nt(qt_ref[rows, lanes], kt_ref[rows, lanes])
            p_ref[h, rows, rows] = jnp.where(same_chunk, d, 0.0).astype(BF16)
            pair = slice(hf * chunk, (hf + 1) * chunk)
            lo = hf * half
            p_ref[h, lo + chunk:lo + half, lo:lo + chunk] = _dot_nt(
                qo_ref[pair, lanes], ko_ref[pair, lanes]).astype(BF16)
        p_ref[h, half:, 0:half] = _dot_nt(ql_ref[:, lanes], kl_ref[:, lanes]).astype(BF16)
    fill()
    for h in range(N_HEADS):
        lanes = slice(h * HEAD_K, (h + 1) * HEAD_K)
        vl = slice(h * dv, (h + 1) * dv)
        ol = slice(out_off + h * dv, out_off + (h + 1) * dv)
        st = st_ref[h]
        st_b = st.astype(BF16)
        mixed_ref[0:half, ol] = (_dot(p_ref[h, 0:half, 0:half], vb_ref[0:half, vl])
                                 + _dot(qi_ref[0:half, lanes], st_b))
        mixed_ref[half:, ol] = (_dot(p_ref[h, half:, :], vb_ref[:, vl])
                                + _dot(qi_ref[half:, lanes], st_b))
        st_ref[h] = st * decay_col[lanes, :] + _dot_tn(ks_ref[:, lanes], vb_ref[:, vl])
    fill()
    for h in range(N_HEADS):
        ol = slice(out_off + h * dv, out_off + (h + 1) * dv)
        mixed_ref[:, ol] = _rms(mixed_ref[:, ol], norm_w)


def _mixer_safe(q_ref, k_ref, vb_ref, b_ref, st_ref, norm_w, mixed_ref, out_off, dv, segment):
    row16 = lax.broadcasted_iota(jnp.int32, (SAFE_BLOCK, 1), 0)

    def block_body(sb, carry):
        s0 = pl.multiple_of(sb * SAFE_BLOCK, SAFE_BLOCK)
        rows_b = pl.ds(s0, SAFE_BLOCK)
        before = pl.multiple_of(jnp.maximum(s0 - 8, 0), 8)
        prev = b_ref[pl.ds(before, 8), :][7:8, :]
        prev = jnp.where(s0 % segment == 0, 0.0, prev)
        for h in range(N_HEADS):
            lanes = slice(h * HEAD_K, (h + 1) * HEAD_K)
            vl = slice(h * dv, (h + 1) * dv)
            bl = b_ref[rows_b, lanes] - prev[:, lanes]
            q = q_ref[rows_b, lanes]
            k = k_ref[rows_b, lanes]
            vb = vb_ref[rows_b, vl]
            v = vb.astype(F32)
            o = jnp.zeros((SAFE_BLOCK, dv), F32)
            for j in range(SAFE_BLOCK):
                w = jnp.exp(jnp.minimum(bl - bl[j:j + 1, :], 0.0))
                sc = jnp.sum(q * k[j:j + 1, :] * w, axis=-1, keepdims=True)
                o = o + jnp.where(row16 >= j, sc, 0.0) * v[j:j + 1, :]
            st = st_ref[h]
            o = o + _dot((q * jnp.exp(bl)).astype(BF16), st.astype(BF16))
            b_end = bl[SAFE_BLOCK - 1:SAFE_BLOCK, :]
            kl = (k * jnp.exp(b_end - bl)).astype(BF16)
            st_ref[h] = st * _column(jnp.exp(b_end)) + _dot_tn(kl, vb)
            mixed_ref[rows_b, out_off + h * dv:out_off + (h + 1) * dv] = _rms(o, norm_w)
        return carry

    lax.fori_loop(0, SEQ_TILE // SAFE_BLOCK, block_body, 0)


def _fast_path_ok(b_ref, n_chunks):
    chunk = MIX_TILE // n_chunks
    worst = jnp.zeros((1, KEY_WIDTH), F32)
    for c in range(n_chunks):
        r = b_ref[c * chunk + chunk // 2 - 1:c * chunk + chunk // 2, :]
        total = b_ref[(c + 1) * chunk - 1:(c + 1) * chunk, :]
        worst = jnp.maximum(worst, jnp.maximum(-r, r - total))
    return jnp.max(worst) < FAST_PATH_LIMIT


def _piece_order(gates, heads, outs, xq, epilogue):
    h0, h1 = heads
    return [[gates[0]], [gates[1]], [xq],
            [gates[2]], [gates[3]], [h0[0], h0[1]],
            [h0[2]], [h0[3]], [h1[0]],
            [outs[0]], [outs[1]], [h1[1]],
            [h1[2], h1[3], outs[2], outs[3], epilogue]]


def _layer_kernel(x_ref, mem_ref, tri_tile_ref, tri_chunk_ref,
                  norm_w_ref, w_a_ref, w_b_ref, w_up_ref, b_gate_ref,
                  gla_norm_ref, lb_ref, hgrn_norm_ref, mem_norm_ref, w_mem_ref, xattn_norm_ref,
                  w_out_ref, final_norm_ref,
                  out_ref,
                  h_ref, gq_ref, gk_ref, gb_ref, hq_ref, hk_ref, hb_ref,
                  gv_ref, hv_ref, ghi_ref, glo_ref, hhi_ref, hlo_ref,
                  mixed_ref, gs_ref, xq_ref, mk_ref, mv_ref,
                  gla_st_ref, hgrn_st_ref, gla_prev_ref, hgrn_prev_ref,
                  g_qt_ref, g_kt_ref, g_qi_ref, g_ks_ref,
                  h_qt_ref, h_kt_ref, h_qo_ref, h_ko_ref, h_qi_ref, h_ks_ref, h_ql_ref, h_kl_ref,
                  p_ref,
                  *, layer, final_norm):
    t = pl.program_id(1)
    n_pass = SEQ_TILE // MIX_TILE
    gla_operands = (g_qt_ref, g_kt_ref, None, None, g_qi_ref, g_ks_ref, None, None, None)
    hgrn_operands = (h_qt_ref, h_kt_ref, h_qo_ref, h_ko_ref, h_qi_ref, h_ks_ref, h_ql_ref,
                     h_kl_ref, p_ref)
    zg_ref, zf_ref = gb_ref, hk_ref

    @pl.when(t == 0)
    def _start_of_sequence():
        gla_st_ref[...] = jnp.zeros_like(gla_st_ref)
        hgrn_st_ref[...] = jnp.zeros_like(hgrn_st_ref)
        m = _rms(mem_ref[0], mem_norm_ref[...]).astype(BF16)
        mk_ref[...] = _dot(m, w_mem_ref[:, :XATTN_WIDTH]).astype(BF16)
        mv_ref[...] = _dot(m, w_mem_ref[:, XATTN_WIDTH:]).astype(BF16)

    h_ref[...] = _rms(x_ref[0], norm_w_ref[...]).astype(BF16)

    def proj_a(off, width):
        return _dot(h_ref[...], w_a_ref[:, off:off + width])

    def proj(off, width):
        return _dot(h_ref[...], w_b_ref[:, off:off + width])

    lr_hq = proj(OFF_LR, LANE + KEY_WIDTH)
    low_rank = lr_hq[:, :LANE].astype(BF16)
    hq_ref[...] = lr_hq[:, LANE:]
    zg_ref[...] = _dot(low_rank, w_up_ref[...]) + b_gate_ref[...]
    zf_ref[...] = proj(OFF_HF, KEY_WIDTH)

    lbw = lb_ref[...]
    lb_e = jnp.exp(lbw - jnp.max(lbw, axis=0, keepdims=True))
    lb_p = lb_e / jnp.sum(lb_e, axis=0, keepdims=True)
    lb = jnp.sum(lb_p[:layer + 1], axis=0, keepdims=True) - lb_p[0:1]
    log_lb = jnp.log(jnp.maximum(lb, LB_FLOOR))
    log_1m_lb = jnp.log1p(-lb)
    one_m_lb = 1.0 - lb

    def decay(j):
        rows = slice(j * DECAY_ROWS, (j + 1) * DECAY_ROWS)
        z = zg_ref[rows, :]
        ls = jnp.minimum(z, 0.0) - jnp.log(1.0 + jnp.exp(-jnp.abs(z)))
        _store_split(ls * (1.0 / GLA_GATE_TAU), ghi_ref, glo_ref, rows)
        z = zf_ref[rows, :]
        e = jnp.exp(-jnp.abs(z))
        u = 1.0 + e
        c = log_1m_lb + (jnp.minimum(z, 0.0) - jnp.log(u))
        log_f = jnp.maximum(log_lb, c) + jnp.log(1.0 + jnp.exp(-jnp.abs(log_lb - c)))
        _store_split(log_f, hhi_ref, hlo_ref, rows)
        hk_ref[rows, :] = one_m_lb * (jnp.where(z >= 0.0, e, 1.0) / u)

    def gq_piece():
        gq_ref[...] = proj_a(OFF_GQ, KEY_WIDTH) * (HEAD_K ** -0.5)

    def gk_piece():
        gk_ref[...] = proj_a(OFF_GK, KEY_WIDTH)

    def gv_piece(c0):
        def emit():
            gv_ref[:, c0:c0 + KEY_WIDTH] = proj_a(OFF_GV + c0, KEY_WIDTH).astype(BF16)
        return emit

    def hv_piece():
        hv_ref[...] = proj(OFF_HI, HGRN_VAL_WIDTH).astype(BF16)

    matmul_pieces = [gq_piece, gk_piece, gv_piece(0), gv_piece(KEY_WIDTH), hv_piece]
    n_decay = SEQ_TILE // DECAY_ROWS
    for i, piece in enumerate(matmul_pieces):
        piece()
        for j in range(i * n_decay // len(matmul_pieces), (i + 1) * n_decay // len(matmul_pieces)):
            decay(j)

    gla_ok = hgrn_ok = None
    for s in range(n_pass):
        rows = slice(s * MIX_TILE, (s + 1) * MIX_TILE)
        gb_ref[rows, :] = (_dot(tri_tile_ref[...], ghi_ref[rows, :])
                           + _dot(tri_tile_ref[...], glo_ref[rows, :]))
        hb_ref[rows, :] = (_dot(tri_chunk_ref[...], hhi_ref[rows, :])
                           + _dot(tri_chunk_ref[...], hlo_ref[rows, :]))
        g_ok = _fast_path_ok(gb_ref.at[rows], GLA_CHUNKS)
        h_ok = _fast_path_ok(hb_ref.at[rows], HGRN_CHUNKS)
        gla_ok = g_ok if gla_ok is None else jnp.logical_and(gla_ok, g_ok)
        hgrn_ok = h_ok if hgrn_ok is None else jnp.logical_and(hgrn_ok, h_ok)

    xoff = GLA_VAL_WIDTH + HGRN_VAL_WIDTH
    gate_block = 512

    def gate_piece(c0):
        def emit():
            g = proj(OFF_GATE + c0, gate_block)
            gs_ref[:, c0:c0 + gate_block] = g * jax.nn.sigmoid(g)
        return emit

    def xq_piece():
        xq_ref[...] = proj(OFF_XQ, XATTN_WIDTH).astype(BF16)

    def xattn_piece(s, hd):
        def emit():
            rows = slice(s * MIX_TILE, (s + 1) * MIX_TILE)
            lanes = slice(hd * HEAD_K, (hd + 1) * HEAD_K)
            sc = _dot_nt(xq_ref[rows, lanes], mk_ref[:, lanes]) * (HEAD_K ** -0.5)
            p = jnp.exp(sc - jnp.max(sc, axis=-1, keepdims=True))
            o = _dot(p.astype(BF16), mv_ref[:, lanes]) / jnp.sum(p, axis=-1, keepdims=True)
            mixed_ref[rows, xoff + hd * HEAD_K:xoff + (hd + 1) * HEAD_K] = _rms(
                o, xattn_norm_ref[...])
        return emit

    def out_piece(c0):
        def emit():
            cols = slice(c0, c0 + gate_block)
            gated = (mixed_ref[:, cols] * gs_ref[:, cols]).astype(BF16)
            part = _dot(gated, w_out_ref[cols, :])
            out_ref[0] = (x_ref[0] if c0 == 0 else out_ref[0]) + part
        return emit

    def epilogue():
        if final_norm:
            out_ref[0] = _rms(out_ref[0], final_norm_ref[...])

    gates = [gate_piece(c0) for c0 in range(0, D_MIX, gate_block)]
    heads = [[xattn_piece(s, hd) for hd in range(N_HEADS)] for s in range(n_pass)]
    outs = [out_piece(c0) for c0 in range(0, D_MIX, gate_block)]
    groups = iter(_piece_order(gates, heads, outs, xq_piece, epilogue))

    def fill():
        for piece in next(groups, []):
            piece()

    gla_prev_ref[...] = gla_st_ref[...]
    hgrn_prev_ref[...] = hgrn_st_ref[...]
    for s in range(n_pass):
        rows = slice(s * MIX_TILE, (s + 1) * MIX_TILE)
        _mixer_fast(gq_ref.at[rows], gk_ref.at[rows], gv_ref.at[rows], gb_ref.at[rows],
                    gla_st_ref, gla_norm_ref[...], mixed_ref.at[rows],
                    0, GLA_HEAD_V, GLA_CHUNKS, gla_operands, fill)
        _mixer_fast(hq_ref.at[rows], hk_ref.at[rows], hv_ref.at[rows], hb_ref.at[rows],
                    hgrn_st_ref, hgrn_norm_ref[...], mixed_ref.at[rows],
                    GLA_VAL_WIDTH, HGRN_HEAD_V, HGRN_CHUNKS, hgrn_operands, fill)
    for group in groups:
        for piece in group:
            piece()

    @pl.when(jnp.logical_not(jnp.logical_and(gla_ok, hgrn_ok)))
    def _redo_with_pairwise_path():
        @pl.when(jnp.logical_not(gla_ok))
        def _gla_pairwise():
            gla_st_ref[...] = gla_prev_ref[...]
            _mixer_safe(gq_ref, gk_ref, gv_ref, gb_ref, gla_st_ref, gla_norm_ref[...],
                        mixed_ref, 0, GLA_HEAD_V, MIX_TILE // GLA_CHUNKS)

        @pl.when(jnp.logical_not(hgrn_ok))
        def _hgrn_pairwise():
            hgrn_st_ref[...] = hgrn_prev_ref[...]
            _mixer_safe(hq_ref, hk_ref, hv_ref, hb_ref, hgrn_st_ref, hgrn_norm_ref[...],
                        mixed_ref, GLA_VAL_WIDTH, HGRN_HEAD_V, MIX_TILE // HGRN_CHUNKS)

        for piece in outs:
            piece()
        epilogue()


def _layer_call(x, mem, params, layer, final_norm):
    batch, seq, _ = x.shape
    n_tiles = seq // SEQ_TILE

    def resident(arr, stacked):
        if stacked:
            shape = (None,) + arr.shape[1:]
            index = (layer,) + (0,) * (arr.ndim - 1)
        else:
            shape = arr.shape
            index = (0,) * arr.ndim
        return pl.BlockSpec(shape, lambda b, t: index, pipeline_mode=pl.Buffered(1))

    in_specs = [
        pl.BlockSpec((1, SEQ_TILE, D_MODEL), lambda b, t: (b, t, 0)),
        pl.BlockSpec((1, MEM_LEN, D_MODEL), lambda b, t: (b, 0, 0), pipeline_mode=pl.Buffered(1)),
    ] + [resident(arr, stacked) for arr, stacked in params]
    params = [arr for arr, _ in params]
    key_f32 = pltpu.VMEM((SEQ_TILE, KEY_WIDTH), F32)
    key_bf16 = pltpu.VMEM((SEQ_TILE, KEY_WIDTH), BF16)
    operand = pltpu.VMEM((MIX_TILE, KEY_WIDTH), BF16)
    half_operand = pltpu.VMEM((MIX_TILE // 2, KEY_WIDTH), BF16)
    gla_state = pltpu.VMEM((N_HEADS, HEAD_K, GLA_HEAD_V), F32)
    hgrn_state = pltpu.VMEM((N_HEADS, HEAD_K, HGRN_HEAD_V), F32)
    scratch = [
        pltpu.VMEM((SEQ_TILE, D_MODEL), BF16),
        key_f32, key_f32, key_f32,
        key_f32, key_f32, key_f32,
        pltpu.VMEM((SEQ_TILE, GLA_VAL_WIDTH), BF16),
        pltpu.VMEM((SEQ_TILE, HGRN_VAL_WIDTH), BF16),
        key_bf16, key_bf16, key_bf16, key_bf16,
        pltpu.VMEM((SEQ_TILE, D_MIX), F32),
        pltpu.VMEM((SEQ_TILE, D_MIX), F32),
        key_bf16,
        pltpu.VMEM((MEM_LEN, XATTN_WIDTH), BF16),
        pltpu.VMEM((MEM_LEN, XATTN_WIDTH), BF16),
        gla_state, hgrn_state,
        gla_state, hgrn_state,
        operand, operand, operand, operand,
        operand, operand, half_operand, half_operand, operand, operand,
        half_operand, half_operand,
        pltpu.VMEM((N_HEADS, MIX_TILE, MIX_TILE), BF16),
    ]
    return pl.pallas_call(
        functools.partial(_layer_kernel, layer=layer, final_norm=final_norm),
        out_shape=jax.ShapeDtypeStruct(x.shape, x.dtype),
        grid=(batch, n_tiles),
        in_specs=in_specs,
        out_specs=pl.BlockSpec((1, SEQ_TILE, D_MODEL), lambda b, t: (b, t, 0)),
        scratch_shapes=scratch,
        compiler_params=pltpu.CompilerParams(
            dimension_semantics=("arbitrary", "arbitrary"),
            vmem_limit_bytes=VMEM_LIMIT_BYTES),
        name=f"hybrid_layer_{layer}",
    )(x, mem, *params)


def _lower_tri_ones(segment):
    idx = jnp.arange(MIX_TILE)
    keep = (idx[None, :] <= idx[:, None]) & ((idx[None, :] // segment) == (idx[:, None] // segment))
    return keep.astype(BF16)


def kernel(x, mem, norm_w, w_in, gla_w_gate_up, gla_b_gate, gla_norm_w, hgrn_lower_bounds,
           hgrn_norm_w, mem_norm_w, w_mem_kv, xattn_norm_w, w_out, final_norm_w):
    assert x.shape[1] % SEQ_TILE == 0 and SEQ_TILE % MIX_TILE == 0
    rows = lambda v: (v.reshape(DEPTH, 1, -1).astype(F32), True)
    w_in_b = lax.optimization_barrier(w_in.astype(BF16))
    w_a = w_in_b[:, :, :WIN_A_WIDTH]
    w_b = w_in_b[:, :, WIN_B_START:]
    w_up = jnp.pad(gla_w_gate_up, ((0, 0), (LOWRANK_LEAD, 0), (0, 0))).astype(BF16)
    params = (
        (_lower_tri_ones(MIX_TILE // GLA_CHUNKS), False),
        (_lower_tri_ones(MIX_TILE // HGRN_CHUNKS), False),
        rows(norm_w), (w_a, True), (w_b, True), (w_up, True), rows(gla_b_gate),
        rows(gla_norm_w), (hgrn_lower_bounds.astype(F32), False), rows(hgrn_norm_w),
        rows(mem_norm_w), (w_mem_kv.astype(BF16), True), rows(xattn_norm_w),
        (w_out.astype(BF16), True), (final_norm_w.reshape(1, -1).astype(F32), False),
    )
    for layer in range(DEPTH):
        x = _layer_call(x, mem, params, layer, final_norm=(layer == DEPTH - 1))
    return x
```

```python
import functools

import jax
import jax.numpy as jnp
from jax import lax
from jax.experimental import pallas as pl
from jax.experimental.pallas import tpu as pltpu

F32 = jnp.float32
BF16 = jnp.bfloat16

D_MODEL = 1024
DEPTH = 2
MEM_LEN = 256
N_HEADS = 4
HEAD_K = 128
GLA_HEAD_V = 256
HGRN_HEAD_V = 128
KEY_WIDTH = N_HEADS * HEAD_K
GLA_VAL_WIDTH = N_HEADS * GLA_HEAD_V
HGRN_VAL_WIDTH = N_HEADS * HGRN_HEAD_V
XATTN_WIDTH = N_HEADS * HEAD_K
D_MIX = 2048
GLA_LOWRANK = 16
GLA_GATE_TAU = 16.0
LB_FLOOR = 1e-30
NORM_EPS = 1e-6

LANE = 128
LOWRANK_LEAD = LANE - GLA_LOWRANK
WIN_A_WIDTH = 2 * KEY_WIDTH + GLA_VAL_WIDTH
WIN_B_START = WIN_A_WIDTH - LOWRANK_LEAD
OFF_GQ, OFF_GK, OFF_GV = 0, KEY_WIDTH, 2 * KEY_WIDTH
OFF_LR = 0
OFF_HQ = OFF_LR + LANE
OFF_HF = OFF_HQ + KEY_WIDTH
OFF_HI = OFF_HF + KEY_WIDTH
OFF_XQ = OFF_HI + HGRN_VAL_WIDTH
OFF_GATE = OFF_XQ + XATTN_WIDTH

SEQ_TILE = 512
MIX_TILE = 256
GLA_CHUNKS = 1
HGRN_CHUNKS = 4
DECAY_ROWS = 64
SAFE_BLOCK = 16
FAST_PATH_LIMIT = 60.0

WINDOW_ROWS = 128

VMEM_LIMIT_BYTES = 58 * 1024 * 1024


def _dot(a, b):
    return jnp.dot(a, b, preferred_element_type=F32)


def _dot_nt(a, b):
    return lax.dot_general(a, b, (((1,), (1,)), ((), ())), preferred_element_type=F32)


def _dot_tn(a, b):
    return lax.dot_general(a, b, (((0,), (0,)), ((), ())), preferred_element_type=F32)


def _rms(x, w):
    return x * lax.rsqrt(jnp.mean(x * x, axis=-1, keepdims=True) + NORM_EPS) * w


def _store_split(x, hi_ref, lo_ref, rows):
    hi = x.astype(BF16)
    hi_ref[rows, :] = hi
    lo_ref[rows, :] = (x - hi.astype(F32)).astype(BF16)


def _column(row):
    return jnp.broadcast_to(row, (8, row.shape[1])).T[:, 0:1]


def _mixer_fast(q_ref, k_ref, vb_ref, b_ref, st_ref, norm_w, mixed_ref, out_off, dv, n_chunks,
                operand_refs, fill):
    assert n_chunks in (1, 4)
    qt_ref, kt_ref, qo_ref, ko_ref, qi_ref, ks_ref, ql_ref, kl_ref, p_ref = operand_refs
    chunk = MIX_TILE // n_chunks
    half = MIX_TILE // 2
    tot = [b_ref[(c + 1) * chunk - 1:(c + 1) * chunk, :] for c in range(n_chunks)]
    ref = [b_ref[c * chunk + chunk // 2 - 1:c * chunk + chunk // 2, :] for c in range(n_chunks)]
    before = [jnp.zeros_like(tot[0])]
    for c in range(n_chunks):
        before.append(before[-1] + tot[c])
    tile_total = before[n_chunks]
    decay_col = _column(jnp.exp(tile_total))
    if n_chunks == 4:
        e_in = [None] + [jnp.exp(before[c]) for c in range(1, 4)]
        e_out = [jnp.exp(tile_total - before[c + 1]) for c in range(3)] + [None]
        e_tot1 = jnp.exp(tot[1])
        e_tot2 = jnp.exp(tot[2])

    for c in range(n_chunks):
        rows = slice(c * chunk, (c + 1) * chunk)
        for h in range(N_HEADS):
            lanes = slice(h * HEAD_K, (h + 1) * HEAD_K)
            b = b_ref[rows, lanes]
            q = q_ref[rows, lanes]
            k = k_ref[rows, lanes]
            r = ref[c][:, lanes]
            qe = q * jnp.exp(b)
            ke = k * jnp.exp(tot[c][:, lanes] - b)
            qt_ref[rows, lanes] = (q * jnp.exp(b - r)).astype(BF16)
            kt_ref[rows, lanes] = (k * jnp.exp(r - b)).astype(BF16)
            if n_chunks == 1:
                qi_ref[rows, lanes] = qe.astype(BF16)
                ks_ref[rows, lanes] = ke.astype(BF16)
            else:
                pair = slice((c // 2) * chunk, (c // 2 + 1) * chunk)
                if c % 2 == 1:
                    qo_ref[pair, lanes] = qe.astype(BF16)
                else:
                    ko_ref[pair, lanes] = ke.astype(BF16)
                qi = qe if c == 0 else qe * e_in[c][:, lanes]
                ks = ke if c == 3 else ke * e_out[c][:, lanes]
                qi_ref[rows, lanes] = qi.astype(BF16)
                ks_ref[rows, lanes] = ks.astype(BF16)
                if c == 0:
                    kl_ref[0:chunk, lanes] = (ke * e_tot1[:, lanes]).astype(BF16)
                elif c == 1:
                    kl_ref[chunk:half, lanes] = ke.astype(BF16)
                elif c == 2:
                    ql_ref[0:chunk, lanes] = qe.astype(BF16)
                else:
                    ql_ref[chunk:half, lanes] = (qe * e_tot2[:, lanes]).astype(BF16)
    fill()

    if n_chunks == 1:
        row = lax.broadcasted_iota(jnp.int32, (MIX_TILE, MIX_TILE), 0)
        col = lax.broadcasted_iota(jnp.int32, (MIX_TILE, MIX_TILE), 1)
        causal = row >= col
        for h in range(N_HEADS):
            lanes = slice(h * HEAD_K, (h + 1) * HEAD_K)
            vl = slice(h * dv, (h + 1) * dv)
            s = _dot_nt(qt_ref[:, lanes], kt_ref[:, lanes])
            p = jnp.where(causal, s, 0.0).astype(BF16)
            st = st_ref[h]
            o = _dot(p, vb_ref[:, vl]) + _dot(qi_ref[:, lanes], st.astype(BF16))
            st_ref[h] = st * decay_col[lanes, :] + _dot_tn(ks_ref[:, lanes], vb_ref[:, vl])
            mixed_ref[:, out_off + h * dv:out_off + (h + 1) * dv] = _rms(o, norm_w)
            if h % 2 == 1:
                fill()
        return

    row = lax.broadcasted_iota(jnp.int32, (half, half), 0)
    col = lax.broadcasted_iota(jnp.int32, (half, half), 1)
    same_chunk = ((row >= chunk) == (col >= chunk)) & (row >= col)
    for h in range(N_HEADS):
        lanes = slice(h * HEAD_K, (h + 1) * HEAD_K)
        for hf in range(2):
            rows = slice(hf * half, (hf + 1) * half)
            d = _dot_nt(qt_ref[rows, lanes], kt_ref[rows, lanes])
            p_ref[h, rows, rows] = jnp.where(same_chunk, d, 0.0).astype(BF16)
            pair = slice(hf * chunk, (hf + 1) * chunk)
            lo = hf * half
            p_ref[h, lo + chunk:lo + half, lo:lo + chunk] = _dot_nt(
                qo_ref[pair, lanes], ko_ref[pair, lanes]).astype(BF16)
        p_ref[h, half:, 0:half] = _dot_nt(ql_ref[:, lanes], kl_ref[:, lanes]).astype(BF16)
    fill()
    for h in range(N_HEADS):
        lanes = slice(h * HEAD_K, (h + 1) * HEAD_K)
        vl = slice(h * dv, (h + 1) * dv)
        ol = slice(out_off + h * dv, out_off + (h + 1) * dv)
        st = st_ref[h]
        st_b = st.astype(BF16)
        mixed_ref[0:half, ol] = (_dot(p_ref[h, 0:half, 0:half], vb_ref[0:half, vl])
                                 + _dot(qi_ref[0:half, lanes], st_b))
        mixed_ref[half:, ol] = (_dot(p_ref[h, half:, :], vb_ref[:, vl])
                                + _dot(qi_ref[half:, lanes], st_b))
        st_ref[h] = st * decay_col[lanes, :] + _dot_tn(ks_ref[:, lanes], vb_ref[:, vl])
    fill()
    for h in range(N_HEADS):
        ol = slice(out_off + h * dv, out_off + (h + 1) * dv)
        mixed_ref[:, ol] = _rms(mixed_ref[:, ol], norm_w)


def _mixer_safe(q_ref, k_ref, vb_ref, b_ref, st_ref, norm_w, mixed_ref, out_off, dv, segment):
    row16 = lax.broadcasted_iota(jnp.int32, (SAFE_BLOCK, 1), 0)

    def block_body(sb, carry):
        s0 = pl.multiple_of(sb * SAFE_BLOCK, SAFE_BLOCK)
        rows_b = pl.ds(s0, SAFE_BLOCK)
        before = pl.multiple_of(jnp.maximum(s0 - 8, 0), 8)
        prev = b_ref[pl.ds(before, 8), :][7:8, :]
        prev = jnp.where(s0 % segment == 0, 0.0, prev)
        for h in range(N_HEADS):
            lanes = slice(h * HEAD_K, (h + 1) * HEAD_K)
            vl = slice(h * dv, (h + 1) * dv)
            bl = b_ref[rows_b, lanes] - prev[:, lanes]
            q = q_ref[rows_b, lanes]
            k = k_ref[rows_b, lanes]
            vb = vb_ref[rows_b, vl]
            v = vb.astype(F32)
            o = jnp.zeros((SAFE_BLOCK, dv), F32)
            for j in range(SAFE_BLOCK):
                w = jnp.exp(jnp.minimum(bl - bl[j:j + 1, :], 0.0))
                sc = jnp.sum(q * k[j:j + 1, :] * w, axis=-1, keepdims=True)
                o = o + jnp.where(row16 >= j, sc, 0.0) * v[j:j + 1, :]
            st = st_ref[h]
            o = o + _dot((q * jnp.exp(bl)).astype(BF16), st.astype(BF16))
            b_end = bl[SAFE_BLOCK - 1:SAFE_BLOCK, :]
            kl = (k * jnp.exp(b_end - bl)).astype(BF16)
            st_ref[h] = st * _column(jnp.exp(b_end)) + _dot_tn(kl, vb)
            mixed_ref[rows_b, out_off + h * dv:out_off + (h + 1) * dv] = _rms(o, norm_w)
        return carry

    lax.fori_loop(0, SEQ_TILE // SAFE_BLOCK, block_body, 0)


def _fast_path_ok(b_ref, n_chunks):
    chunk = MIX_TILE // n_chunks
    worst = jnp.zeros((1, KEY_WIDTH), F32)
    for c in range(n_chunks):
        r = b_ref[c * chunk + chunk // 2 - 1:c * chunk + chunk // 2, :]
        total = b_ref[(c + 1) * chunk - 1:(c + 1) * chunk, :]
        worst = jnp.maximum(worst, jnp.maximum(-r, r - total))
    return jnp.max(worst) < FAST_PATH_LIMIT


def _piece_order(gates, heads, outs, xq, epilogue):
    h0, h1 = heads
    return [[gates[0]], [gates[1]], [xq],
            [gates[2]], [gates[3]], [h0[0], h0[1]],
            [h0[2]], [h0[3]], [h1[0]],
            [outs[0]], [outs[1]], [h1[1]],
            [h1[2], h1[3], outs[2], outs[3], epilogue]]


def _layer_kernel(x_ref, mem_ref, tri_tile_ref, tri_chunk_ref,
                  norm_w_ref, w_a_ref, w_b_ref, w_up_ref, b_gate_ref,
                  gla_norm_ref, lb_ref, hgrn_norm_ref, mem_norm_ref, w_mem_ref, xattn_norm_ref,
                  w_out_ref, final_norm_ref,
                  out_ref,
                  h_ref, gq_ref, gk_ref, gb_ref, hq_ref, hk_ref, hb_ref,
                  gv_ref, hv_ref, ghi_ref, glo_ref, hhi_ref, hlo_ref,
                  mixed_ref, gs_ref, xq_ref, mk_ref, mv_ref,
                  gla_st_ref, hgrn_st_ref, gla_prev_ref, hgrn_prev_ref,
                  g_qt_ref, g_kt_ref, g_qi_ref, g_ks_ref,
                  h_qt_ref, h_kt_ref, h_qo_ref, h_ko_ref, h_qi_ref, h_ks_ref, h_ql_ref, h_kl_ref,
                  p_ref,
                  *, layer, final_norm):
    t = pl.program_id(1)
    n_pass = SEQ_TILE // MIX_TILE
    gla_operands = (g_qt_ref, g_kt_ref, None, None, g_qi_ref, g_ks_ref, None, None, None)
    hgrn_operands = (h_qt_ref, h_kt_ref, h_qo_ref, h_ko_ref, h_qi_ref, h_ks_ref, h_ql_ref,
                     h_kl_ref, p_ref)
    zg_ref, zf_ref = gb_ref, hk_ref

    @pl.when(t == 0)
    def _start_of_sequence():
        gla_st_ref[...] = jnp.zeros_like(gla_st_ref)
        hgrn_st_ref[...] = jnp.zeros_like(hgrn_st_ref)
        m = _rms(mem_ref[0], mem_norm_ref[...]).astype(BF16)
        mk_ref[...] = _dot(m, w_mem_ref[:, :XATTN_WIDTH]).astype(BF16)
        mv_ref[...] = _dot(m, w_mem_ref[:, XATTN_WIDTH:]).astype(BF16)

    h_ref[...] = _rms(x_ref[0], norm_w_ref[...]).astype(BF16)

    def proj_a(off, width):
        return _dot(h_ref[...], w_a_ref[:, off:off + width])

    def proj(off, width):
        return _dot(h_ref[...], w_b_ref[:, off:off + width])

    lr_hq = proj(OFF_LR, LANE + KEY_WIDTH)
    low_rank = lr_hq[:, :LANE].astype(BF16)
    hq_ref[...] = lr_hq[:, LANE:]
    zg_ref[...] = _dot(low_rank, w_up_ref[...]) + b_gate_ref[...]
    zf_ref[...] = proj(OFF_HF, KEY_WIDTH)

    lbw = lb_ref[...]
    lb_e = jnp.exp(lbw - jnp.max(lbw, axis=0, keepdims=True))
    lb_p = lb_e / jnp.sum(lb_e, axis=0, keepdims=True)
    lb = jnp.sum(lb_p[:layer + 1], axis=0, keepdims=True) - lb_p[0:1]
    log_lb = jnp.log(jnp.maximum(lb, LB_FLOOR))
    log_1m_lb = jnp.log1p(-lb)
    one_m_lb = 1.0 - lb

    def decay(j):
        rows = slice(j * DECAY_ROWS, (j + 1) * DECAY_ROWS)
        z = zg_ref[rows, :]
        ls = jnp.minimum(z, 0.0) - jnp.log(1.0 + jnp.exp(-jnp.abs(z)))
        _store_split(ls * (1.0 / GLA_GATE_TAU), ghi_ref, glo_ref, rows)
        z = zf_ref[rows, :]
        e = jnp.exp(-jnp.abs(z))
        u = 1.0 + e
        c = log_1m_lb + (jnp.minimum(z, 0.0) - jnp.log(u))
        log_f = jnp.maximum(log_lb, c) + jnp.log(1.0 + jnp.exp(-jnp.abs(log_lb - c)))
        _store_split(log_f, hhi_ref, hlo_ref, rows)
        hk_ref[rows, :] = one_m_lb * (jnp.where(z >= 0.0, e, 1.0) / u)

    def gq_piece():
        gq_ref[...] = proj_a(OFF_GQ, KEY_WIDTH) * (HEAD_K ** -0.5)

    def gk_piece():
        gk_ref[...] = proj_a(OFF_GK, KEY_WIDTH)

    def gv_piece(c0):
        def emit():
            gv_ref[:, c0:c0 + KEY_WIDTH] = proj_a(OFF_GV + c0, KEY_WIDTH).astype(BF16)
        return emit

    def hv_piece():
        hv_ref[...] = proj(OFF_HI, HGRN_VAL_WIDTH).astype(BF16)

    matmul_pieces = [gq_piece, gk_piece, gv_piece(0), gv_piece(KEY_WIDTH), hv_piece]
    n_decay = SEQ_TILE // DECAY_ROWS
    for i, piece in enumerate(matmul_pieces):
        piece()
        for j in range(i * n_decay // len(matmul_pieces), (i + 1) * n_decay // len(matmul_pieces)):
            decay(j)

    gla_ok = hgrn_ok = None
    for s in range(n_pass):
        rows = slice(s * MIX_TILE, (s + 1) * MIX_TILE)
        gb_ref[rows, :] = (_dot(tri_tile_ref[...], ghi_ref[rows, :])
                           + _dot(tri_tile_ref[...], glo_ref[rows, :]))
        hb_ref[rows, :] = (_dot(tri_chunk_ref[...], hhi_ref[rows, :])
                           + _dot(tri_chunk_ref[...], hlo_ref[rows, :]))
        g_ok = _fast_path_ok(gb_ref.at[rows], GLA_CHUNKS)
        h_ok = _fast_path_ok(hb_ref.at[rows], HGRN_CHUNKS)
        gla_ok = g_ok if gla_ok is None else jnp.logical_and(gla_ok, g_ok)
        hgrn_ok = h_ok if hgrn_ok is None else jnp.logical_and(hgrn_ok, h_ok)

    xoff = GLA_VAL_WIDTH + HGRN_VAL_WIDTH
    gate_block = 512

    def gate_piece(c0):
        def emit():
            g = proj(OFF_GATE + c0, gate_block)
            gs_ref[:, c0:c0 + gate_block] = g * jax.nn.sigmoid(g)
        return emit

    def xq_piece():
        xq_ref[...] = proj(OFF_XQ, XATTN_WIDTH).astype(BF16)

    def xattn_piece(s, hd):
        def emit():
            rows = slice(s * MIX_TILE, (s + 1) * MIX_TILE)
            lanes = slice(hd * HEAD_K, (hd + 1) * HEAD_K)
            sc = _dot_nt(xq_ref[rows, lanes], mk_ref[:, lanes]) * (HEAD_K ** -0.5)
            p = jnp.exp(sc - jnp.max(sc, axis=-1, keepdims=True))
            o = _dot(p.astype(BF16), mv_ref[:, lanes]) / jnp.sum(p, axis=-1, keepdims=True)
            mixed_ref[rows, xoff + hd * HEAD_K:xoff + (hd + 1) * HEAD_K] = _rms(
                o, xattn_norm_ref[...])
        return emit

    def out_piece(c0):
        def emit():
            cols = slice(c0, c0 + gate_block)
            gated = (mixed_ref[:, cols] * gs_ref[:, cols]).astype(BF16)
            part = _dot(gated, w_out_ref[cols, :])
            out_ref[0] = (x_ref[0] if c0 == 0 else out_ref[0]) + part
        return emit

    def epilogue():
        if final_norm:
            out_ref[0] = _rms(out_ref[0], final_norm_ref[...])

    gates = [gate_piece(c0) for c0 in range(0, D_MIX, gate_block)]
    heads = [[xattn_piece(s, hd) for hd in range(N_HEADS)] for s in range(n_pass)]
    outs = [out_piece(c0) for c0 in range(0, D_MIX, gate_block)]
    groups = iter(_piece_order(gates, heads, outs, xq_piece, epilogue))

    def fill():
        for piece in next(groups, []):
            piece()

    gla_prev_ref[...] = gla_st_ref[...]
    hgrn_prev_ref[...] = hgrn_st_ref[...]
    for s in range(n_pass):
        rows = slice(s * MIX_TILE, (s + 1) * MIX_TILE)
        _mixer_fast(gq_ref.at[rows], gk_ref.at[rows], gv_ref.at[rows], gb_ref.at[rows],
                    gla_st_ref, gla_norm_ref[...], mixed_ref.at[rows],
                    0, GLA_HEAD_V, GLA_CHUNKS, gla_operands, fill)
        _mixer_fast(hq_ref.at[rows], hk_ref.at[rows], hv_ref.at[rows], hb_ref.at[rows],
                    hgrn_st_ref, hgrn_norm_ref[...], mixed_ref.at[rows],
                    GLA_VAL_WIDTH, HGRN_HEAD_V, HGRN_CHUNKS, hgrn_operands, fill)
    for group in groups:
        for piece in group:
            piece()

    @pl.when(jnp.logical_not(jnp.logical_and(gla_ok, hgrn_ok)))
    def _redo_with_pairwise_path():
        @pl.when(jnp.logical_not(gla_ok))
        def _gla_pairwise():
            gla_st_ref[...] = gla_prev_ref[...]
            _mixer_safe(gq_ref, gk_ref, gv_ref, gb_ref, gla_st_ref, gla_norm_ref[...],
                        mixed_ref, 0, GLA_HEAD_V, MIX_TILE // GLA_CHUNKS)

        @pl.when(jnp.logical_not(hgrn_ok))
        def _hgrn_pairwise():
            hgrn_st_ref[...] = hgrn_prev_ref[...]
            _mixer_safe(hq_ref, hk_ref, hv_ref, hb_ref, hgrn_st_ref, hgrn_norm_ref[...],
                        mixed_ref, GLA_VAL_WIDTH, HGRN_HEAD_V, MIX_TILE // HGRN_CHUNKS)

        for piece in outs:
            piece()
        epilogue()


def _layer_call(x, mem, params, layer, final_norm):
    batch, seq, _ = x.shape
    n_tiles = seq // SEQ_TILE

    def resident(arr, stacked):
        if stacked:
            shape = (None,) + arr.shape[1:]
            index = (layer,) + (0,) * (arr.ndim - 1)
        else:
            shape = arr.shape
            index = (0,) * arr.ndim
        return pl.BlockSpec(shape, lambda b, t: index, pipeline_mode=pl.Buffered(1))

    in_specs = [
        pl.BlockSpec((1, SEQ_TILE, D_MODEL), lambda b, t: (b, t, 0)),
        pl.BlockSpec((1, MEM_LEN, D_MODEL), lambda b, t: (b, 0, 0), pipeline_mode=pl.Buffered(1)),
    ] + [resident(arr, stacked) for arr, stacked in params]
    params = [arr for arr, _ in params]
    key_f32 = pltpu.VMEM((SEQ_TILE, KEY_WIDTH), F32)
    key_bf16 = pltpu.VMEM((SEQ_TILE, KEY_WIDTH), BF16)
    operand = pltpu.VMEM((MIX_TILE, KEY_WIDTH), BF16)
    half_operand = pltpu.VMEM((MIX_TILE // 2, KEY_WIDTH), BF16)
    gla_state = pltpu.VMEM((N_HEADS, HEAD_K, GLA_HEAD_V), F32)
    hgrn_state = pltpu.VMEM((N_HEADS, HEAD_K, HGRN_HEAD_V), F32)
    scratch = [
        pltpu.VMEM((SEQ_TILE, D_MODEL), BF16),
        key_f32, key_f32, key_f32,
        key_f32, key_f32, key_f32,
        pltpu.VMEM((SEQ_TILE, GLA_VAL_WIDTH), BF16),
        pltpu.VMEM((SEQ_TILE, HGRN_VAL_WIDTH), BF16),
        key_bf16, key_bf16, key_bf16, key_bf16,
        pltpu.VMEM((SEQ_TILE, D_MIX), F32),
        pltpu.VMEM((SEQ_TILE, D_MIX), F32),
        key_bf16,
        pltpu.VMEM((MEM_LEN, XATTN_WIDTH), BF16),
        pltpu.VMEM((MEM_LEN, XATTN_WIDTH), BF16),
        gla_state, hgrn_state,
        gla_state, hgrn_state,
        operand, operand, operand, operand,
        operand, operand, half_operand, half_operand, operand, operand,
        half_operand, half_operand,
        pltpu.VMEM((N_HEADS, MIX_TILE, MIX_TILE), BF16),
    ]
    return pl.pallas_call(
        functools.partial(_layer_kernel, layer=layer, final_norm=final_norm),
        out_shape=jax.ShapeDtypeStruct(x.shape, x.dtype),
        grid=(batch, n_tiles),
        in_specs=in_specs,
        out_specs=pl.BlockSpec((1, SEQ_TILE, D_MODEL), lambda b, t: (b, t, 0)),
        scratch_shapes=scratch,
        compiler_params=pltpu.CompilerParams(
            dimension_semantics=("arbitrary", "arbitrary"),
            vmem_limit_bytes=VMEM_LIMIT_BYTES),
        name=f"hybrid_layer_{layer}",
    )(x, mem, *params)


def _window_kernel(w_ref, a_ref, b_ref):
    a_ref[...] = w_ref[:, :WIN_A_WIDTH].astype(BF16)
    b_ref[...] = w_ref[:, WIN_B_START:].astype(BF16)


def _projection_windows(w_in):
    depth, rows, cols = w_in.shape
    win_b = cols - WIN_B_START
    block = lambda width: pl.BlockSpec((None, WINDOW_ROWS, width), lambda l, i: (l, i, 0))
    return pl.pallas_call(
        _window_kernel,
        out_shape=(jax.ShapeDtypeStruct((depth, rows, WIN_A_WIDTH), BF16),
                   jax.ShapeDtypeStruct((depth, rows, win_b), BF16)),
        grid=(depth, rows // WINDOW_ROWS),
        in_specs=[block(cols)],
        out_specs=(block(WIN_A_WIDTH), block(win_b)),
        compiler_params=pltpu.CompilerParams(dimension_semantics=("arbitrary", "arbitrary")),
        name="projection_windows",
    )(w_in)


def _lower_tri_ones(segment):
    idx = jnp.arange(MIX_TILE)
    keep = (idx[None, :] <= idx[:, None]) & ((idx[None, :] // segment) == (idx[:, None] // segment))
    return keep.astype(BF16)


def kernel(x, mem, norm_w, w_in, gla_w_gate_up, gla_b_gate, gla_norm_w, hgrn_lower_bounds,
           hgrn_norm_w, mem_norm_w, w_mem_kv, xattn_norm_w, w_out, final_norm_w):
    assert x.shape[1] % SEQ_TILE == 0 and SEQ_TILE % MIX_TILE == 0
    rows = lambda v: (v.reshape(DEPTH, 1, -1).astype(F32), True)
    w_a, w_b = _projection_windows(w_in)
    w_up = jnp.pad(gla_w_gate_up, ((0, 0), (LOWRANK_LEAD, 0), (0, 0))).astype(BF16)
    params = (
        (_lower_tri_ones(MIX_TILE // GLA_CHUNKS), False),
        (_lower_tri_ones(MIX_TILE // HGRN_CHUNKS), False),
        rows(norm_w), (w_a, True), (w_b, True), (w_up, True), rows(gla_b_gate),
        rows(gla_norm_w), (hgrn_lower_bounds.astype(F32), False), rows(hgrn_norm_w),
        rows(mem_norm_w), (w_mem_kv.astype(BF16), True), rows(xattn_norm_w),
        (w_out.astype(BF16), True), (final_norm_w.reshape(1, -1).astype(F32), False),
    )
    for layer in range(DEPTH):
        x = _layer_call(x, mem, params, layer, final_norm=(layer == DEPTH - 1))
    return x
```

```python
import functools

import jax
import jax.numpy as jnp
from jax import lax
from jax.experimental import pallas as pl
from jax.experimental.pallas import tpu as pltpu

F32 = jnp.float32
BF16 = jnp.bfloat16

D_MODEL = 1024
DEPTH = 2
MEM_LEN = 256
N_HEADS = 4
HEAD_K = 128
GLA_HEAD_V = 256
HGRN_HEAD_V = 128
KEY_WIDTH = N_HEADS * HEAD_K
GLA_VAL_WIDTH = N_HEADS * GLA_HEAD_V
HGRN_VAL_WIDTH = N_HEADS * HGRN_HEAD_V
XATTN_WIDTH = N_HEADS * HEAD_K
D_MIX = 2048
GLA_LOWRANK = 16
GLA_GATE_TAU = 16.0
LB_FLOOR = 1e-30
NORM_EPS = 1e-6

LANE = 128
LOWRANK_LEAD = LANE - GLA_LOWRANK
WIN_A_WIDTH = 2 * KEY_WIDTH + GLA_VAL_WIDTH
WIN_B_START = WIN_A_WIDTH - LOWRANK_LEAD
OFF_GQ, OFF_GK, OFF_GV = 0, KEY_WIDTH, 2 * KEY_WIDTH
OFF_LR = 0
OFF_HQ = OFF_LR + LANE
OFF_HF = OFF_HQ + KEY_WIDTH
OFF_HI = OFF_HF + KEY_WIDTH
OFF_XQ = OFF_HI + HGRN_VAL_WIDTH
OFF_GATE = OFF_XQ + XATTN_WIDTH

SEQ_TILE = 512
MIX_TILE = 256
GLA_CHUNKS = 1
HGRN_CHUNKS = 4
DECAY_ROWS = 64
SAFE_BLOCK = 16
FAST_PATH_LIMIT = 60.0

WIN_A_STEP, WIN_B_STEP = 256, 384

VMEM_LIMIT_BYTES = 58 * 1024 * 1024


def _dot(a, b):
    return jnp.dot(a, b, preferred_element_type=F32)


def _dot_nt(a, b):
    return lax.dot_general(a, b, (((1,), (1,)), ((), ())), preferred_element_type=F32)


def _dot_tn(a, b):
    return lax.dot_general(a, b, (((0,), (0,)), ((), ())), preferred_element_type=F32)


def _rms(x, w):
    return x * lax.rsqrt(jnp.mean(x * x, axis=-1, keepdims=True) + NORM_EPS) * w


def _store_split(x, hi_ref, lo_ref, rows):
    hi = x.astype(BF16)
    hi_ref[rows, :] = hi
    lo_ref[rows, :] = (x - hi.astype(F32)).astype(BF16)


def _column(row):
    return jnp.broadcast_to(row, (8, row.shape[1])).T[:, 0:1]


def _mixer_fast(q_ref, k_ref, vb_ref, b_ref, st_ref, norm_w, mixed_ref, out_off, dv, n_chunks,
                operand_refs, fill):
    assert n_chunks in (1, 4)
    qt_ref, kt_ref, qo_ref, ko_ref, qi_ref, ks_ref, ql_ref, kl_ref, p_ref = operand_refs
    chunk = MIX_TILE // n_chunks
    half = MIX_TILE // 2
    tot = [b_ref[(c + 1) * chunk - 1:(c + 1) * chunk, :] for c in range(n_chunks)]
    ref = [b_ref[c * chunk + chunk // 2 - 1:c * chunk + chunk // 2, :] for c in range(n_chunks)]
    before = [jnp.zeros_like(tot[0])]
    for c in range(n_chunks):
        before.append(before[-1] + tot[c])
    tile_total = before[n_chunks]
    decay_col = _column(jnp.exp(tile_total))
    if n_chunks == 4:
        e_in = [None] + [jnp.exp(before[c]) for c in range(1, 4)]
        e_out = [jnp.exp(tile_total - before[c + 1]) for c in range(3)] + [None]
        e_tot1 = jnp.exp(tot[1])
        e_tot2 = jnp.exp(tot[2])

    for c in range(n_chunks):
        rows = slice(c * chunk, (c + 1) * chunk)
        for h in range(N_HEADS):
            lanes = slice(h * HEAD_K, (h + 1) * HEAD_K)
            b = b_ref[rows, lanes]
            q = q_ref[rows, lanes]
            k = k_ref[rows, lanes]
            r = ref[c][:, lanes]
            qe = q * jnp.exp(b)
            ke = k * jnp.exp(tot[c][:, lanes] - b)
            qt_ref[rows, lanes] = (q * jnp.exp(b - r)).astype(BF16)
            kt_ref[rows, lanes] = (k * jnp.exp(r - b)).astype(BF16)
            if n_chunks == 1:
                qi_ref[rows, lanes] = qe.astype(BF16)
                ks_ref[rows, lanes] = ke.astype(BF16)
            else:
                pair = slice((c // 2) * chunk, (c // 2 + 1) * chunk)
                if c % 2 == 1:
                    qo_ref[pair, lanes] = qe.astype(BF16)
                else:
                    ko_ref[pair, lanes] = ke.astype(BF16)
                qi = qe if c == 0 else qe * e_in[c][:, lanes]
                ks = ke if c == 3 else ke * e_out[c][:, lanes]
                qi_ref[rows, lanes] = qi.astype(BF16)
                ks_ref[rows, lanes] = ks.astype(BF16)
                if c == 0:
                    kl_ref[0:chunk, lanes] = (ke * e_tot1[:, lanes]).astype(BF16)
                elif c == 1:
                    kl_ref[chunk:half, lanes] = ke.astype(BF16)
                elif c == 2:
                    ql_ref[0:chunk, lanes] = qe.astype(BF16)
                else:
                    ql_ref[chunk:half, lanes] = (qe * e_tot2[:, lanes]).astype(BF16)
    fill()

    if n_chunks == 1:
        row = lax.broadcasted_iota(jnp.int32, (MIX_TILE, MIX_TILE), 0)
        col = lax.broadcasted_iota(jnp.int32, (MIX_TILE, MIX_TILE), 1)
        causal = row >= col
        for h in range(N_HEADS):
            lanes = slice(h * HEAD_K, (h + 1) * HEAD_K)
            vl = slice(h * dv, (h + 1) * dv)
            s = _dot_nt(qt_ref[:, lanes], kt_ref[:, lanes])
            p = jnp.where(causal, s, 0.0).astype(BF16)
            st = st_ref[h]
            o = _dot(p, vb_ref[:, vl]) + _dot(qi_ref[:, lanes], st.astype(BF16))
            st_ref[h] = st * decay_col[lanes, :] + _dot_tn(ks_ref[:, lanes], vb_ref[:, vl])
            mixed_ref[:, out_off + h * dv:out_off + (h + 1) * dv] = _rms(o, norm_w)
            if h % 2 == 1:
                fill()
        return

    row = lax.broadcasted_iota(jnp.int32, (half, half), 0)
    col = lax.broadcasted_iota(jnp.int32, (half, half), 1)
    same_chunk = ((row >= chunk) == (col >= chunk)) & (row >= col)
    for h in range(N_HEADS):
        lanes = slice(h * HEAD_K, (h + 1) * HEAD_K)
        for hf in range(2):
            rows = slice(hf * half, (hf + 1) * half)
            d = _dot_nt(qt_ref[rows, lanes], kt_ref[rows, lanes])
            p_ref[h, rows, rows] = jnp.where(same_chunk, d, 0.0).astype(BF16)
            pair = slice(hf * chunk, (hf + 1) * chunk)
            lo = hf * half
            p_ref[h, lo + chunk:lo + half, lo:lo + chunk] = _dot_nt(
                qo_ref[pair, lanes], ko_ref[pair, lanes]).astype(BF16)
        p_ref[h, half:, 0:half] = _dot_nt(ql_ref[:, lanes], kl_ref[:, lanes]).astype(BF16)
    fill()
    for h in range(N_HEADS):
        lanes = slice(h * HEAD_K, (h + 1) * HEAD_K)
        vl = slice(h * dv, (h + 1) * dv)
        ol = slice(out_off + h * dv, out_off + (h + 1) * dv)
        st = st_ref[h]
        st_b = st.astype(BF16)
        mixed_ref[0:half, ol] = (_dot(p_ref[h, 0:half, 0:half], vb_ref[0:half, vl])
                                 + _dot(qi_ref[0:half, lanes], st_b))
        mixed_ref[half:, ol] = (_dot(p_ref[h, half:, :], vb_ref[:, vl])
                                + _dot(qi_ref[half:, lanes], st_b))
        st_ref[h] = st * decay_col[lanes, :] + _dot_tn(ks_ref[:, lanes], vb_ref[:, vl])
    fill()
    for h in range(N_HEADS):
        ol = slice(out_off + h * dv, out_off + (h + 1) * dv)
        mixed_ref[:, ol] = _rms(mixed_ref[:, ol], norm_w)


def _mixer_safe(q_ref, k_ref, vb_ref, b_ref, st_ref, norm_w, mixed_ref, out_off, dv, segment):
    row16 = lax.broadcasted_iota(jnp.int32, (SAFE_BLOCK, 1), 0)

    def block_body(sb, carry):
        s0 = pl.multiple_of(sb * SAFE_BLOCK, SAFE_BLOCK)
        rows_b = pl.ds(s0, SAFE_BLOCK)
        before = pl.multiple_of(jnp.maximum(s0 - 8, 0), 8)
        prev = b_ref[pl.ds(before, 8), :][7:8, :]
        prev = jnp.where(s0 % segment == 0, 0.0, prev)
        for h in range(N_HEADS):
            lanes = slice(h * HEAD_K, (h + 1) * HEAD_K)
            vl = slice(h * dv, (h + 1) * dv)
            bl = b_ref[rows_b, lanes] - prev[:, lanes]
            q = q_ref[rows_b, lanes]
            k = k_ref[rows_b, lanes]
            vb = vb_ref[rows_b, vl]
            v = vb.astype(F32)
            o = jnp.zeros((SAFE_BLOCK, dv), F32)
            for j in range(SAFE_BLOCK):
                w = jnp.exp(jnp.minimum(bl - bl[j:j + 1, :], 0.0))
                sc = jnp.sum(q * k[j:j + 1, :] * w, axis=-1, keepdims=True)
                o = o + jnp.where(row16 >= j, sc, 0.0) * v[j:j + 1, :]
            st = st_ref[h]
            o = o + _dot((q * jnp.exp(bl)).astype(BF16), st.astype(BF16))
            b_end = bl[SAFE_BLOCK - 1:SAFE_BLOCK, :]
            kl = (k * jnp.exp(b_end - bl)).astype(BF16)
            st_ref[h] = st * _column(jnp.exp(b_end)) + _dot_tn(kl, vb)
            mixed_ref[rows_b, out_off + h * dv:out_off + (h + 1) * dv] = _rms(o, norm_w)
        return carry

    lax.fori_loop(0, SEQ_TILE // SAFE_BLOCK, block_body, 0)


def _fast_path_ok(b_ref, n_chunks):
    chunk = MIX_TILE // n_chunks
    worst = jnp.zeros((1, KEY_WIDTH), F32)
    for c in range(n_chunks):
        r = b_ref[c * chunk + chunk // 2 - 1:c * chunk + chunk // 2, :]
        total = b_ref[(c + 1) * chunk - 1:(c + 1) * chunk, :]
        worst = jnp.maximum(worst, jnp.maximum(-r, r - total))
    return jnp.max(worst) < FAST_PATH_LIMIT


def _piece_order(gates, heads, outs, xq, epilogue):
    h0, h1 = heads
    return [[gates[0]], [gates[1]], [xq],
            [gates[2]], [gates[3]], [h0[0], h0[1]],
            [h0[2]], [h0[3]], [h1[0]],
            [outs[0]], [outs[1]], [h1[1]],
            [h1[2], h1[3], outs[2], outs[3], epilogue]]


def _layer_kernel(x_ref, mem_ref, tri_tile_ref, tri_chunk_ref,
                  norm_w_ref, w_a_ref, w_b_ref, w_up_ref, b_gate_ref,
                  gla_norm_ref, lb_ref, hgrn_norm_ref, mem_norm_ref, w_mem_ref, xattn_norm_ref,
                  w_out_ref, final_norm_ref,
                  out_ref,
                  h_ref, gq_ref, gk_ref, gb_ref, hq_ref, hk_ref, hb_ref,
                  gv_ref, hv_ref, ghi_ref, glo_ref, hhi_ref, hlo_ref,
                  mixed_ref, gs_ref, xq_ref, mk_ref, mv_ref,
                  gla_st_ref, hgrn_st_ref, gla_prev_ref, hgrn_prev_ref,
                  g_qt_ref, g_kt_ref, g_qi_ref, g_ks_ref,
                  h_qt_ref, h_kt_ref, h_qo_ref, h_ko_ref, h_qi_ref, h_ks_ref, h_ql_ref, h_kl_ref,
                  p_ref,
                  *, layer, final_norm):
    t = pl.program_id(1)
    n_pass = SEQ_TILE // MIX_TILE
    gla_operands = (g_qt_ref, g_kt_ref, None, None, g_qi_ref, g_ks_ref, None, None, None)
    hgrn_operands = (h_qt_ref, h_kt_ref, h_qo_ref, h_ko_ref, h_qi_ref, h_ks_ref, h_ql_ref,
                     h_kl_ref, p_ref)
    zg_ref, zf_ref = gb_ref, hk_ref

    @pl.when(t == 0)
    def _start_of_sequence():
        gla_st_ref[...] = jnp.zeros_like(gla_st_ref)
        hgrn_st_ref[...] = jnp.zeros_like(hgrn_st_ref)
        m = _rms(mem_ref[0], mem_norm_ref[...]).astype(BF16)
        mk_ref[...] = _dot(m, w_mem_ref[:, :XATTN_WIDTH]).astype(BF16)
        mv_ref[...] = _dot(m, w_mem_ref[:, XATTN_WIDTH:]).astype(BF16)

    h_ref[...] = _rms(x_ref[0], norm_w_ref[...]).astype(BF16)

    def proj_a(off, width):
        return _dot(h_ref[...], w_a_ref[:, off:off + width])

    def proj(off, width):
        return _dot(h_ref[...], w_b_ref[:, off:off + width])

    lr_hq = proj(OFF_LR, LANE + KEY_WIDTH)
    low_rank = lr_hq[:, :LANE].astype(BF16)
    hq_ref[...] = lr_hq[:, LANE:]
    zg_ref[...] = _dot(low_rank, w_up_ref[...]) + b_gate_ref[...]
    zf_ref[...] = proj(OFF_HF, KEY_WIDTH)

    lbw = lb_ref[...]
    lb_e = jnp.exp(lbw - jnp.max(lbw, axis=0, keepdims=True))
    lb_p = lb_e / jnp.sum(lb_e, axis=0, keepdims=True)
    lb = jnp.sum(lb_p[:layer + 1], axis=0, keepdims=True) - lb_p[0:1]
    log_lb = jnp.log(jnp.maximum(lb, LB_FLOOR))
    log_1m_lb = jnp.log1p(-lb)
    one_m_lb = 1.0 - lb

    def decay(j):
        rows = slice(j * DECAY_ROWS, (j + 1) * DECAY_ROWS)
        z = zg_ref[rows, :]
        ls = jnp.minimum(z, 0.0) - jnp.log(1.0 + jnp.exp(-jnp.abs(z)))
        _store_split(ls * (1.0 / GLA_GATE_TAU), ghi_ref, glo_ref, rows)
        z = zf_ref[rows, :]
        e = jnp.exp(-jnp.abs(z))
        u = 1.0 + e
        c = log_1m_lb + (jnp.minimum(z, 0.0) - jnp.log(u))
        log_f = jnp.maximum(log_lb, c) + jnp.log(1.0 + jnp.exp(-jnp.abs(log_lb - c)))
        _store_split(log_f, hhi_ref, hlo_ref, rows)
        hk_ref[rows, :] = one_m_lb * (jnp.where(z >= 0.0, e, 1.0) / u)

    def gq_piece():
        gq_ref[...] = proj_a(OFF_GQ, KEY_WIDTH) * (HEAD_K ** -0.5)

    def gk_piece():
        gk_ref[...] = proj_a(OFF_GK, KEY_WIDTH)

    def gv_piece(c0):
        def emit():
            gv_ref[:, c0:c0 + KEY_WIDTH] = proj_a(OFF_GV + c0, KEY_WIDTH).astype(BF16)
        return emit

    def hv_piece():
        hv_ref[...] = proj(OFF_HI, HGRN_VAL_WIDTH).astype(BF16)

    matmul_pieces = [gq_piece, gk_piece, gv_piece(0), gv_piece(KEY_WIDTH), hv_piece]
    n_decay = SEQ_TILE // DECAY_ROWS
    for i, piece in enumerate(matmul_pieces):
        piece()
        for j in range(i * n_decay // len(matmul_pieces), (i + 1) * n_decay // len(matmul_pieces)):
            decay(j)

    gla_ok = hgrn_ok = None
    for s in range(n_pass):
        rows = slice(s * MIX_TILE, (s + 1) * MIX_TILE)
        gb_ref[rows, :] = (_dot(tri_tile_ref[...], ghi_ref[rows, :])
                           + _dot(tri_tile_ref[...], glo_ref[rows, :]))
        hb_ref[rows, :] = (_dot(tri_chunk_ref[...], hhi_ref[rows, :])
                           + _dot(tri_chunk_ref[...], hlo_ref[rows, :]))
        g_ok = _fast_path_ok(gb_ref.at[rows], GLA_CHUNKS)
        h_ok = _fast_path_ok(hb_ref.at[rows], HGRN_CHUNKS)
        gla_ok = g_ok if gla_ok is None else jnp.logical_and(gla_ok, g_ok)
        hgrn_ok = h_ok if hgrn_ok is None else jnp.logical_and(hgrn_ok, h_ok)

    xoff = GLA_VAL_WIDTH + HGRN_VAL_WIDTH
    gate_block = 512

    def gate_piece(c0):
        def emit():
            g = proj(OFF_GATE + c0, gate_block)
            gs_ref[:, c0:c0 + gate_block] = g * jax.nn.sigmoid(g)
        return emit

    def xq_piece():
        xq_ref[...] = proj(OFF_XQ, XATTN_WIDTH).astype(BF16)

    def xattn_piece(s, hd):
        def emit():
            rows = slice(s * MIX_TILE, (s + 1) * MIX_TILE)
            lanes = slice(hd * HEAD_K, (hd + 1) * HEAD_K)
            sc = _dot_nt(xq_ref[rows, lanes], mk_ref[:, lanes]) * (HEAD_K ** -0.5)
            p = jnp.exp(sc - jnp.max(sc, axis=-1, keepdims=True))
            o = _dot(p.astype(BF16), mv_ref[:, lanes]) / jnp.sum(p, axis=-1, keepdims=True)
            mixed_ref[rows, xoff + hd * HEAD_K:xoff + (hd + 1) * HEAD_K] = _rms(
                o, xattn_norm_ref[...])
        return emit

    def out_piece(c0):
        def emit():
            cols = slice(c0, c0 + gate_block)
            gated = (mixed_ref[:, cols] * gs_ref[:, cols]).astype(BF16)
            part = _dot(gated, w_out_ref[cols, :])
            out_ref[0] = (x_ref[0] if c0 == 0 else out_ref[0]) + part
        return emit

    def epilogue():
        if final_norm:
            out_ref[0] = _rms(out_ref[0], final_norm_ref[...])

    gates = [gate_piece(c0) for c0 in range(0, D_MIX, gate_block)]
    heads = [[xattn_piece(s, hd) for hd in range(N_HEADS)] for s in range(n_pass)]
    outs = [out_piece(c0) for c0 in range(0, D_MIX, gate_block)]
    groups = iter(_piece_order(gates, heads, outs, xq_piece, epilogue))

    def fill():
        for piece in next(groups, []):
            piece()

    gla_prev_ref[...] = gla_st_ref[...]
    hgrn_prev_ref[...] = hgrn_st_ref[...]
    for s in range(n_pass):
        rows = slice(s * MIX_TILE, (s + 1) * MIX_TILE)
        _mixer_fast(gq_ref.at[rows], gk_ref.at[rows], gv_ref.at[rows], gb_ref.at[rows],
                    gla_st_ref, gla_norm_ref[...], mixed_ref.at[rows],
                    0, GLA_HEAD_V, GLA_CHUNKS, gla_operands, fill)
        _mixer_fast(hq_ref.at[rows], hk_ref.at[rows], hv_ref.at[rows], hb_ref.at[rows],
                    hgrn_st_ref, hgrn_norm_ref[...], mixed_ref.at[rows],
                    GLA_VAL_WIDTH, HGRN_HEAD_V, HGRN_CHUNKS, hgrn_operands, fill)
    for group in groups:
        for piece in group:
            piece()

    @pl.when(jnp.logical_not(jnp.logical_and(gla_ok, hgrn_ok)))
    def _redo_with_pairwise_path():
        @pl.when(jnp.logical_not(gla_ok))
        def _gla_pairwise():
            gla_st_ref[...] = gla_prev_ref[...]
            _mixer_safe(gq_ref, gk_ref, gv_ref, gb_ref, gla_st_ref, gla_norm_ref[...],
                        mixed_ref, 0, GLA_HEAD_V, MIX_TILE // GLA_CHUNKS)

        @pl.when(jnp.logical_not(hgrn_ok))
        def _hgrn_pairwise():
            hgrn_st_ref[...] = hgrn_prev_ref[...]
            _mixer_safe(hq_ref, hk_ref, hv_ref, hb_ref, hgrn_st_ref, hgrn_norm_ref[...],
                        mixed_ref, GLA_VAL_WIDTH, HGRN_HEAD_V, MIX_TILE // HGRN_CHUNKS)

        for piece in outs:
            piece()
        epilogue()


def _layer_call(x, mem, params, layer, final_norm):
    batch, seq, _ = x.shape
    n_tiles = seq // SEQ_TILE

    def resident(arr, stacked):
        if stacked:
            shape = (None,) + arr.shape[1:]
            index = (layer,) + (0,) * (arr.ndim - 1)
        else:
            shape = arr.shape
            index = (0,) * arr.ndim
        return pl.BlockSpec(shape, lambda b, t: index, pipeline_mode=pl.Buffered(1))

    in_specs = [
        pl.BlockSpec((1, SEQ_TILE, D_MODEL), lambda b, t: (b, t, 0)),
        pl.BlockSpec((1, MEM_LEN, D_MODEL), lambda b, t: (b, 0, 0), pipeline_mode=pl.Buffered(1)),
    ] + [resident(arr, stacked) for arr, stacked in params]
    params = [arr for arr, _ in params]
    key_f32 = pltpu.VMEM((SEQ_TILE, KEY_WIDTH), F32)
    key_bf16 = pltpu.VMEM((SEQ_TILE, KEY_WIDTH), BF16)
    operand = pltpu.VMEM((MIX_TILE, KEY_WIDTH), BF16)
    half_operand = pltpu.VMEM((MIX_TILE // 2, KEY_WIDTH), BF16)
    gla_state = pltpu.VMEM((N_HEADS, HEAD_K, GLA_HEAD_V), F32)
    hgrn_state = pltpu.VMEM((N_HEADS, HEAD_K, HGRN_HEAD_V), F32)
    scratch = [
        pltpu.VMEM((SEQ_TILE, D_MODEL), BF16),
        key_f32, key_f32, key_f32,
        key_f32, key_f32, key_f32,
        pltpu.VMEM((SEQ_TILE, GLA_VAL_WIDTH), BF16),
        pltpu.VMEM((SEQ_TILE, HGRN_VAL_WIDTH), BF16),
        key_bf16, key_bf16, key_bf16, key_bf16,
        pltpu.VMEM((SEQ_TILE, D_MIX), F32),
        pltpu.VMEM((SEQ_TILE, D_MIX), F32),
        key_bf16,
        pltpu.VMEM((MEM_LEN, XATTN_WIDTH), BF16),
        pltpu.VMEM((MEM_LEN, XATTN_WIDTH), BF16),
        gla_state, hgrn_state,
        gla_state, hgrn_state,
        operand, operand, operand, operand,
        operand, operand, half_operand, half_operand, operand, operand,
        half_operand, half_operand,
        pltpu.VMEM((N_HEADS, MIX_TILE, MIX_TILE), BF16),
    ]
    return pl.pallas_call(
        functools.partial(_layer_kernel, layer=layer, final_norm=final_norm),
        out_shape=jax.ShapeDtypeStruct(x.shape, x.dtype),
        grid=(batch, n_tiles),
        in_specs=in_specs,
        out_specs=pl.BlockSpec((1, SEQ_TILE, D_MODEL), lambda b, t: (b, t, 0)),
        scratch_shapes=scratch,
        compiler_params=pltpu.CompilerParams(
            dimension_semantics=("arbitrary", "arbitrary"),
            vmem_limit_bytes=VMEM_LIMIT_BYTES),
        name=f"hybrid_layer_{layer}",
    )(x, mem, *params)


def _window_kernel(w_ref, o_ref):
    o_ref[...] = w_ref[0].T.astype(BF16)


def _projection_window(w_t, start, width, step):
    depth, _, k = w_t.shape
    assert width % step == 0 and start % 16 == 0
    return pl.pallas_call(
        _window_kernel,
        out_shape=jax.ShapeDtypeStruct((depth, k, width), BF16),
        grid=(depth, width // step),
        in_specs=[pl.BlockSpec((pl.Element(1), pl.Element(step), pl.Element(k)),
                               lambda l, i: (l, pl.multiple_of(start + i * step, 16), 0))],
        out_specs=pl.BlockSpec((None, k, step), lambda l, i: (l, 0, i)),
        compiler_params=pltpu.CompilerParams(dimension_semantics=("arbitrary", "arbitrary")),
        name=f"projection_window_{start}",
    )(w_t)


def _lower_tri_ones(segment):
    idx = jnp.arange(MIX_TILE)
    keep = (idx[None, :] <= idx[:, None]) & ((idx[None, :] // segment) == (idx[:, None] // segment))
    return keep.astype(BF16)


def kernel(x, mem, norm_w, w_in, gla_w_gate_up, gla_b_gate, gla_norm_w, hgrn_lower_bounds,
           hgrn_norm_w, mem_norm_w, w_mem_kv, xattn_norm_w, w_out, final_norm_w):
    assert x.shape[1] % SEQ_TILE == 0 and SEQ_TILE % MIX_TILE == 0
    rows = lambda v: (v.reshape(DEPTH, 1, -1).astype(F32), True)
    w_t = jnp.swapaxes(w_in, 1, 2)
    w_a = _projection_window(w_t, 0, WIN_A_WIDTH, WIN_A_STEP)
    w_b = _projection_window(w_t, WIN_B_START, w_in.shape[2] - WIN_B_START, WIN_B_STEP)
    w_up = jnp.pad(gla_w_gate_up, ((0, 0), (LOWRANK_LEAD, 0), (0, 0))).astype(BF16)
    params = (
        (_lower_tri_ones(MIX_TILE // GLA_CHUNKS), False),
        (_lower_tri_ones(MIX_TILE // HGRN_CHUNKS), False),
        rows(norm_w), (w_a, True), (w_b, True), (w_up, True), rows(gla_b_gate),
        rows(gla_norm_w), (hgrn_lower_bounds.astype(F32), False), rows(hgrn_norm_w),
        rows(mem_norm_w), (w_mem_kv.astype(BF16), True), rows(xattn_norm_w),
        (w_out.astype(BF16), True), (final_norm_w.reshape(1, -1).astype(F32), False),
    )
    for layer in range(DEPTH):
        x = _layer_call(x, mem, params, layer, final_norm=(layer == DEPTH - 1))
    return x
```

```python
import functools

import jax
import jax.numpy as jnp
from jax import lax
from jax.experimental import pallas as pl
from jax.experimental.pallas import tpu as pltpu

F32 = jnp.float32
BF16 = jnp.bfloat16

D_MODEL = 1024
DEPTH = 2
MEM_LEN = 256
N_HEADS = 4
HEAD_K = 128
GLA_HEAD_V = 256
HGRN_HEAD_V = 128
KEY_WIDTH = N_HEADS * HEAD_K
GLA_VAL_WIDTH = N_HEADS * GLA_HEAD_V
HGRN_VAL_WIDTH = N_HEADS * HGRN_HEAD_V
XATTN_WIDTH = N_HEADS * HEAD_K
D_MIX = 2048
GLA_LOWRANK = 16
GLA_GATE_TAU = 16.0
LB_FLOOR = 1e-30
NORM_EPS = 1e-6

LANE = 128
LOWRANK_LEAD = LANE - GLA_LOWRANK
WIN_A_WIDTH = 2 * KEY_WIDTH + GLA_VAL_WIDTH
WIN_B_START = WIN_A_WIDTH - LOWRANK_LEAD
OFF_GQ, OFF_GK, OFF_GV = 0, KEY_WIDTH, 2 * KEY_WIDTH
OFF_LR = 0
OFF_HQ = OFF_LR + LANE
OFF_HF = OFF_HQ + KEY_WIDTH
OFF_HI = OFF_HF + KEY_WIDTH
OFF_XQ = OFF_HI + HGRN_VAL_WIDTH
OFF_GATE = OFF_XQ + XATTN_WIDTH

SEQ_TILE = 512
MIX_TILE = 256
GLA_CHUNKS = 1
HGRN_CHUNKS = 4
DECAY_ROWS = 64
SAFE_BLOCK = 16
FAST_PATH_LIMIT = 60.0

WIN_A_STEP, WIN_B_STEP = 256, 384

VMEM_LIMIT_BYTES = 58 * 1024 * 1024


def _dot(a, b):
    return jnp.dot(a, b, preferred_element_type=F32)


def _dot_nt(a, b):
    return lax.dot_general(a, b, (((1,), (1,)), ((), ())), preferred_element_type=F32)


def _dot_tn(a, b):
    return lax.dot_general(a, b, (((0,), (0,)), ((), ())), preferred_element_type=F32)


def _rms(x, w):
    return x * lax.rsqrt(jnp.mean(x * x, axis=-1, keepdims=True) + NORM_EPS) * w


def _store_split(x, hi_ref, lo_ref, rows):
    hi = x.astype(BF16)
    hi_ref[rows, :] = hi
    lo_ref[rows, :] = (x - hi.astype(F32)).astype(BF16)


def _column(row):
    return jnp.broadcast_to(row, (8, row.shape[1])).T[:, 0:1]


def _mixer_fast(q_ref, k_ref, vb_ref, b_ref, st_ref, norm_w, mixed_ref, out_off, dv, n_chunks,
                operand_refs, fill):
    assert n_chunks in (1, 4)
    qt_ref, kt_ref, qo_ref, ko_ref, qi_ref, ks_ref, ql_ref, kl_ref, p_ref = operand_refs
    chunk = MIX_TILE // n_chunks
    half = MIX_TILE // 2
    tot = [b_ref[(c + 1) * chunk - 1:(c + 1) * chunk, :] for c in range(n_chunks)]
    ref = [b_ref[c * chunk + chunk // 2 - 1:c * chunk + chunk // 2, :] for c in range(n_chunks)]
    before = [jnp.zeros_like(tot[0])]
    for c in range(n_chunks):
        before.append(before[-1] + tot[c])
    tile_total = before[n_chunks]
    decay_col = _column(jnp.exp(tile_total))
    if n_chunks == 4:
        e_in = [None] + [jnp.exp(before[c]) for c in range(1, 4)]
        e_out = [jnp.exp(tile_total - before[c + 1]) for c in range(3)] + [None]
        e_tot1 = jnp.exp(tot[1])
        e_tot2 = jnp.exp(tot[2])

    for c in range(n_chunks):
        rows = slice(c * chunk, (c + 1) * chunk)
        for h in range(N_HEADS):
            lanes = slice(h * HEAD_K, (h + 1) * HEAD_K)
            b = b_ref[rows, lanes]
            q = q_ref[rows, lanes]
            k = k_ref[rows, lanes]
            r = ref[c][:, lanes]
            qe = q * jnp.exp(b)
            ke = k * jnp.exp(tot[c][:, lanes] - b)
            qt_ref[rows, lanes] = (q * jnp.exp(b - r)).astype(BF16)
            kt_ref[rows, lanes] = (k * jnp.exp(r - b)).astype(BF16)
            if n_chunks == 1:
                qi_ref[rows, lanes] = qe.astype(BF16)
                ks_ref[rows, lanes] = ke.astype(BF16)
            else:
                pair = slice((c // 2) * chunk, (c // 2 + 1) * chunk)
                if c % 2 == 1:
                    qo_ref[pair, lanes] = qe.astype(BF16)
                else:
                    ko_ref[pair, lanes] = ke.astype(BF16)
                qi = qe if c == 0 else qe * e_in[c][:, lanes]
                ks = ke if c == 3 else ke * e_out[c][:, lanes]
                qi_ref[rows, lanes] = qi.astype(BF16)
                ks_ref[rows, lanes] = ks.astype(BF16)
                if c == 0:
                    kl_ref[0:chunk, lanes] = (ke * e_tot1[:, lanes]).astype(BF16)
                elif c == 1:
                    kl_ref[chunk:half, lanes] = ke.astype(BF16)
                elif c == 2:
                    ql_ref[0:chunk, lanes] = qe.astype(BF16)
                else:
                    ql_ref[chunk:half, lanes] = (qe * e_tot2[:, lanes]).astype(BF16)
    fill()

    if n_chunks == 1:
        row = lax.broadcasted_iota(jnp.int32, (MIX_TILE, MIX_TILE), 0)
        col = lax.broadcasted_iota(jnp.int32, (MIX_TILE, MIX_TILE), 1)
        causal = row >= col
        for h in range(N_HEADS):
            lanes = slice(h * HEAD_K, (h + 1) * HEAD_K)
            vl = slice(h * dv, (h + 1) * dv)
            s = _dot_nt(qt_ref[:, lanes], kt_ref[:, lanes])
            p = jnp.where(causal, s, 0.0).astype(BF16)
            st = st_ref[h]
            o = _dot(p, vb_ref[:, vl]) + _dot(qi_ref[:, lanes], st.astype(BF16))
            st_ref[h] = st * decay_col[lanes, :] + _dot_tn(ks_ref[:, lanes], vb_ref[:, vl])
            mixed_ref[:, out_off + h * dv:out_off + (h + 1) * dv] = _rms(o, norm_w)
            if h % 2 == 1:
                fill()
        return

    row = lax.broadcasted_iota(jnp.int32, (half, half), 0)
    col = lax.broadcasted_iota(jnp.int32, (half, half), 1)
    same_chunk = ((row >= chunk) == (col >= chunk)) & (row >= col)
    for h in range(N_HEADS):
        lanes = slice(h * HEAD_K, (h + 1) * HEAD_K)
        for hf in range(2):
            rows = slice(hf * half, (hf + 1) * half)
            d = _dot_nt(qt_ref[rows, lanes], kt_ref[rows, lanes])
            p_ref[h, rows, rows] = jnp.where(same_chunk, d, 0.0).astype(BF16)
            pair = slice(hf * chunk, (hf + 1) * chunk)
            lo = hf * half
            p_ref[h, lo + chunk:lo + half, lo:lo + chunk] = _dot_nt(
                qo_ref[pair, lanes], ko_ref[pair, lanes]).astype(BF16)
        p_ref[h, half:, 0:half] = _dot_nt(ql_ref[:, lanes], kl_ref[:, lanes]).astype(BF16)
    fill()
    for h in range(N_HEADS):
        lanes = slice(h * HEAD_K, (h + 1) * HEAD_K)
        vl = slice(h * dv, (h + 1) * dv)
        ol = slice(out_off + h * dv, out_off + (h + 1) * dv)
        st = st_ref[h]
        st_b = st.astype(BF16)
        mixed_ref[0:half, ol] = (_dot(p_ref[h, 0:half, 0:half], vb_ref[0:half, vl])
                                 + _dot(qi_ref[0:half, lanes], st_b))
        mixed_ref[half:, ol] = (_dot(p_ref[h, half:, :], vb_ref[:, vl])
                                + _dot(qi_ref[half:, lanes], st_b))
        st_ref[h] = st * decay_col[lanes, :] + _dot_tn(ks_ref[:, lanes], vb_ref[:, vl])
    fill()
    for h in range(N_HEADS):
        ol = slice(out_off + h * dv, out_off + (h + 1) * dv)
        mixed_ref[:, ol] = _rms(mixed_ref[:, ol], norm_w)


def _mixer_safe(q_ref, k_ref, vb_ref, b_ref, st_ref, norm_w, mixed_ref, out_off, dv, segment):
    row16 = lax.broadcasted_iota(jnp.int32, (SAFE_BLOCK, 1), 0)

    def block_body(sb, carry):
        s0 = pl.multiple_of(sb * SAFE_BLOCK, SAFE_BLOCK)
        rows_b = pl.ds(s0, SAFE_BLOCK)
        before = pl.multiple_of(jnp.maximum(s0 - 8, 0), 8)
        prev = b_ref[pl.ds(before, 8), :][7:8, :]
        prev = jnp.where(s0 % segment == 0, 0.0, prev)
        for h in range(N_HEADS):
            lanes = slice(h * HEAD_K, (h + 1) * HEAD_K)
            vl = slice(h * dv, (h + 1) * dv)
            bl = b_ref[rows_b, lanes] - prev[:, lanes]
            q = q_ref[rows_b, lanes]
            k = k_ref[rows_b, lanes]
            vb = vb_ref[rows_b, vl]
            v = vb.astype(F32)
            o = jnp.zeros((SAFE_BLOCK, dv), F32)
            for j in range(SAFE_BLOCK):
                w = jnp.exp(jnp.minimum(bl - bl[j:j + 1, :], 0.0))
                sc = jnp.sum(q * k[j:j + 1, :] * w, axis=-1, keepdims=True)
                o = o + jnp.where(row16 >= j, sc, 0.0) * v[j:j + 1, :]
            st = st_ref[h]
            o = o + _dot((q * jnp.exp(bl)).astype(BF16), st.astype(BF16))
            b_end = bl[SAFE_BLOCK - 1:SAFE_BLOCK, :]
            kl = (k * jnp.exp(b_end - bl)).astype(BF16)
            st_ref[h] = st * _column(jnp.exp(b_end)) + _dot_tn(kl, vb)
            mixed_ref[rows_b, out_off + h * dv:out_off + (h + 1) * dv] = _rms(o, norm_w)
        return carry

    lax.fori_loop(0, SEQ_TILE // SAFE_BLOCK, block_body, 0)


def _fast_path_ok(b_ref, n_chunks):
    chunk = MIX_TILE // n_chunks
    worst = jnp.zeros((1, KEY_WIDTH), F32)
    for c in range(n_chunks):
        r = b_ref[c * chunk + chunk // 2 - 1:c * chunk + chunk // 2, :]
        total = b_ref[(c + 1) * chunk - 1:(c + 1) * chunk, :]
        worst = jnp.maximum(worst, jnp.maximum(-r, r - total))
    return jnp.max(worst) < FAST_PATH_LIMIT


def _piece_order(gates, heads, outs, xq, epilogue):
    h0, h1 = heads
    return [[gates[0]], [gates[1]], [xq],
            [gates[2]], [gates[3]], [h0[0], h0[1]],
            [outs[0]], [h0[2], h0[3]], [outs[1]],
            [h1[0], h1[1]], [h1[2]], [h1[3]],
            [outs[2], outs[3], epilogue]]


def _layer_kernel(x_ref, mem_ref, tri_tile_ref, tri_chunk_ref,
                  norm_w_ref, w_a_ref, w_b_ref, w_up_ref, b_gate_ref,
                  gla_norm_ref, lb_ref, hgrn_norm_ref, mem_norm_ref, w_mem_ref, xattn_norm_ref,
                  w_out_ref, final_norm_ref,
                  out_ref,
                  h_ref, gq_ref, gk_ref, gb_ref, hq_ref, hk_ref, hb_ref,
                  gv_ref, hv_ref, ghi_ref, glo_ref, hhi_ref, hlo_ref,
                  mixed_ref, gs_ref, xq_ref, mk_ref, mv_ref,
                  gla_st_ref, hgrn_st_ref, gla_prev_ref, hgrn_prev_ref,
                  g_qt_ref, g_kt_ref, g_qi_ref, g_ks_ref,
                  h_qt_ref, h_kt_ref, h_qo_ref, h_ko_ref, h_qi_ref, h_ks_ref, h_ql_ref, h_kl_ref,
                  p_ref,
                  *, layer, final_norm):
    t = pl.program_id(1)
    n_pass = SEQ_TILE // MIX_TILE
    gla_operands = (g_qt_ref, g_kt_ref, None, None, g_qi_ref, g_ks_ref, None, None, None)
    hgrn_operands = (h_qt_ref, h_kt_ref, h_qo_ref, h_ko_ref, h_qi_ref, h_ks_ref, h_ql_ref,
                     h_kl_ref, p_ref)
    zg_ref, zf_ref = gb_ref, hk_ref

    @pl.when(t == 0)
    def _start_of_sequence():
        gla_st_ref[...] = jnp.zeros_like(gla_st_ref)
        hgrn_st_ref[...] = jnp.zeros_like(hgrn_st_ref)
        m = _rms(mem_ref[0], mem_norm_ref[...]).astype(BF16)
        mk_ref[...] = _dot(m, w_mem_ref[:, :XATTN_WIDTH]).astype(BF16)
        mv_ref[...] = _dot(m, w_mem_ref[:, XATTN_WIDTH:]).astype(BF16)

    h_ref[...] = _rms(x_ref[0], norm_w_ref[...]).astype(BF16)

    def proj_a(off, width):
        return _dot(h_ref[...], w_a_ref[:, off:off + width])

    def proj(off, width):
        return _dot(h_ref[...], w_b_ref[:, off:off + width])

    lr_hq = proj(OFF_LR, LANE + KEY_WIDTH)
    low_rank = lr_hq[:, :LANE].astype(BF16)
    hq_ref[...] = lr_hq[:, LANE:]
    zg_ref[...] = _dot(low_rank, w_up_ref[...]) + b_gate_ref[...]
    zf_ref[...] = proj(OFF_HF, KEY_WIDTH)

    lbw = lb_ref[...]
    lb_e = jnp.exp(lbw - jnp.max(lbw, axis=0, keepdims=True))
    lb_p = lb_e / jnp.sum(lb_e, axis=0, keepdims=True)
    lb = jnp.sum(lb_p[:layer + 1], axis=0, keepdims=True) - lb_p[0:1]
    log_lb = jnp.log(jnp.maximum(lb, LB_FLOOR))
    log_1m_lb = jnp.log1p(-lb)
    one_m_lb = 1.0 - lb

    def decay(j):
        rows = slice(j * DECAY_ROWS, (j + 1) * DECAY_ROWS)
        z = zg_ref[rows, :]
        ls = jnp.minimum(z, 0.0) - jnp.log(1.0 + jnp.exp(-jnp.abs(z)))
        _store_split(ls * (1.0 / GLA_GATE_TAU), ghi_ref, glo_ref, rows)
        z = zf_ref[rows, :]
        e = jnp.exp(-jnp.abs(z))
        u = 1.0 + e
        c = log_1m_lb + (jnp.minimum(z, 0.0) - jnp.log(u))
        log_f = jnp.maximum(log_lb, c) + jnp.log(1.0 + jnp.exp(-jnp.abs(log_lb - c)))
        _store_split(log_f, hhi_ref, hlo_ref, rows)
        hk_ref[rows, :] = one_m_lb * (jnp.where(z >= 0.0, e, 1.0) / u)

    def gq_piece():
        gq_ref[...] = proj_a(OFF_GQ, KEY_WIDTH) * (HEAD_K ** -0.5)

    def gk_piece():
        gk_ref[...] = proj_a(OFF_GK, KEY_WIDTH)

    def gv_piece(c0):
        def emit():
            gv_ref[:, c0:c0 + KEY_WIDTH] = proj_a(OFF_GV + c0, KEY_WIDTH).astype(BF16)
        return emit

    def hv_piece():
        hv_ref[...] = proj(OFF_HI, HGRN_VAL_WIDTH).astype(BF16)

    matmul_pieces = [gq_piece, gk_piece, gv_piece(0), gv_piece(KEY_WIDTH), hv_piece]
    n_decay = SEQ_TILE // DECAY_ROWS
    for i, piece in enumerate(matmul_pieces):
        piece()
        for j in range(i * n_decay // len(matmul_pieces), (i + 1) * n_decay // len(matmul_pieces)):
            decay(j)

    gla_ok = hgrn_ok = None
    for s in range(n_pass):
        rows = slice(s * MIX_TILE, (s + 1) * MIX_TILE)
        gb_ref[rows, :] = (_dot(tri_tile_ref[...], ghi_ref[rows, :])
                           + _dot(tri_tile_ref[...], glo_ref[rows, :]))
        hb_ref[rows, :] = (_dot(tri_chunk_ref[...], hhi_ref[rows, :])
                           + _dot(tri_chunk_ref[...], hlo_ref[rows, :]))
        g_ok = _fast_path_ok(gb_ref.at[rows], GLA_CHUNKS)
        h_ok = _fast_path_ok(hb_ref.at[rows], HGRN_CHUNKS)
        gla_ok = g_ok if gla_ok is None else jnp.logical_and(gla_ok, g_ok)
        hgrn_ok = h_ok if hgrn_ok is None else jnp.logical_and(hgrn_ok, h_ok)

    xoff = GLA_VAL_WIDTH + HGRN_VAL_WIDTH
    gate_block = 512

    def gate_piece(c0):
        def emit():
            g = proj(OFF_GATE + c0, gate_block)
            gs_ref[:, c0:c0 + gate_block] = g * jax.nn.sigmoid(g)
        return emit

    def xq_piece():
        xq_ref[...] = proj(OFF_XQ, XATTN_WIDTH).astype(BF16)

    def xattn_piece(s, hd):
        def emit():
            rows = slice(s * MIX_TILE, (s + 1) * MIX_TILE)
            lanes = slice(hd * HEAD_K, (hd + 1) * HEAD_K)
            sc = _dot_nt(xq_ref[rows, lanes], mk_ref[:, lanes]) * (HEAD_K ** -0.5)
            p = jnp.exp(sc - jnp.max(sc, axis=-1, keepdims=True))
            o = _dot(p.astype(BF16), mv_ref[:, lanes]) / jnp.sum(p, axis=-1, keepdims=True)
            mixed_ref[rows, xoff + hd * HEAD_K:xoff + (hd + 1) * HEAD_K] = _rms(
                o, xattn_norm_ref[...])
        return emit

    def out_piece(c0):
        def emit():
            cols = slice(c0, c0 + gate_block)
            gated = (mixed_ref[:, cols] * gs_ref[:, cols]).astype(BF16)
            part = _dot(gated, w_out_ref[cols, :])
            out_ref[0] = (x_ref[0] if c0 == 0 else out_ref[0]) + part
        return emit

    def epilogue():
        if final_norm:
            out_ref[0] = _rms(out_ref[0], final_norm_ref[...])

    gates = [gate_piece(c0) for c0 in range(0, D_MIX, gate_block)]
    heads = [[xattn_piece(s, hd) for hd in range(N_HEADS)] for s in range(n_pass)]
    outs = [out_piece(c0) for c0 in range(0, D_MIX, gate_block)]
    groups = iter(_piece_order(gates, heads, outs, xq_piece, epilogue))

    def fill():
        for piece in next(groups, []):
            piece()

    gla_prev_ref[...] = gla_st_ref[...]
    hgrn_prev_ref[...] = hgrn_st_ref[...]
    for s in range(n_pass):
        rows = slice(s * MIX_TILE, (s + 1) * MIX_TILE)
        _mixer_fast(gq_ref.at[rows], gk_ref.at[rows], gv_ref.at[rows], gb_ref.at[rows],
                    gla_st_ref, gla_norm_ref[...], mixed_ref.at[rows],
                    0, GLA_HEAD_V, GLA_CHUNKS, gla_operands, fill)
    for s in range(n_pass):
        rows = slice(s * MIX_TILE, (s + 1) * MIX_TILE)
        _mixer_fast(hq_ref.at[rows], hk_ref.at[rows], hv_ref.at[rows], hb_ref.at[rows],
                    hgrn_st_ref, hgrn_norm_ref[...], mixed_ref.at[rows],
                    GLA_VAL_WIDTH, HGRN_HEAD_V, HGRN_CHUNKS, hgrn_operands, fill)
    for group in groups:
        for piece in group:
            piece()

    @pl.when(jnp.logical_not(jnp.logical_and(gla_ok, hgrn_ok)))
    def _redo_with_pairwise_path():
        @pl.when(jnp.logical_not(gla_ok))
        def _gla_pairwise():
            gla_st_ref[...] = gla_prev_ref[...]
            _mixer_safe(gq_ref, gk_ref, gv_ref, gb_ref, gla_st_ref, gla_norm_ref[...],
                        mixed_ref, 0, GLA_HEAD_V, MIX_TILE // GLA_CHUNKS)

        @pl.when(jnp.logical_not(hgrn_ok))
        def _hgrn_pairwise():
            hgrn_st_ref[...] = hgrn_prev_ref[...]
            _mixer_safe(hq_ref, hk_ref, hv_ref, hb_ref, hgrn_st_ref, hgrn_norm_ref[...],
                        mixed_ref, GLA_VAL_WIDTH, HGRN_HEAD_V, MIX_TILE // HGRN_CHUNKS)

        for piece in outs:
            piece()
        epilogue()


def _layer_call(x, mem, params, layer, final_norm):
    batch, seq, _ = x.shape
    n_tiles = seq // SEQ_TILE

    def resident(arr, stacked):
        if stacked:
            shape = (None,) + arr.shape[1:]
            index = (layer,) + (0,) * (arr.ndim - 1)
        else:
            shape = arr.shape
            index = (0,) * arr.ndim
        return pl.BlockSpec(shape, lambda b, t: index, pipeline_mode=pl.Buffered(1))

    in_specs = [
        pl.BlockSpec((1, SEQ_TILE, D_MODEL), lambda b, t: (b, t, 0)),
        pl.BlockSpec((1, MEM_LEN, D_MODEL), lambda b, t: (b, 0, 0), pipeline_mode=pl.Buffered(1)),
    ] + [resident(arr, stacked) for arr, stacked in params]
    params = [arr for arr, _ in params]
    key_f32 = pltpu.VMEM((SEQ_TILE, KEY_WIDTH), F32)
    key_bf16 = pltpu.VMEM((SEQ_TILE, KEY_WIDTH), BF16)
    operand = pltpu.VMEM((MIX_TILE, KEY_WIDTH), BF16)
    half_operand = pltpu.VMEM((MIX_TILE // 2, KEY_WIDTH), BF16)
    gla_state = pltpu.VMEM((N_HEADS, HEAD_K, GLA_HEAD_V), F32)
    hgrn_state = pltpu.VMEM((N_HEADS, HEAD_K, HGRN_HEAD_V), F32)
    scratch = [
        pltpu.VMEM((SEQ_TILE, D_MODEL), BF16),
        key_f32, key_f32, key_f32,
        key_f32, key_f32, key_f32,
        pltpu.VMEM((SEQ_TILE, GLA_VAL_WIDTH), BF16),
        pltpu.VMEM((SEQ_TILE, HGRN_VAL_WIDTH), BF16),
        key_bf16, key_bf16, key_bf16, key_bf16,
        pltpu.VMEM((SEQ_TILE, D_MIX), F32),
        pltpu.VMEM((SEQ_TILE, D_MIX), F32),
        key_bf16,
        pltpu.VMEM((MEM_LEN, XATTN_WIDTH), BF16),
        pltpu.VMEM((MEM_LEN, XATTN_WIDTH), BF16),
        gla_state, hgrn_state,
        gla_state, hgrn_state,
        operand, operand, operand, operand,
        operand, operand, half_operand, half_operand, operand, operand,
        half_operand, half_operand,
        pltpu.VMEM((N_HEADS, MIX_TILE, MIX_TILE), BF16),
    ]
    return pl.pallas_call(
        functools.partial(_layer_kernel, layer=layer, final_norm=final_norm),
        out_shape=jax.ShapeDtypeStruct(x.shape, x.dtype),
        grid=(batch, n_tiles),
        in_specs=in_specs,
        out_specs=pl.BlockSpec((1, SEQ_TILE, D_MODEL), lambda b, t: (b, t, 0)),
        scratch_shapes=scratch,
        compiler_params=pltpu.CompilerParams(
            dimension_semantics=("arbitrary", "arbitrary"),
            vmem_limit_bytes=VMEM_LIMIT_BYTES),
        name=f"hybrid_layer_{layer}",
    )(x, mem, *params)


def _window_kernel(w_ref, o_ref):
    o_ref[...] = w_ref[0].astype(BF16).T


def _projection_window(w_t, start, width, step):
    depth, _, k = w_t.shape
    assert width % step == 0 and start % 16 == 0
    return pl.pallas_call(
        _window_kernel,
        out_shape=jax.ShapeDtypeStruct((depth, k, width), BF16),
        grid=(depth, width // step),
        in_specs=[pl.BlockSpec((pl.Element(1), pl.Element(step), pl.Element(k)),
                               lambda l, i: (l, pl.multiple_of(start + i * step, 16), 0))],
        out_specs=pl.BlockSpec((None, k, step), lambda l, i: (l, 0, i)),
        compiler_params=pltpu.CompilerParams(dimension_semantics=("arbitrary", "arbitrary")),
        name=f"projection_window_{start}",
    )(w_t)


def _lower_tri_ones(segment):
    idx = jnp.arange(MIX_TILE)
    keep = (idx[None, :] <= idx[:, None]) & ((idx[None, :] // segment) == (idx[:, None] // segment))
    return keep.astype(BF16)


def kernel(x, mem, norm_w, w_in, gla_w_gate_up, gla_b_gate, gla_norm_w, hgrn_lower_bounds,
           hgrn_norm_w, mem_norm_w, w_mem_kv, xattn_norm_w, w_out, final_norm_w):
    assert x.shape[1] % SEQ_TILE == 0 and SEQ_TILE % MIX_TILE == 0
    rows = lambda v: (v.reshape(DEPTH, 1, -1).astype(F32), True)
    w_t = jnp.swapaxes(w_in, 1, 2)
    w_a = _projection_window(w_t, 0, WIN_A_WIDTH, WIN_A_STEP)
    w_b = _projection_window(w_t, WIN_B_START, w_in.shape[2] - WIN_B_START, WIN_B_STEP)
    w_up = jnp.pad(gla_w_gate_up, ((0, 0), (LOWRANK_LEAD, 0), (0, 0))).astype(BF16)
    params = (
        (_lower_tri_ones(MIX_TILE // GLA_CHUNKS), False),
        (_lower_tri_ones(MIX_TILE // HGRN_CHUNKS), False),
        rows(norm_w), (w_a, True), (w_b, True), (w_up, True), rows(gla_b_gate),
        rows(gla_norm_w), (hgrn_lower_bounds.astype(F32), False), rows(hgrn_norm_w),
        rows(mem_norm_w), (w_mem_kv.astype(BF16), True), rows(xattn_norm_w),
        (w_out.astype(BF16), True), (final_norm_w.reshape(1, -1).astype(F32), False),
    )
    for layer in range(DEPTH):
        x = _layer_call(x, mem, params, layer, final_norm=(layer == DEPTH - 1))
    return x
```

```python
import functools

import jax
import jax.numpy as jnp
from jax import lax
from jax.experimental import pallas as pl
from jax.experimental.pallas import tpu as pltpu

F32 = jnp.float32
BF16 = jnp.bfloat16

D_MODEL = 1024
DEPTH = 2
MEM_LEN = 256
N_HEADS = 4
HEAD_K = 128
GLA_HEAD_V = 256
HGRN_HEAD_V = 128
KEY_WIDTH = N_HEADS * HEAD_K
GLA_VAL_WIDTH = N_HEADS * GLA_HEAD_V
HGRN_VAL_WIDTH = N_HEADS * HGRN_HEAD_V
XATTN_WIDTH = N_HEADS * HEAD_K
D_MIX = 2048
GLA_LOWRANK = 16
GLA_GATE_TAU = 16.0
LB_FLOOR = 1e-30
NORM_EPS = 1e-6

LANE = 128
LOWRANK_LEAD = LANE - GLA_LOWRANK
WIN_A_WIDTH = 2 * KEY_WIDTH + GLA_VAL_WIDTH
WIN_B_START = WIN_A_WIDTH - LOWRANK_LEAD
OFF_GQ, OFF_GK, OFF_GV = 0, KEY_WIDTH, 2 * KEY_WIDTH
OFF_LR = 0
OFF_HQ = OFF_LR + LANE
OFF_HF = OFF_HQ + KEY_WIDTH
OFF_HI = OFF_HF + KEY_WIDTH
OFF_XQ = OFF_HI + HGRN_VAL_WIDTH
OFF_GATE = OFF_XQ + XATTN_WIDTH

SEQ_TILE = 512
MIX_TILE = 256
GLA_CHUNKS = 1
HGRN_CHUNKS = 4
DECAY_ROWS = 64
SAFE_BLOCK = 16
FAST_PATH_LIMIT = 60.0

WIN_A_STEP, WIN_B_STEP = 1024, 1408

VMEM_LIMIT_BYTES = 58 * 1024 * 1024


def _dot(a, b):
    return jnp.dot(a, b, preferred_element_type=F32)


def _dot_nt(a, b):
    return lax.dot_general(a, b, (((1,), (1,)), ((), ())), preferred_element_type=F32)


def _dot_tn(a, b):
    return lax.dot_general(a, b, (((0,), (0,)), ((), ())), preferred_element_type=F32)


def _rms(x, w):
    return x * lax.rsqrt(jnp.mean(x * x, axis=-1, keepdims=True) + NORM_EPS) * w


def _store_split(x, hi_ref, lo_ref, rows):
    hi = x.astype(BF16)
    hi_ref[rows, :] = hi
    lo_ref[rows, :] = (x - hi.astype(F32)).astype(BF16)


def _column(row):
    return jnp.broadcast_to(row, (8, row.shape[1])).T[:, 0:1]


def _mixer_fast(q_ref, k_ref, vb_ref, b_ref, st_ref, norm_w, mixed_ref, out_off, dv, n_chunks,
                operand_refs):
    assert n_chunks in (1, 4)
    qt_ref, kt_ref, qo_ref, ko_ref, qi_ref, ks_ref, ql_ref, kl_ref, p_ref = operand_refs
    chunk = MIX_TILE // n_chunks
    half = MIX_TILE // 2
    tot = [b_ref[(c + 1) * chunk - 1:(c + 1) * chunk, :] for c in range(n_chunks)]
    ref = [b_ref[c * chunk + chunk // 2 - 1:c * chunk + chunk // 2, :] for c in range(n_chunks)]
    before = [jnp.zeros_like(tot[0])]
    for c in range(n_chunks):
        before.append(before[-1] + tot[c])
    tile_total = before[n_chunks]
    decay_col = _column(jnp.exp(tile_total))
    if n_chunks == 4:
        e_in = [None] + [jnp.exp(before[c]) for c in range(1, 4)]
        e_out = [jnp.exp(tile_total - before[c + 1]) for c in range(3)] + [None]
        e_tot1 = jnp.exp(tot[1])
        e_tot2 = jnp.exp(tot[2])

    for c in range(n_chunks):
        rows = slice(c * chunk, (c + 1) * chunk)
        for h in range(N_HEADS):
            lanes = slice(h * HEAD_K, (h + 1) * HEAD_K)
            b = b_ref[rows, lanes]
            q = q_ref[rows, lanes]
            k = k_ref[rows, lanes]
            r = ref[c][:, lanes]
            qe = q * jnp.exp(b)
            ke = k * jnp.exp(tot[c][:, lanes] - b)
            qt_ref[rows, lanes] = (q * jnp.exp(b - r)).astype(BF16)
            kt_ref[rows, lanes] = (k * jnp.exp(r - b)).astype(BF16)
            if n_chunks == 1:
                qi_ref[rows, lanes] = qe.astype(BF16)
                ks_ref[rows, lanes] = ke.astype(BF16)
            else:
                pair = slice((c // 2) * chunk, (c // 2 + 1) * chunk)
                if c % 2 == 1:
                    qo_ref[pair, lanes] = qe.astype(BF16)
                else:
                    ko_ref[pair, lanes] = ke.astype(BF16)
                qi = qe if c == 0 else qe * e_in[c][:, lanes]
                ks = ke if c == 3 else ke * e_out[c][:, lanes]
                qi_ref[rows, lanes] = qi.astype(BF16)
                ks_ref[rows, lanes] = ks.astype(BF16)
                if c == 0:
                    kl_ref[0:chunk, lanes] = (ke * e_tot1[:, lanes]).astype(BF16)
                elif c == 1:
                    kl_ref[chunk:half, lanes] = ke.astype(BF16)
                elif c == 2:
                    ql_ref[0:chunk, lanes] = qe.astype(BF16)
                else:
                    ql_ref[chunk:half, lanes] = (qe * e_tot2[:, lanes]).astype(BF16)
    yield

    if n_chunks == 1:
        row = lax.broadcasted_iota(jnp.int32, (MIX_TILE, MIX_TILE), 0)
        col = lax.broadcasted_iota(jnp.int32, (MIX_TILE, MIX_TILE), 1)
        causal = row >= col
        for h in range(N_HEADS):
            lanes = slice(h * HEAD_K, (h + 1) * HEAD_K)
            vl = slice(h * dv, (h + 1) * dv)
            s = _dot_nt(qt_ref[:, lanes], kt_ref[:, lanes])
            p = jnp.where(causal, s, 0.0).astype(BF16)
            st = st_ref[h]
            o = _dot(p, vb_ref[:, vl]) + _dot(qi_ref[:, lanes], st.astype(BF16))
            st_ref[h] = st * decay_col[lanes, :] + _dot_tn(ks_ref[:, lanes], vb_ref[:, vl])
            mixed_ref[:, out_off + h * dv:out_off + (h + 1) * dv] = _rms(o, norm_w)
            if h % 2 == 1:
                yield
        return

    row = lax.broadcasted_iota(jnp.int32, (half, half), 0)
    col = lax.broadcasted_iota(jnp.int32, (half, half), 1)
    same_chunk = ((row >= chunk) == (col >= chunk)) & (row >= col)
    for h in range(N_HEADS):
        lanes = slice(h * HEAD_K, (h + 1) * HEAD_K)
        for hf in range(2):
            rows = slice(hf * half, (hf + 1) * half)
            d = _dot_nt(qt_ref[rows, lanes], kt_ref[rows, lanes])
            p_ref[h, rows, rows] = jnp.where(same_chunk, d, 0.0).astype(BF16)
            pair = slice(hf * chunk, (hf + 1) * chunk)
            lo = hf * half
            p_ref[h, lo + chunk:lo + half, lo:lo + chunk] = _dot_nt(
                qo_ref[pair, lanes], ko_ref[pair, lanes]).astype(BF16)
        p_ref[h, half:, 0:half] = _dot_nt(ql_ref[:, lanes], kl_ref[:, lanes]).astype(BF16)
    yield
    for h in range(N_HEADS):
        lanes = slice(h * HEAD_K, (h + 1) * HEAD_K)
        vl = slice(h * dv, (h + 1) * dv)
        ol = slice(out_off + h * dv, out_off + (h + 1) * dv)
        st = st_ref[h]
        st_b = st.astype(BF16)
        mixed_ref[0:half, ol] = (_dot(p_ref[h, 0:half, 0:half], vb_ref[0:half, vl])
                                 + _dot(qi_ref[0:half, lanes], st_b))
        mixed_ref[half:, ol] = (_dot(p_ref[h, half:, :], vb_ref[:, vl])
                                + _dot(qi_ref[half:, lanes], st_b))
        st_ref[h] = st * decay_col[lanes, :] + _dot_tn(ks_ref[:, lanes], vb_ref[:, vl])
    yield
    for h in range(N_HEADS):
        ol = slice(out_off + h * dv, out_off + (h + 1) * dv)
        mixed_ref[:, ol] = _rms(mixed_ref[:, ol], norm_w)


def _mixer_safe(q_ref, k_ref, vb_ref, b_ref, st_ref, norm_w, mixed_ref, out_off, dv, segment):
    row16 = lax.broadcasted_iota(jnp.int32, (SAFE_BLOCK, 1), 0)

    def block_body(sb, carry):
        s0 = pl.multiple_of(sb * SAFE_BLOCK, SAFE_BLOCK)
        rows_b = pl.ds(s0, SAFE_BLOCK)
        before = pl.multiple_of(jnp.maximum(s0 - 8, 0), 8)
        prev = b_ref[pl.ds(before, 8), :][7:8, :]
        prev = jnp.where(s0 % segment == 0, 0.0, prev)
        for h in range(N_HEADS):
            lanes = slice(h * HEAD_K, (h + 1) * HEAD_K)
            vl = slice(h * dv, (h + 1) * dv)
            bl = b_ref[rows_b, lanes] - prev[:, lanes]
            q = q_ref[rows_b, lanes]
            k = k_ref[rows_b, lanes]
            vb = vb_ref[rows_b, vl]
            v = vb.astype(F32)
            o = jnp.zeros((SAFE_BLOCK, dv), F32)
            for j in range(SAFE_BLOCK):
                w = jnp.exp(jnp.minimum(bl - bl[j:j + 1, :], 0.0))
                sc = jnp.sum(q * k[j:j + 1, :] * w, axis=-1, keepdims=True)
                o = o + jnp.where(row16 >= j, sc, 0.0) * v[j:j + 1, :]
            st = st_ref[h]
            o = o + _dot((q * jnp.exp(bl)).astype(BF16), st.astype(BF16))
            b_end = bl[SAFE_BLOCK - 1:SAFE_BLOCK, :]
            kl = (k * jnp.exp(b_end - bl)).astype(BF16)
            st_ref[h] = st * _column(jnp.exp(b_end)) + _dot_tn(kl, vb)
            mixed_ref[rows_b, out_off + h * dv:out_off + (h + 1) * dv] = _rms(o, norm_w)
        return carry

    lax.fori_loop(0, SEQ_TILE // SAFE_BLOCK, block_body, 0)


def _fast_path_ok(b_ref, n_chunks):
    chunk = MIX_TILE // n_chunks
    worst = jnp.zeros((1, KEY_WIDTH), F32)
    for c in range(n_chunks):
        r = b_ref[c * chunk + chunk // 2 - 1:c * chunk + chunk // 2, :]
        total = b_ref[(c + 1) * chunk - 1:(c + 1) * chunk, :]
        worst = jnp.maximum(worst, jnp.maximum(-r, r - total))
    return jnp.max(worst) < FAST_PATH_LIMIT


def _piece_order(gates, heads, outs, xq, epilogue):
    h0, h1 = heads
    return [[gates[0]], [gates[1]], [xq],
            [gates[2]], [gates[3]], [h0[0], h0[1]],
            [outs[0]], [h0[2], h0[3]], [outs[1]],
            [h1[0], h1[1]], [h1[2]], [h1[3]],
            [outs[2], outs[3], epilogue]]


def _layer_kernel(x_ref, mem_ref, tri_tile_ref, tri_chunk_ref,
                  norm_w_ref, w_a_ref, w_b_ref, w_up_ref, b_gate_ref,
                  gla_norm_ref, lb_ref, hgrn_norm_ref, mem_norm_ref, w_mem_ref, xattn_norm_ref,
                  w_out_ref, final_norm_ref,
                  out_ref,
                  h_ref, gq_ref, gk_ref, gb_ref, hq_ref, hk_ref, hb_ref,
                  gv_ref, hv_ref, ghi_ref, glo_ref, hhi_ref, hlo_ref,
                  mixed_ref, gs_ref, xq_ref, mk_ref, mv_ref,
                  gla_st_ref, hgrn_st_ref, gla_prev_ref, hgrn_prev_ref,
                  g_qt_ref, g_kt_ref, g_qi_ref, g_ks_ref,
                  h_qt_ref, h_kt_ref, h_qo_ref, h_ko_ref, h_qi_ref, h_ks_ref, h_ql_ref, h_kl_ref,
                  p_ref,
                  *, layer, final_norm):
    t = pl.program_id(1)
    n_pass = SEQ_TILE // MIX_TILE
    gla_operands = (g_qt_ref, g_kt_ref, None, None, g_qi_ref, g_ks_ref, None, None, None)
    hgrn_operands = (h_qt_ref, h_kt_ref, h_qo_ref, h_ko_ref, h_qi_ref, h_ks_ref, h_ql_ref,
                     h_kl_ref, p_ref)
    zg_ref, zf_ref = gb_ref, hk_ref

    @pl.when(t == 0)
    def _start_of_sequence():
        gla_st_ref[...] = jnp.zeros_like(gla_st_ref)
        hgrn_st_ref[...] = jnp.zeros_like(hgrn_st_ref)
        m = _rms(mem_ref[0], mem_norm_ref[...]).astype(BF16)
        mk_ref[...] = _dot(m, w_mem_ref[:, :XATTN_WIDTH]).astype(BF16)
        mv_ref[...] = _dot(m, w_mem_ref[:, XATTN_WIDTH:]).astype(BF16)

    h_ref[...] = _rms(x_ref[0], norm_w_ref[...]).astype(BF16)

    def proj_a(off, width):
        return _dot(h_ref[...], w_a_ref[:, off:off + width])

    def proj(off, width):
        return _dot(h_ref[...], w_b_ref[:, off:off + width])

    lr_hq = proj(OFF_LR, LANE + KEY_WIDTH)
    low_rank = lr_hq[:, :LANE].astype(BF16)
    hq_ref[...] = lr_hq[:, LANE:]
    zg_ref[...] = _dot(low_rank, w_up_ref[...]) + b_gate_ref[...]
    zf_ref[...] = proj(OFF_HF, KEY_WIDTH)

    lbw = lb_ref[...]
    lb_e = jnp.exp(lbw - jnp.max(lbw, axis=0, keepdims=True))
    lb_p = lb_e / jnp.sum(lb_e, axis=0, keepdims=True)
    lb = jnp.sum(lb_p[:layer + 1], axis=0, keepdims=True) - lb_p[0:1]
    log_lb = jnp.log(jnp.maximum(lb, LB_FLOOR))
    log_1m_lb = jnp.log1p(-lb)
    one_m_lb = 1.0 - lb

    def decay(j):
        rows = slice(j * DECAY_ROWS, (j + 1) * DECAY_ROWS)
        z = zg_ref[rows, :]
        ls = jnp.minimum(z, 0.0) - jnp.log(1.0 + jnp.exp(-jnp.abs(z)))
        _store_split(ls * (1.0 / GLA_GATE_TAU), ghi_ref, glo_ref, rows)
        z = zf_ref[rows, :]
        e = jnp.exp(-jnp.abs(z))
        u = 1.0 + e
        c = log_1m_lb + (jnp.minimum(z, 0.0) - jnp.log(u))
        log_f = jnp.maximum(log_lb, c) + jnp.log(1.0 + jnp.exp(-jnp.abs(log_lb - c)))
        _store_split(log_f, hhi_ref, hlo_ref, rows)
        hk_ref[rows, :] = one_m_lb * (jnp.where(z >= 0.0, e, 1.0) / u)

    def gq_piece():
        gq_ref[...] = proj_a(OFF_GQ, KEY_WIDTH) * (HEAD_K ** -0.5)

    def gk_piece():
        gk_ref[...] = proj_a(OFF_GK, KEY_WIDTH)

    def gv_piece(c0):
        def emit():
            gv_ref[:, c0:c0 + KEY_WIDTH] = proj_a(OFF_GV + c0, KEY_WIDTH).astype(BF16)
        return emit

    def hv_piece():
        hv_ref[...] = proj(OFF_HI, HGRN_VAL_WIDTH).astype(BF16)

    matmul_pieces = [gq_piece, gk_piece, gv_piece(0), gv_piece(KEY_WIDTH), hv_piece]
    n_decay = SEQ_TILE // DECAY_ROWS
    for i, piece in enumerate(matmul_pieces):
        piece()
        for j in range(i * n_decay // len(matmul_pieces), (i + 1) * n_decay // len(matmul_pieces)):
            decay(j)

    gla_ok = hgrn_ok = None
    for s in range(n_pass):
        rows = slice(s * MIX_TILE, (s + 1) * MIX_TILE)
        gb_ref[rows, :] = (_dot(tri_tile_ref[...], ghi_ref[rows, :])
                           + _dot(tri_tile_ref[...], glo_ref[rows, :]))
        hb_ref[rows, :] = (_dot(tri_chunk_ref[...], hhi_ref[rows, :])
                           + _dot(tri_chunk_ref[...], hlo_ref[rows, :]))
        g_ok = _fast_path_ok(gb_ref.at[rows], GLA_CHUNKS)
        h_ok = _fast_path_ok(hb_ref.at[rows], HGRN_CHUNKS)
        gla_ok = g_ok if gla_ok is None else jnp.logical_and(gla_ok, g_ok)
        hgrn_ok = h_ok if hgrn_ok is None else jnp.logical_and(hgrn_ok, h_ok)

    xoff = GLA_VAL_WIDTH + HGRN_VAL_WIDTH
    gate_block = 512

    def gate_piece(c0):
        def emit():
            g = proj(OFF_GATE + c0, gate_block)
            gs_ref[:, c0:c0 + gate_block] = g * jax.nn.sigmoid(g)
        return emit

    def xq_piece():
        xq_ref[...] = proj(OFF_XQ, XATTN_WIDTH).astype(BF16)

    def xattn_piece(s, hd):
        def emit():
            rows = slice(s * MIX_TILE, (s + 1) * MIX_TILE)
            lanes = slice(hd * HEAD_K, (hd + 1) * HEAD_K)
            sc = _dot_nt(xq_ref[rows, lanes], mk_ref[:, lanes]) * (HEAD_K ** -0.5)
            p = jnp.exp(sc - jnp.max(sc, axis=-1, keepdims=True))
            o = _dot(p.astype(BF16), mv_ref[:, lanes]) / jnp.sum(p, axis=-1, keepdims=True)
            mixed_ref[rows, xoff + hd * HEAD_K:xoff + (hd + 1) * HEAD_K] = _rms(
                o, xattn_norm_ref[...])
        return emit

    def out_piece(c0):
        def emit():
            cols = slice(c0, c0 + gate_block)
            gated = (mixed_ref[:, cols] * gs_ref[:, cols]).astype(BF16)
            part = _dot(gated, w_out_ref[cols, :])
            out_ref[0] = (x_ref[0] if c0 == 0 else out_ref[0]) + part
        return emit

    def epilogue():
        if final_norm:
            out_ref[0] = _rms(out_ref[0], final_norm_ref[...])

    gates = [gate_piece(c0) for c0 in range(0, D_MIX, gate_block)]
    heads = [[xattn_piece(s, hd) for hd in range(N_HEADS)] for s in range(n_pass)]
    outs = [out_piece(c0) for c0 in range(0, D_MIX, gate_block)]
    groups = iter(_piece_order(gates, heads, outs, xq_piece, epilogue))

    def fill():
        for piece in next(groups, []):
            piece()

    gla_prev_ref[...] = gla_st_ref[...]
    hgrn_prev_ref[...] = hgrn_st_ref[...]
    pass_rows = [slice(s * MIX_TILE, (s + 1) * MIX_TILE) for s in range(n_pass)]
    for rows in pass_rows:
        for _ in _mixer_fast(gq_ref.at[rows], gk_ref.at[rows], gv_ref.at[rows], gb_ref.at[rows],
                             gla_st_ref, gla_norm_ref[...], mixed_ref.at[rows],
                             0, GLA_HEAD_V, GLA_CHUNKS, gla_operands):
            fill()
    for s, rows in enumerate(pass_rows):
        for _ in _mixer_fast(hq_ref.at[rows], hk_ref.at[rows], hv_ref.at[rows], hb_ref.at[rows],
                             hgrn_st_ref, hgrn_norm_ref[...], mixed_ref.at[rows],
                             GLA_VAL_WIDTH, HGRN_HEAD_V, HGRN_CHUNKS,
                             tuple(r.at[s] for r in hgrn_operands)):
            fill()
    for group in groups:
        for piece in group:
            piece()

    @pl.when(jnp.logical_not(jnp.logical_and(gla_ok, hgrn_ok)))
    def _redo_with_pairwise_path():
        @pl.when(jnp.logical_not(gla_ok))
        def _gla_pairwise():
            gla_st_ref[...] = gla_prev_ref[...]
            _mixer_safe(gq_ref, gk_ref, gv_ref, gb_ref, gla_st_ref, gla_norm_ref[...],
                        mixed_ref, 0, GLA_HEAD_V, MIX_TILE // GLA_CHUNKS)

        @pl.when(jnp.logical_not(hgrn_ok))
        def _hgrn_pairwise():
            hgrn_st_ref[...] = hgrn_prev_ref[...]
            _mixer_safe(hq_ref, hk_ref, hv_ref, hb_ref, hgrn_st_ref, hgrn_norm_ref[...],
                        mixed_ref, GLA_VAL_WIDTH, HGRN_HEAD_V, MIX_TILE // HGRN_CHUNKS)

        for piece in outs:
            piece()
        epilogue()


def _layer_call(x, mem, params, layer, final_norm):
    batch, seq, _ = x.shape
    n_tiles = seq // SEQ_TILE

    def resident(arr, stacked):
        if stacked:
            shape = (None,) + arr.shape[1:]
            index = (layer,) + (0,) * (arr.ndim - 1)
        else:
            shape = arr.shape
            index = (0,) * arr.ndim
        return pl.BlockSpec(shape, lambda b, t: index, pipeline_mode=pl.Buffered(1))

    in_specs = [
        pl.BlockSpec((1, SEQ_TILE, D_MODEL), lambda b, t: (b, t, 0)),
        pl.BlockSpec((1, MEM_LEN, D_MODEL), lambda b, t: (b, 0, 0), pipeline_mode=pl.Buffered(1)),
    ] + [resident(arr, stacked) for arr, stacked in params]
    params = [arr for arr, _ in params]
    key_f32 = pltpu.VMEM((SEQ_TILE, KEY_WIDTH), F32)
    key_bf16 = pltpu.VMEM((SEQ_TILE, KEY_WIDTH), BF16)
    operand = pltpu.VMEM((MIX_TILE, KEY_WIDTH), BF16)
    n_pass = SEQ_TILE // MIX_TILE
    per_pass = pltpu.VMEM((n_pass, MIX_TILE, KEY_WIDTH), BF16)
    per_pass_half = pltpu.VMEM((n_pass, MIX_TILE // 2, KEY_WIDTH), BF16)
    gla_state = pltpu.VMEM((N_HEADS, HEAD_K, GLA_HEAD_V), F32)
    hgrn_state = pltpu.VMEM((N_HEADS, HEAD_K, HGRN_HEAD_V), F32)
    scratch = [
        pltpu.VMEM((SEQ_TILE, D_MODEL), BF16),
        key_f32, key_f32, key_f32,
        key_f32, key_f32, key_f32,
        pltpu.VMEM((SEQ_TILE, GLA_VAL_WIDTH), BF16),
        pltpu.VMEM((SEQ_TILE, HGRN_VAL_WIDTH), BF16),
        key_bf16, key_bf16, key_bf16, key_bf16,
        pltpu.VMEM((SEQ_TILE, D_MIX), F32),
        pltpu.VMEM((SEQ_TILE, D_MIX), F32),
        key_bf16,
        pltpu.VMEM((MEM_LEN, XATTN_WIDTH), BF16),
        pltpu.VMEM((MEM_LEN, XATTN_WIDTH), BF16),
        gla_state, hgrn_state,
        gla_state, hgrn_state,
        operand, operand, operand, operand,
        per_pass, per_pass, per_pass_half, per_pass_half, per_pass, per_pass,
        per_pass_half, per_pass_half,
        pltpu.VMEM((n_pass, N_HEADS, MIX_TILE, MIX_TILE), BF16),
    ]
    return pl.pallas_call(
        functools.partial(_layer_kernel, layer=layer, final_norm=final_norm),
        out_shape=jax.ShapeDtypeStruct(x.shape, x.dtype),
        grid=(batch, n_tiles),
        in_specs=in_specs,
        out_specs=pl.BlockSpec((1, SEQ_TILE, D_MODEL), lambda b, t: (b, t, 0)),
        scratch_shapes=scratch,
        compiler_params=pltpu.CompilerParams(
            dimension_semantics=("arbitrary", "arbitrary"),
            vmem_limit_bytes=VMEM_LIMIT_BYTES),
        name=f"hybrid_layer_{layer}",
    )(x, mem, *params)


def _window_kernel(w_ref, o_ref):
    o_ref[...] = w_ref[0].astype(BF16).T


def _projection_window(w_t, start, width, step):
    depth, _, k = w_t.shape
    assert width % step == 0 and start % 16 == 0
    return pl.pallas_call(
        _window_kernel,
        out_shape=jax.ShapeDtypeStruct((depth, k, width), BF16),
        grid=(depth, width // step),
        in_specs=[pl.BlockSpec((pl.Element(1), pl.Element(step), pl.Element(k)),
                               lambda l, i: (l, pl.multiple_of(start + i * step, 16), 0))],
        out_specs=pl.BlockSpec((None, k, step), lambda l, i: (l, 0, i)),
        compiler_params=pltpu.CompilerParams(dimension_semantics=("arbitrary", "arbitrary")),
        name=f"projection_window_{start}",
    )(w_t)


def _lower_tri_ones(segment):
    idx = jnp.arange(MIX_TILE)
    keep = (idx[None, :] <= idx[:, None]) & ((idx[None, :] // segment) == (idx[:, None] // segment))
    return keep.astype(BF16)


def kernel(x, mem, norm_w, w_in, gla_w_gate_up, gla_b_gate, gla_norm_w, hgrn_lower_bounds,
           hgrn_norm_w, mem_norm_w, w_mem_kv, xattn_norm_w, w_out, final_norm_w):
    assert x.shape[1] % SEQ_TILE == 0 and SEQ_TILE % MIX_TILE == 0
    rows = lambda v: (v.reshape(DEPTH, 1, -1).astype(F32), True)
    w_t = jnp.swapaxes(w_in, 1, 2)
    w_a = _projection_window(w_t, 0, WIN_A_WIDTH, WIN_A_STEP)
    w_b = _projection_window(w_t, WIN_B_START, w_in.shape[2] - WIN_B_START, WIN_B_STEP)
    w_up = jnp.pad(gla_w_gate_up, ((0, 0), (LOWRANK_LEAD, 0), (0, 0))).astype(BF16)
    params = (
        (_lower_tri_ones(MIX_TILE // GLA_CHUNKS), False),
        (_lower_tri_ones(MIX_TILE // HGRN_CHUNKS), False),
        rows(norm_w), (w_a, True), (w_b, True), (w_up, True), rows(gla_b_gate),
        rows(gla_norm_w), (hgrn_lower_bounds.astype(F32), False), rows(hgrn_norm_w),
        rows(mem_norm_w), (w_mem_kv.astype(BF16), True), rows(xattn_norm_w),
        (w_out.astype(BF16), True), (final_norm_w.reshape(1, -1).astype(F32), False),
    )
    for layer in range(DEPTH):
        x = _layer_call(x, mem, params, layer, final_norm=(layer == DEPTH - 1))
    return x
```

```python
import functools

import jax
import jax.numpy as jnp
from jax import lax
from jax.experimental import pallas as pl
from jax.experimental.pallas import tpu as pltpu

F32 = jnp.float32
BF16 = jnp.bfloat16

D_MODEL = 1024
DEPTH = 2
MEM_LEN = 256
N_HEADS = 4
HEAD_K = 128
GLA_HEAD_V = 256
HGRN_HEAD_V = 128
KEY_WIDTH = N_HEADS * HEAD_K
GLA_VAL_WIDTH = N_HEADS * GLA_HEAD_V
HGRN_VAL_WIDTH = N_HEADS * HGRN_HEAD_V
XATTN_WIDTH = N_HEADS * HEAD_K
D_MIX = 2048
GLA_LOWRANK = 16
GLA_GATE_TAU = 16.0
LB_FLOOR = 1e-30
NORM_EPS = 1e-6

LANE = 128
LOWRANK_LEAD = LANE - GLA_LOWRANK
WIN_A_WIDTH = 2 * KEY_WIDTH + GLA_VAL_WIDTH
WIN_B_START = WIN_A_WIDTH - LOWRANK_LEAD
OFF_GQ, OFF_GK, OFF_GV = 0, KEY_WIDTH, 2 * KEY_WIDTH
OFF_LR = 0
OFF_HQ = OFF_LR + LANE
OFF_HF = OFF_HQ + KEY_WIDTH
OFF_HI = OFF_HF + KEY_WIDTH
OFF_XQ = OFF_HI + HGRN_VAL_WIDTH
OFF_GATE = OFF_XQ + XATTN_WIDTH

SEQ_TILE = 512
MIX_TILE = 256
GLA_CHUNKS = 1
HGRN_CHUNKS = 4
DECAY_ROWS = 64
SAFE_BLOCK = 16
FAST_PATH_LIMIT = 60.0

WIN_A_STEP, WIN_B_STEP = 1024, 1408

VMEM_LIMIT_BYTES = 58 * 1024 * 1024


def _dot(a, b):
    return jnp.dot(a, b, preferred_element_type=F32)


def _dot_nt(a, b):
    return lax.dot_general(a, b, (((1,), (1,)), ((), ())), preferred_element_type=F32)


def _dot_tn(a, b):
    return lax.dot_general(a, b, (((0,), (0,)), ((), ())), preferred_element_type=F32)


def _rms(x, w):
    return x * lax.rsqrt(jnp.mean(x * x, axis=-1, keepdims=True) + NORM_EPS) * w


def _store_split(x, hi_ref, lo_ref, rows):
    hi = x.astype(BF16)
    hi_ref[rows, :] = hi
    lo_ref[rows, :] = (x - hi.astype(F32)).astype(BF16)


def _column(row):
    return jnp.broadcast_to(row, (8, row.shape[1])).T[:, 0:1]


def _mixer_fast(q_ref, k_ref, vb_ref, b_ref, st_ref, norm_w, mixed_ref, out_off, dv, n_chunks,
                operand_refs):
    assert n_chunks in (1, 4)
    qt_ref, kt_ref, qo_ref, ko_ref, qi_ref, ks_ref, ql_ref, kl_ref, p_ref = operand_refs
    chunk = MIX_TILE // n_chunks
    half = MIX_TILE // 2
    tot = [b_ref[(c + 1) * chunk - 1:(c + 1) * chunk, :] for c in range(n_chunks)]
    ref = [b_ref[c * chunk + chunk // 2 - 1:c * chunk + chunk // 2, :] for c in range(n_chunks)]
    before = [jnp.zeros_like(tot[0])]
    for c in range(n_chunks):
        before.append(before[-1] + tot[c])
    tile_total = before[n_chunks]
    decay_col = _column(jnp.exp(tile_total))
    if n_chunks == 4:
        e_in = [None] + [jnp.exp(before[c]) for c in range(1, 4)]
        e_out = [jnp.exp(tile_total - before[c + 1]) for c in range(3)] + [None]
        e_tot1 = jnp.exp(tot[1])
        e_tot2 = jnp.exp(tot[2])

    for c in range(n_chunks):
        rows = slice(c * chunk, (c + 1) * chunk)
        for h in range(N_HEADS):
            lanes = slice(h * HEAD_K, (h + 1) * HEAD_K)
            b = b_ref[rows, lanes]
            q = q_ref[rows, lanes]
            k = k_ref[rows, lanes]
            r = ref[c][:, lanes]
            qe = q * jnp.exp(b)
            ke = k * jnp.exp(tot[c][:, lanes] - b)
            qt_ref[rows, lanes] = (q * jnp.exp(b - r)).astype(BF16)
            kt_ref[rows, lanes] = (k * jnp.exp(r - b)).astype(BF16)
            if n_chunks == 1:
                qi_ref[rows, lanes] = qe.astype(BF16)
                ks_ref[rows, lanes] = ke.astype(BF16)
            else:
                pair = slice((c // 2) * chunk, (c // 2 + 1) * chunk)
                if c % 2 == 1:
                    qo_ref[pair, lanes] = qe.astype(BF16)
                else:
                    ko_ref[pair, lanes] = ke.astype(BF16)
                qi = qe if c == 0 else qe * e_in[c][:, lanes]
                ks = ke if c == 3 else ke * e_out[c][:, lanes]
                qi_ref[rows, lanes] = qi.astype(BF16)
                ks_ref[rows, lanes] = ks.astype(BF16)
                if c == 0:
                    kl_ref[0:chunk, lanes] = (ke * e_tot1[:, lanes]).astype(BF16)
                elif c == 1:
                    kl_ref[chunk:half, lanes] = ke.astype(BF16)
                elif c == 2:
                    ql_ref[0:chunk, lanes] = qe.astype(BF16)
                else:
                    ql_ref[chunk:half, lanes] = (qe * e_tot2[:, lanes]).astype(BF16)
    yield

    if n_chunks == 1:
        row = lax.broadcasted_iota(jnp.int32, (MIX_TILE, MIX_TILE), 0)
        col = lax.broadcasted_iota(jnp.int32, (MIX_TILE, MIX_TILE), 1)
        causal = row >= col
        for h in range(N_HEADS):
            lanes = slice(h * HEAD_K, (h + 1) * HEAD_K)
            vl = slice(h * dv, (h + 1) * dv)
            s = _dot_nt(qt_ref[:, lanes], kt_ref[:, lanes])
            p = jnp.where(causal, s, 0.0).astype(BF16)
            st = st_ref[h]
            o = _dot(p, vb_ref[:, vl]) + _dot(qi_ref[:, lanes], st.astype(BF16))
            st_ref[h] = st * decay_col[lanes, :] + _dot_tn(ks_ref[:, lanes], vb_ref[:, vl])
            mixed_ref[:, out_off + h * dv:out_off + (h + 1) * dv] = _rms(o, norm_w)
            if h % 2 == 1:
                yield
        return

    row = lax.broadcasted_iota(jnp.int32, (half, half), 0)
    col = lax.broadcasted_iota(jnp.int32, (half, half), 1)
    same_chunk = ((row >= chunk) == (col >= chunk)) & (row >= col)
    for h in range(N_HEADS):
        lanes = slice(h * HEAD_K, (h + 1) * HEAD_K)
        for hf in range(2):
            rows = slice(hf * half, (hf + 1) * half)
            d = _dot_nt(qt_ref[rows, lanes], kt_ref[rows, lanes])
            p_ref[h, rows, rows] = jnp.where(same_chunk, d, 0.0).astype(BF16)
            pair = slice(hf * chunk, (hf + 1) * chunk)
            lo = hf * half
            p_ref[h, lo + chunk:lo + half, lo:lo + chunk] = _dot_nt(
                qo_ref[pair, lanes], ko_ref[pair, lanes]).astype(BF16)
        p_ref[h, half:, 0:half] = _dot_nt(ql_ref[:, lanes], kl_ref[:, lanes]).astype(BF16)
    yield
    for h in range(N_HEADS):
        lanes = slice(h * HEAD_K, (h + 1) * HEAD_K)
        vl = slice(h * dv, (h + 1) * dv)
        ol = slice(out_off + h * dv, out_off + (h + 1) * dv)
        st = st_ref[h]
        st_b = st.astype(BF16)
        mixed_ref[0:half, ol] = (_dot(p_ref[h, 0:half, 0:half], vb_ref[0:half, vl])
                                 + _dot(qi_ref[0:half, lanes], st_b))
        mixed_ref[half:, ol] = (_dot(p_ref[h, half:, :], vb_ref[:, vl])
                                + _dot(qi_ref[half:, lanes], st_b))
        st_ref[h] = st * decay_col[lanes, :] + _dot_tn(ks_ref[:, lanes], vb_ref[:, vl])
    yield
    for h in range(N_HEADS):
        ol = slice(out_off + h * dv, out_off + (h + 1) * dv)
        mixed_ref[:, ol] = _rms(mixed_ref[:, ol], norm_w)


def _mixer_safe(q_ref, k_ref, vb_ref, b_ref, st_ref, norm_w, mixed_ref, out_off, dv, segment):
    row16 = lax.broadcasted_iota(jnp.int32, (SAFE_BLOCK, 1), 0)

    def block_body(sb, carry):
        s0 = pl.multiple_of(sb * SAFE_BLOCK, SAFE_BLOCK)
        rows_b = pl.ds(s0, SAFE_BLOCK)
        before = pl.multiple_of(jnp.maximum(s0 - 8, 0), 8)
        prev = b_ref[pl.ds(before, 8), :][7:8, :]
        prev = jnp.where(s0 % segment == 0, 0.0, prev)
        for h in range(N_HEADS):
            lanes = slice(h * HEAD_K, (h + 1) * HEAD_K)
            vl = slice(h * dv, (h + 1) * dv)
            bl = b_ref[rows_b, lanes] - prev[:, lanes]
            q = q_ref[rows_b, lanes]
            k = k_ref[rows_b, lanes]
            vb = vb_ref[rows_b, vl]
            v = vb.astype(F32)
            o = jnp.zeros((SAFE_BLOCK, dv), F32)
            for j in range(SAFE_BLOCK):
                w = jnp.exp(jnp.minimum(bl - bl[j:j + 1, :], 0.0))
                sc = jnp.sum(q * k[j:j + 1, :] * w, axis=-1, keepdims=True)
                o = o + jnp.where(row16 >= j, sc, 0.0) * v[j:j + 1, :]
            st = st_ref[h]
            o = o + _dot((q * jnp.exp(bl)).astype(BF16), st.astype(BF16))
            b_end = bl[SAFE_BLOCK - 1:SAFE_BLOCK, :]
            kl = (k * jnp.exp(b_end - bl)).astype(BF16)
            st_ref[h] = st * _column(jnp.exp(b_end)) + _dot_tn(kl, vb)
            mixed_ref[rows_b, out_off + h * dv:out_off + (h + 1) * dv] = _rms(o, norm_w)
        return carry

    lax.fori_loop(0, SEQ_TILE // SAFE_BLOCK, block_body, 0)


def _fast_path_ok(b_ref, n_chunks):
    chunk = MIX_TILE // n_chunks
    worst = jnp.zeros((1, KEY_WIDTH), F32)
    for c in range(n_chunks):
        r = b_ref[c * chunk + chunk // 2 - 1:c * chunk + chunk // 2, :]
        total = b_ref[(c + 1) * chunk - 1:(c + 1) * chunk, :]
        worst = jnp.maximum(worst, jnp.maximum(-r, r - total))
    return jnp.max(worst) < FAST_PATH_LIMIT


def _piece_order(gates, outs, epilogue):
    return [[gates[2]], [gates[3]], [outs[3]],
            [], [], [],
            [outs[2]], [], [],
            [], [], [],
            [outs[0], outs[1], epilogue]]


def _layer_kernel(x_ref, mem_ref, tri_tile_ref, tri_chunk_ref,
                  norm_w_ref, w_a_ref, w_b_ref, w_up_ref, b_gate_ref,
                  gla_norm_ref, lb_ref, hgrn_norm_ref, mem_norm_ref, w_mem_ref, xattn_norm_ref,
                  w_out_ref, final_norm_ref,
                  out_ref,
                  h_ref, gq_ref, gk_ref, gb_ref, hq_ref, hk_ref, hb_ref,
                  gv_ref, hv_ref, ghi_ref, glo_ref, hhi_ref, hlo_ref,
                  mixed_ref, gs_ref, xq_ref, mk_ref, mv_ref,
                  gla_st_ref, hgrn_st_ref, gla_prev_ref, hgrn_prev_ref,
                  g_qt_ref, g_kt_ref, g_qi_ref, g_ks_ref,
                  h_qt_ref, h_kt_ref, h_qo_ref, h_ko_ref, h_qi_ref, h_ks_ref, h_ql_ref, h_kl_ref,
                  p_ref,
                  *, layer, final_norm):
    t = pl.program_id(1)
    n_pass = SEQ_TILE // MIX_TILE
    gla_operands = (g_qt_ref, g_kt_ref, None, None, g_qi_ref, g_ks_ref, None, None, None)
    hgrn_operands = (h_qt_ref, h_kt_ref, h_qo_ref, h_ko_ref, h_qi_ref, h_ks_ref, h_ql_ref,
                     h_kl_ref, p_ref)
    zg_ref, zf_ref = gb_ref, hk_ref

    @pl.when(t == 0)
    def _start_of_sequence():
        gla_st_ref[...] = jnp.zeros_like(gla_st_ref)
        hgrn_st_ref[...] = jnp.zeros_like(hgrn_st_ref)
        m = _rms(mem_ref[0], mem_norm_ref[...]).astype(BF16)
        mk_ref[...] = _dot(m, w_mem_ref[:, :XATTN_WIDTH]).astype(BF16)
        mv_ref[...] = _dot(m, w_mem_ref[:, XATTN_WIDTH:]).astype(BF16)

    h_ref[...] = _rms(x_ref[0], norm_w_ref[...]).astype(BF16)

    def proj_a(off, width):
        return _dot(h_ref[...], w_a_ref[:, off:off + width])

    def proj(off, width):
        return _dot(h_ref[...], w_b_ref[:, off:off + width])

    lr_hq = proj(OFF_LR, LANE + KEY_WIDTH)
    low_rank = lr_hq[:, :LANE].astype(BF16)
    hq_ref[...] = lr_hq[:, LANE:]
    zg_ref[...] = _dot(low_rank, w_up_ref[...]) + b_gate_ref[...]
    zf_ref[...] = proj(OFF_HF, KEY_WIDTH)

    lbw = lb_ref[...]
    lb_e = jnp.exp(lbw - jnp.max(lbw, axis=0, keepdims=True))
    lb_p = lb_e / jnp.sum(lb_e, axis=0, keepdims=True)
    lb = jnp.sum(lb_p[:layer + 1], axis=0, keepdims=True) - lb_p[0:1]
    log_lb = jnp.log(jnp.maximum(lb, LB_FLOOR))
    log_1m_lb = jnp.log1p(-lb)
    one_m_lb = 1.0 - lb

    def decay(j):
        rows = slice(j * DECAY_ROWS, (j + 1) * DECAY_ROWS)
        z = zg_ref[rows, :]
        ls = jnp.minimum(z, 0.0) - jnp.log(1.0 + jnp.exp(-jnp.abs(z)))
        _store_split(ls * (1.0 / GLA_GATE_TAU), ghi_ref, glo_ref, rows)
        z = zf_ref[rows, :]
        e = jnp.exp(-jnp.abs(z))
        u = 1.0 + e
        c = log_1m_lb + (jnp.minimum(z, 0.0) - jnp.log(u))
        log_f = jnp.maximum(log_lb, c) + jnp.log(1.0 + jnp.exp(-jnp.abs(log_lb - c)))
        _store_split(log_f, hhi_ref, hlo_ref, rows)
        hk_ref[rows, :] = one_m_lb * (jnp.where(z >= 0.0, e, 1.0) / u)

    def gq_piece():
        gq_ref[...] = proj_a(OFF_GQ, KEY_WIDTH) * (HEAD_K ** -0.5)

    def gk_piece():
        gk_ref[...] = proj_a(OFF_GK, KEY_WIDTH)

    def gv_piece(c0):
        def emit():
            gv_ref[:, c0:c0 + KEY_WIDTH] = proj_a(OFF_GV + c0, KEY_WIDTH).astype(BF16)
        return emit

    def hv_piece():
        hv_ref[...] = proj(OFF_HI, HGRN_VAL_WIDTH).astype(BF16)

    xoff = GLA_VAL_WIDTH + HGRN_VAL_WIDTH
    gate_block = 512

    def gate_piece(c0):
        def emit():
            g = proj(OFF_GATE + c0, gate_block)
            gs_ref[:, c0:c0 + gate_block] = g * jax.nn.sigmoid(g)
        return emit

    def xq_piece():
        xq_ref[...] = proj(OFF_XQ, XATTN_WIDTH).astype(BF16)

    def xattn_piece(s, hd):
        def emit():
            rows = slice(s * MIX_TILE, (s + 1) * MIX_TILE)
            lanes = slice(hd * HEAD_K, (hd + 1) * HEAD_K)
            sc = _dot_nt(xq_ref[rows, lanes], mk_ref[:, lanes]) * (HEAD_K ** -0.5)
            p = jnp.exp(sc - jnp.max(sc, axis=-1, keepdims=True))
            o = _dot(p.astype(BF16), mv_ref[:, lanes]) / jnp.sum(p, axis=-1, keepdims=True)
            mixed_ref[rows, xoff + hd * HEAD_K:xoff + (hd + 1) * HEAD_K] = _rms(
                o, xattn_norm_ref[...])
        return emit

    out_started = [False]

    def out_piece(c0):
        def emit():
            cols = slice(c0, c0 + gate_block)
            gated = (mixed_ref[:, cols] * gs_ref[:, cols]).astype(BF16)
            part = _dot(gated, w_out_ref[cols, :])
            out_ref[0] = (out_ref[0] if out_started[0] else x_ref[0]) + part
            out_started[0] = True
        return emit

    def epilogue():
        if final_norm:
            out_ref[0] = _rms(out_ref[0], final_norm_ref[...])

    gates = [gate_piece(c0) for c0 in range(0, D_MIX, gate_block)]
    heads = [[xattn_piece(s, hd) for hd in range(N_HEADS)] for s in range(n_pass)]
    outs = [out_piece(c0) for c0 in range(0, D_MIX, gate_block)]

    xq_piece()
    matmul_pieces = [hv_piece, gq_piece, gk_piece, gv_piece(0), gv_piece(KEY_WIDTH),
                     gates[0], gates[1], None]
    vector_pieces = heads[0] + heads[1]
    assert len(matmul_pieces) == len(vector_pieces) == SEQ_TILE // DECAY_ROWS
    for j, piece in enumerate(matmul_pieces):
        if piece is not None:
            piece()
        decay(j)
        vector_pieces[j]()

    gla_ok = hgrn_ok = None
    for s in range(n_pass):
        rows = slice(s * MIX_TILE, (s + 1) * MIX_TILE)
        gb_ref[rows, :] = (_dot(tri_tile_ref[...], ghi_ref[rows, :])
                           + _dot(tri_tile_ref[...], glo_ref[rows, :]))
        hb_ref[rows, :] = (_dot(tri_chunk_ref[...], hhi_ref[rows, :])
                           + _dot(tri_chunk_ref[...], hlo_ref[rows, :]))
        g_ok = _fast_path_ok(gb_ref.at[rows], GLA_CHUNKS)
        h_ok = _fast_path_ok(hb_ref.at[rows], HGRN_CHUNKS)
        gla_ok = g_ok if gla_ok is None else jnp.logical_and(gla_ok, g_ok)
        hgrn_ok = h_ok if hgrn_ok is None else jnp.logical_and(hgrn_ok, h_ok)

    groups = iter(_piece_order(gates, outs, epilogue))

    def fill():
        for piece in next(groups, []):
            piece()

    gla_prev_ref[...] = gla_st_ref[...]
    hgrn_prev_ref[...] = hgrn_st_ref[...]
    pass_rows = [slice(s * MIX_TILE, (s + 1) * MIX_TILE) for s in range(n_pass)]
    for s, rows in enumerate(pass_rows):
        for _ in _mixer_fast(hq_ref.at[rows], hk_ref.at[rows], hv_ref.at[rows], hb_ref.at[rows],
                             hgrn_st_ref, hgrn_norm_ref[...], mixed_ref.at[rows],
                             GLA_VAL_WIDTH, HGRN_HEAD_V, HGRN_CHUNKS,
                             tuple(r.at[s] for r in hgrn_operands)):
            fill()
    for rows in pass_rows:
        for _ in _mixer_fast(gq_ref.at[rows], gk_ref.at[rows], gv_ref.at[rows], gb_ref.at[rows],
                             gla_st_ref, gla_norm_ref[...], mixed_ref.at[rows],
                             0, GLA_HEAD_V, GLA_CHUNKS, gla_operands):
            fill()
    for group in groups:
        for piece in group:
            piece()

    @pl.when(jnp.logical_not(jnp.logical_and(gla_ok, hgrn_ok)))
    def _redo_with_pairwise_path():
        @pl.when(jnp.logical_not(gla_ok))
        def _gla_pairwise():
            gla_st_ref[...] = gla_prev_ref[...]
            _mixer_safe(gq_ref, gk_ref, gv_ref, gb_ref, gla_st_ref, gla_norm_ref[...],
                        mixed_ref, 0, GLA_HEAD_V, MIX_TILE // GLA_CHUNKS)

        @pl.when(jnp.logical_not(hgrn_ok))
        def _hgrn_pairwise():
            hgrn_st_ref[...] = hgrn_prev_ref[...]
            _mixer_safe(hq_ref, hk_ref, hv_ref, hb_ref, hgrn_st_ref, hgrn_norm_ref[...],
                        mixed_ref, GLA_VAL_WIDTH, HGRN_HEAD_V, MIX_TILE // HGRN_CHUNKS)

        out_started[0] = False
        for piece in outs:
            piece()
        epilogue()


def _layer_call(x, mem, params, layer, final_norm):
    batch, seq, _ = x.shape
    n_tiles = seq // SEQ_TILE

    def resident(arr, stacked):
        if stacked:
            shape = (None,) + arr.shape[1:]
            index = (layer,) + (0,) * (arr.ndim - 1)
        else:
            shape = arr.shape
            index = (0,) * arr.ndim
        return pl.BlockSpec(shape, lambda b, t: index, pipeline_mode=pl.Buffered(1))

    in_specs = [
        pl.BlockSpec((1, SEQ_TILE, D_MODEL), lambda b, t: (b, t, 0)),
        pl.BlockSpec((1, MEM_LEN, D_MODEL), lambda b, t: (b, 0, 0), pipeline_mode=pl.Buffered(1)),
    ] + [resident(arr, stacked) for arr, stacked in params]
    params = [arr for arr, _ in params]
    key_f32 = pltpu.VMEM((SEQ_TILE, KEY_WIDTH), F32)
    key_bf16 = pltpu.VMEM((SEQ_TILE, KEY_WIDTH), BF16)
    operand = pltpu.VMEM((MIX_TILE, KEY_WIDTH), BF16)
    n_pass = SEQ_TILE // MIX_TILE
    per_pass = pltpu.VMEM((n_pass, MIX_TILE, KEY_WIDTH), BF16)
    per_pass_half = pltpu.VMEM((n_pass, MIX_TILE // 2, KEY_WIDTH), BF16)
    gla_state = pltpu.VMEM((N_HEADS, HEAD_K, GLA_HEAD_V), F32)
    hgrn_state = pltpu.VMEM((N_HEADS, HEAD_K, HGRN_HEAD_V), F32)
    scratch = [
        pltpu.VMEM((SEQ_TILE, D_MODEL), BF16),
        key_f32, key_f32, key_f32,
        key_f32, key_f32, key_f32,
        pltpu.VMEM((SEQ_TILE, GLA_VAL_WIDTH), BF16),
        pltpu.VMEM((SEQ_TILE, HGRN_VAL_WIDTH), BF16),
        key_bf16, key_bf16, key_bf16, key_bf16,
        pltpu.VMEM((SEQ_TILE, D_MIX), F32),
        pltpu.VMEM((SEQ_TILE, D_MIX), F32),
        key_bf16,
        pltpu.VMEM((MEM_LEN, XATTN_WIDTH), BF16),
        pltpu.VMEM((MEM_LEN, XATTN_WIDTH), BF16),
        gla_state, hgrn_state,
        gla_state, hgrn_state,
        operand, operand, operand, operand,
        per_pass, per_pass, per_pass_half, per_pass_half, per_pass, per_pass,
        per_pass_half, per_pass_half,
        pltpu.VMEM((n_pass, N_HEADS, MIX_TILE, MIX_TILE), BF16),
    ]
    return pl.pallas_call(
        functools.partial(_layer_kernel, layer=layer, final_norm=final_norm),
        out_shape=jax.ShapeDtypeStruct(x.shape, x.dtype),
        grid=(batch, n_tiles),
        in_specs=in_specs,
        out_specs=pl.BlockSpec((1, SEQ_TILE, D_MODEL), lambda b, t: (b, t, 0)),
        scratch_shapes=scratch,
        compiler_params=pltpu.CompilerParams(
            dimension_semantics=("arbitrary", "arbitrary"),
            vmem_limit_bytes=VMEM_LIMIT_BYTES),
        name=f"hybrid_layer_{layer}",
    )(x, mem, *params)


def _window_kernel(w_ref, o_ref):
    o_ref[...] = w_ref[0].astype(BF16).T


def _projection_window(w_t, start, width, step):
    depth, _, k = w_t.shape
    assert width % step == 0 and start % 16 == 0
    return pl.pallas_call(
        _window_kernel,
        out_shape=jax.ShapeDtypeStruct((depth, k, width), BF16),
        grid=(depth, width // step),
        in_specs=[pl.BlockSpec((pl.Element(1), pl.Element(step), pl.Element(k)),
                               lambda l, i: (l, pl.multiple_of(start + i * step, 16), 0))],
        out_specs=pl.BlockSpec((None, k, step), lambda l, i: (l, 0, i)),
        compiler_params=pltpu.CompilerParams(dimension_semantics=("arbitrary", "arbitrary")),
        name=f"projection_window_{start}",
    )(w_t)


def _lower_tri_ones(segment):
    idx = jnp.arange(MIX_TILE)
    keep = (idx[None, :] <= idx[:, None]) & ((idx[None, :] // segment) == (idx[:, None] // segment))
    return keep.astype(BF16)


def kernel(x, mem, norm_w, w_in, gla_w_gate_up, gla_b_gate, gla_norm_w, hgrn_lower_bounds,
           hgrn_norm_w, mem_norm_w, w_mem_kv, xattn_norm_w, w_out, final_norm_w):
    assert x.shape[1] % SEQ_TILE == 0 and SEQ_TILE % MIX_TILE == 0
    rows = lambda v: (v.reshape(DEPTH, 1, -1).astype(F32), True)
    w_t = jnp.swapaxes(w_in, 1, 2)
    w_a = _projection_window(w_t, 0, WIN_A_WIDTH, WIN_A_STEP)
    w_b = _projection_window(w_t, WIN_B_START, w_in.shape[2] - WIN_B_START, WIN_B_STEP)
    w_up = jnp.pad(gla_w_gate_up, ((0, 0), (LOWRANK_LEAD, 0), (0, 0))).astype(BF16)
    params = (
        (_lower_tri_ones(MIX_TILE // GLA_CHUNKS), False),
        (_lower_tri_ones(MIX_TILE // HGRN_CHUNKS), False),
        rows(norm_w), (w_a, True), (w_b, True), (w_up, True), rows(gla_b_gate),
        rows(gla_norm_w), (hgrn_lower_bounds.astype(F32), False), rows(hgrn_norm_w),
        rows(mem_norm_w), (w_mem_kv.astype(BF16), True), rows(xattn_norm_w),
        (w_out.astype(BF16), True), (final_norm_w.reshape(1, -1).astype(F32), False),
    )
    for layer in range(DEPTH):
        x = _layer_call(x, mem, params, layer, final_norm=(layer == DEPTH - 1))
    return x
```

```python
import functools

import jax
import jax.numpy as jnp
from jax import lax
from jax.experimental import pallas as pl
from jax.experimental.pallas import tpu as pltpu

F32 = jnp.float32
BF16 = jnp.bfloat16

D_MODEL = 1024
DEPTH = 2
MEM_LEN = 256
N_HEADS = 4
HEAD_K = 128
GLA_HEAD_V = 256
HGRN_HEAD_V = 128
KEY_WIDTH = N_HEADS * HEAD_K
GLA_VAL_WIDTH = N_HEADS * GLA_HEAD_V
HGRN_VAL_WIDTH = N_HEADS * HGRN_HEAD_V
XATTN_WIDTH = N_HEADS * HEAD_K
D_MIX = 2048
GLA_LOWRANK = 16
GLA_GATE_TAU = 16.0
LB_FLOOR = 1e-30
NORM_EPS = 1e-6

LANE = 128
LOWRANK_LEAD = LANE - GLA_LOWRANK
WIN_A_WIDTH = 2 * KEY_WIDTH + GLA_VAL_WIDTH
WIN_B_START = WIN_A_WIDTH - LOWRANK_LEAD
OFF_GQ, OFF_GK, OFF_GV = 0, KEY_WIDTH, 2 * KEY_WIDTH
OFF_LR = 0
OFF_HQ = OFF_LR + LANE
OFF_HF = OFF_HQ + KEY_WIDTH
OFF_HI = OFF_HF + KEY_WIDTH
OFF_XQ = OFF_HI + HGRN_VAL_WIDTH
OFF_GATE = OFF_XQ + XATTN_WIDTH

SEQ_TILE = 512
MIX_TILE = 256
GLA_CHUNKS = 1
HGRN_CHUNKS = 4
DECAY_ROWS = 64
SAFE_BLOCK = 16
FAST_PATH_LIMIT = 60.0

WIN_A_STEP, WIN_B_STEP = 1024, 1408

VMEM_LIMIT_BYTES = 58 * 1024 * 1024


def _dot(a, b):
    return jnp.dot(a, b, preferred_element_type=F32)


def _dot_nt(a, b):
    return lax.dot_general(a, b, (((1,), (1,)), ((), ())), preferred_element_type=F32)


def _dot_tn(a, b):
    return lax.dot_general(a, b, (((0,), (0,)), ((), ())), preferred_element_type=F32)


def _rms(x, w):
    return x * lax.rsqrt(jnp.mean(x * x, axis=-1, keepdims=True) + NORM_EPS) * w


def _store_split(x, hi_ref, lo_ref, rows):
    hi = x.astype(BF16)
    hi_ref[rows, :] = hi
    lo_ref[rows, :] = (x - hi.astype(F32)).astype(BF16)


def _column(row):
    return jnp.broadcast_to(row, (8, row.shape[1])).T[:, 0:1]


def _mixer_fast(q_ref, k_ref, vb_ref, b_ref, st_ref, norm_w, mixed_ref, out_off, dv, n_chunks,
                operand_refs):
    assert n_chunks in (1, 4)
    qt_ref, kt_ref, qo_ref, ko_ref, qi_ref, ks_ref, ql_ref, kl_ref, p_ref = operand_refs
    chunk = MIX_TILE // n_chunks
    half = MIX_TILE // 2
    tot = [b_ref[(c + 1) * chunk - 1:(c + 1) * chunk, :] for c in range(n_chunks)]
    ref = [b_ref[c * chunk + chunk // 2 - 1:c * chunk + chunk // 2, :] for c in range(n_chunks)]
    before = [jnp.zeros_like(tot[0])]
    for c in range(n_chunks):
        before.append(before[-1] + tot[c])
    tile_total = before[n_chunks]
    decay_col = _column(jnp.exp(tile_total))
    if n_chunks == 4:
        e_in = [None] + [jnp.exp(before[c]) for c in range(1, 4)]
        e_out = [jnp.exp(tile_total - before[c + 1]) for c in range(3)] + [None]
        e_tot1 = jnp.exp(tot[1])
        e_tot2 = jnp.exp(tot[2])

    for c in range(n_chunks):
        rows = slice(c * chunk, (c + 1) * chunk)
        for h in range(N_HEADS):
            lanes = slice(h * HEAD_K, (h + 1) * HEAD_K)
            b = b_ref[rows, lanes]
            q = q_ref[rows, lanes]
            k = k_ref[rows, lanes]
            r = ref[c][:, lanes]
            qe = q * jnp.exp(b)
            ke = k * jnp.exp(tot[c][:, lanes] - b)
            qt_ref[rows, lanes] = (q * jnp.exp(b - r)).astype(BF16)
            kt_ref[rows, lanes] = (k * jnp.exp(r - b)).astype(BF16)
            if n_chunks == 1:
                qi_ref[rows, lanes] = qe.astype(BF16)
                ks_ref[rows, lanes] = ke.astype(BF16)
            else:
                pair = slice((c // 2) * chunk, (c // 2 + 1) * chunk)
                if c % 2 == 1:
                    qo_ref[pair, lanes] = qe.astype(BF16)
                else:
                    ko_ref[pair, lanes] = ke.astype(BF16)
                qi = qe if c == 0 else qe * e_in[c][:, lanes]
                ks = ke if c == 3 else ke * e_out[c][:, lanes]
                qi_ref[rows, lanes] = qi.astype(BF16)
                ks_ref[rows, lanes] = ks.astype(BF16)
                if c == 0:
                    kl_ref[0:chunk, lanes] = (ke * e_tot1[:, lanes]).astype(BF16)
                elif c == 1:
                    kl_ref[chunk:half, lanes] = ke.astype(BF16)
                elif c == 2:
                    ql_ref[0:chunk, lanes] = qe.astype(BF16)
                else:
                    ql_ref[chunk:half, lanes] = (qe * e_tot2[:, lanes]).astype(BF16)
    yield

    if n_chunks == 1:
        row = lax.broadcasted_iota(jnp.int32, (MIX_TILE, MIX_TILE), 0)
        col = lax.broadcasted_iota(jnp.int32, (MIX_TILE, MIX_TILE), 1)
        causal = row >= col
        for h in range(N_HEADS):
            lanes = slice(h * HEAD_K, (h + 1) * HEAD_K)
            vl = slice(h * dv, (h + 1) * dv)
            s = _dot_nt(qt_ref[:, lanes], kt_ref[:, lanes])
            p = jnp.where(causal, s, 0.0).astype(BF16)
            st = st_ref[h]
            o = _dot(p, vb_ref[:, vl]) + _dot(qi_ref[:, lanes], st.astype(BF16))
            st_ref[h] = st * decay_col[lanes, :] + _dot_tn(ks_ref[:, lanes], vb_ref[:, vl])
            mixed_ref[:, out_off + h * dv:out_off + (h + 1) * dv] = _rms(o, norm_w)
            if h % 2 == 1:
                yield
        return

    row = lax.broadcasted_iota(jnp.int32, (half, half), 0)
    col = lax.broadcasted_iota(jnp.int32, (half, half), 1)
    same_chunk = ((row >= chunk) == (col >= chunk)) & (row >= col)
    for h in range(N_HEADS):
        lanes = slice(h * HEAD_K, (h + 1) * HEAD_K)
        for hf in range(2):
            rows = slice(hf * half, (hf + 1) * half)
            d = _dot_nt(qt_ref[rows, lanes], kt_ref[rows, lanes])
            p_ref[h, rows, rows] = jnp.where(same_chunk, d, 0.0).astype(BF16)
            pair = slice(hf * chunk, (hf + 1) * chunk)
            lo = hf * half
            p_ref[h, lo + chunk:lo + half, lo:lo + chunk] = _dot_nt(
                qo_ref[pair, lanes], ko_ref[pair, lanes]).astype(BF16)
        p_ref[h, half:, 0:half] = _dot_nt(ql_ref[:, lanes], kl_ref[:, lanes]).astype(BF16)
    yield
    for h in range(N_HEADS):
        lanes = slice(h * HEAD_K, (h + 1) * HEAD_K)
        vl = slice(h * dv, (h + 1) * dv)
        ol = slice(out_off + h * dv, out_off + (h + 1) * dv)
        st = st_ref[h]
        st_b = st.astype(BF16)
        mixed_ref[0:half, ol] = (_dot(p_ref[h, 0:half, 0:half], vb_ref[0:half, vl])
                                 + _dot(qi_ref[0:half, lanes], st_b))
        mixed_ref[half:, ol] = (_dot(p_ref[h, half:, :], vb_ref[:, vl])
                                + _dot(qi_ref[half:, lanes], st_b))
        st_ref[h] = st * decay_col[lanes, :] + _dot_tn(ks_ref[:, lanes], vb_ref[:, vl])
    yield
    for h in range(N_HEADS):
        ol = slice(out_off + h * dv, out_off + (h + 1) * dv)
        mixed_ref[:, ol] = _rms(mixed_ref[:, ol], norm_w)


def _mixer_safe(q_ref, k_ref, vb_ref, b_ref, st_ref, norm_w, mixed_ref, out_off, dv, segment):
    row16 = lax.broadcasted_iota(jnp.int32, (SAFE_BLOCK, 1), 0)

    def block_body(sb, carry):
        s0 = pl.multiple_of(sb * SAFE_BLOCK, SAFE_BLOCK)
        rows_b = pl.ds(s0, SAFE_BLOCK)
        before = pl.multiple_of(jnp.maximum(s0 - 8, 0), 8)
        prev = b_ref[pl.ds(before, 8), :][7:8, :]
        prev = jnp.where(s0 % segment == 0, 0.0, prev)
        for h in range(N_HEADS):
            lanes = slice(h * HEAD_K, (h + 1) * HEAD_K)
            vl = slice(h * dv, (h + 1) * dv)
            bl = b_ref[rows_b, lanes] - prev[:, lanes]
            q = q_ref[rows_b, lanes]
            k = k_ref[rows_b, lanes]
            vb = vb_ref[rows_b, vl]
            v = vb.astype(F32)
            o = jnp.zeros((SAFE_BLOCK, dv), F32)
            for j in range(SAFE_BLOCK):
                w = jnp.exp(jnp.minimum(bl - bl[j:j + 1, :], 0.0))
                sc = jnp.sum(q * k[j:j + 1, :] * w, axis=-1, keepdims=True)
                o = o + jnp.where(row16 >= j, sc, 0.0) * v[j:j + 1, :]
            st = st_ref[h]
            o = o + _dot((q * jnp.exp(bl)).astype(BF16), st.astype(BF16))
            b_end = bl[SAFE_BLOCK - 1:SAFE_BLOCK, :]
            kl = (k * jnp.exp(b_end - bl)).astype(BF16)
            st_ref[h] = st * _column(jnp.exp(b_end)) + _dot_tn(kl, vb)
            mixed_ref[rows_b, out_off + h * dv:out_off + (h + 1) * dv] = _rms(o, norm_w)
        return carry

    lax.fori_loop(0, SEQ_TILE // SAFE_BLOCK, block_body, 0)


def _fast_path_ok(b_ref, n_chunks):
    chunk = MIX_TILE // n_chunks
    worst = jnp.zeros((1, KEY_WIDTH), F32)
    for c in range(n_chunks):
        r = b_ref[c * chunk + chunk // 2 - 1:c * chunk + chunk // 2, :]
        total = b_ref[(c + 1) * chunk - 1:(c + 1) * chunk, :]
        worst = jnp.maximum(worst, jnp.maximum(-r, r - total))
    return jnp.max(worst) < FAST_PATH_LIMIT


def _piece_order(gates, outs, epilogue):
    return [[gates[2]], [gates[3]], [outs[3]],
            [], [], [],
            [outs[2]], [], [],
            [], [outs[0]], [],
            [outs[1], epilogue]]


def _layer_kernel(x_ref, mem_ref, tri_tile_ref, tri_chunk_ref,
                  norm_w_ref, w_a_ref, w_b_ref, w_up_ref, b_gate_ref,
                  gla_norm_ref, lb_ref, hgrn_norm_ref, mem_norm_ref, w_mem_ref, xattn_norm_ref,
                  w_out_ref, final_norm_ref,
                  out_ref,
                  h_ref, gq_ref, gk_ref, gb_ref, hq_ref, hk_ref, hb_ref,
                  gv_ref, hv_ref, ghi_ref, glo_ref, hhi_ref, hlo_ref,
                  mixed_ref, gs_ref, xq_ref, mk_ref, mv_ref,
                  gla_st_ref, hgrn_st_ref, gla_prev_ref, hgrn_prev_ref,
                  g_qt_ref, g_kt_ref, g_qi_ref, g_ks_ref,
                  h_qt_ref, h_kt_ref, h_qo_ref, h_ko_ref, h_qi_ref, h_ks_ref, h_ql_ref, h_kl_ref,
                  p_ref,
                  *, layer, final_norm):
    t = pl.program_id(1)
    n_pass = SEQ_TILE // MIX_TILE
    gla_operands = (g_qt_ref, g_kt_ref, None, None, g_qi_ref, g_ks_ref, None, None, None)
    hgrn_operands = (h_qt_ref, h_kt_ref, h_qo_ref, h_ko_ref, h_qi_ref, h_ks_ref, h_ql_ref,
                     h_kl_ref, p_ref)
    zg_ref, zf_ref = gb_ref, hk_ref

    @pl.when(t == 0)
    def _start_of_sequence():
        gla_st_ref[...] = jnp.zeros_like(gla_st_ref)
        hgrn_st_ref[...] = jnp.zeros_like(hgrn_st_ref)
        m = _rms(mem_ref[0], mem_norm_ref[...]).astype(BF16)
        mk_ref[...] = _dot(m, w_mem_ref[:, :XATTN_WIDTH]).astype(BF16)
        mv_ref[...] = _dot(m, w_mem_ref[:, XATTN_WIDTH:]).astype(BF16)

    h_ref[...] = _rms(x_ref[0], norm_w_ref[...]).astype(BF16)

    def proj_a(off, width):
        return _dot(h_ref[...], w_a_ref[:, off:off + width])

    def proj(off, width):
        return _dot(h_ref[...], w_b_ref[:, off:off + width])

    lr_hq = proj(OFF_LR, LANE + KEY_WIDTH)
    low_rank = lr_hq[:, :LANE].astype(BF16)
    hq_ref[...] = lr_hq[:, LANE:]
    zg_ref[...] = _dot(low_rank, w_up_ref[...]) + b_gate_ref[...]
    zf_ref[...] = proj(OFF_HF, KEY_WIDTH)

    lbw = lb_ref[...]
    lb_e = jnp.exp(lbw - jnp.max(lbw, axis=0, keepdims=True))
    lb_p = lb_e / jnp.sum(lb_e, axis=0, keepdims=True)
    lb = jnp.sum(lb_p[:layer + 1], axis=0, keepdims=True) - lb_p[0:1]
    log_lb = jnp.log(jnp.maximum(lb, LB_FLOOR))
    log_1m_lb = jnp.log1p(-lb)
    one_m_lb = 1.0 - lb

    def decay(j):
        rows = slice(j * DECAY_ROWS, (j + 1) * DECAY_ROWS)
        z = zg_ref[rows, :]
        ls = jnp.minimum(z, 0.0) - jnp.log(1.0 + jnp.exp(-jnp.abs(z)))
        _store_split(ls * (1.0 / GLA_GATE_TAU), ghi_ref, glo_ref, rows)
        z = zf_ref[rows, :]
        e = jnp.exp(-jnp.abs(z))
        u = 1.0 + e
        c = log_1m_lb + (jnp.minimum(z, 0.0) - jnp.log(u))
        log_f = jnp.maximum(log_lb, c) + jnp.log(1.0 + jnp.exp(-jnp.abs(log_lb - c)))
        _store_split(log_f, hhi_ref, hlo_ref, rows)
        hk_ref[rows, :] = one_m_lb * (jnp.where(z >= 0.0, e, 1.0) / u)

    def gq_piece():
        gq_ref[...] = proj_a(OFF_GQ, KEY_WIDTH) * (HEAD_K ** -0.5)

    def gk_piece():
        gk_ref[...] = proj_a(OFF_GK, KEY_WIDTH)

    def gv_piece(c0):
        def emit():
            gv_ref[:, c0:c0 + KEY_WIDTH] = proj_a(OFF_GV + c0, KEY_WIDTH).astype(BF16)
        return emit

    def hv_piece():
        hv_ref[...] = proj(OFF_HI, HGRN_VAL_WIDTH).astype(BF16)

    xoff = GLA_VAL_WIDTH + HGRN_VAL_WIDTH
    gate_block = 512

    def gate_piece(c0):
        def emit():
            g = proj(OFF_GATE + c0, gate_block)
            gs_ref[:, c0:c0 + gate_block] = g * jax.nn.sigmoid(g)
        return emit

    def xq_piece():
        xq_ref[...] = proj(OFF_XQ, XATTN_WIDTH).astype(BF16)

    def xattn_piece(s, hd):
        def emit():
            rows = slice(s * MIX_TILE, (s + 1) * MIX_TILE)
            lanes = slice(hd * HEAD_K, (hd + 1) * HEAD_K)
            sc = _dot_nt(xq_ref[rows, lanes], mk_ref[:, lanes]) * (HEAD_K ** -0.5)
            p = jnp.exp(sc - jnp.max(sc, axis=-1, keepdims=True))
            o = _dot(p.astype(BF16), mv_ref[:, lanes]) / jnp.sum(p, axis=-1, keepdims=True)
            mixed_ref[rows, xoff + hd * HEAD_K:xoff + (hd + 1) * HEAD_K] = _rms(
                o, xattn_norm_ref[...])
        return emit

    out_started = [False]

    def out_piece(c0):
        def emit():
            cols = slice(c0, c0 + gate_block)
            gated = (mixed_ref[:, cols] * gs_ref[:, cols]).astype(BF16)
            part = _dot(gated, w_out_ref[cols, :])
            out_ref[0] = (out_ref[0] if out_started[0] else x_ref[0]) + part
            out_started[0] = True
        return emit

    def epilogue():
        if final_norm:
            out_ref[0] = _rms(out_ref[0], final_norm_ref[...])

    gates = [gate_piece(c0) for c0 in range(0, D_MIX, gate_block)]
    heads = [[xattn_piece(s, hd) for hd in range(N_HEADS)] for s in range(n_pass)]
    outs = [out_piece(c0) for c0 in range(0, D_MIX, gate_block)]

    xq_piece()
    matmul_pieces = [hv_piece, gq_piece, gk_piece, gv_piece(0), gv_piece(KEY_WIDTH),
                     gates[0], gates[1], None]
    vector_pieces = heads[0] + heads[1]
    assert len(matmul_pieces) == len(vector_pieces) == SEQ_TILE // DECAY_ROWS
    for j, piece in enumerate(matmul_pieces):
        if piece is not None:
            piece()
        if j < 4:
            vector_pieces[2 * j]()
            vector_pieces[2 * j + 1]()
        else:
            decay(2 * (j - 4))
            decay(2 * (j - 4) + 1)

    gla_ok = hgrn_ok = None
    for s in range(n_pass):
        rows = slice(s * MIX_TILE, (s + 1) * MIX_TILE)
        gb_ref[rows, :] = (_dot(tri_tile_ref[...], ghi_ref[rows, :])
                           + _dot(tri_tile_ref[...], glo_ref[rows, :]))
        hb_ref[rows, :] = (_dot(tri_chunk_ref[...], hhi_ref[rows, :])
                           + _dot(tri_chunk_ref[...], hlo_ref[rows, :]))
        g_ok = _fast_path_ok(gb_ref.at[rows], GLA_CHUNKS)
        h_ok = _fast_path_ok(hb_ref.at[rows], HGRN_CHUNKS)
        gla_ok = g_ok if gla_ok is None else jnp.logical_and(gla_ok, g_ok)
        hgrn_ok = h_ok if hgrn_ok is None else jnp.logical_and(hgrn_ok, h_ok)

    groups = iter(_piece_order(gates, outs, epilogue))

    def fill():
        for piece in next(groups, []):
            piece()

    gla_prev_ref[...] = gla_st_ref[...]
    hgrn_prev_ref[...] = hgrn_st_ref[...]
    pass_rows = [slice(s * MIX_TILE, (s + 1) * MIX_TILE) for s in range(n_pass)]
    for s, rows in enumerate(pass_rows):
        for _ in _mixer_fast(hq_ref.at[rows], hk_ref.at[rows], hv_ref.at[rows], hb_ref.at[rows],
                             hgrn_st_ref, hgrn_norm_ref[...], mixed_ref.at[rows],
                             GLA_VAL_WIDTH, HGRN_HEAD_V, HGRN_CHUNKS,
                             tuple(r.at[s] for r in hgrn_operands)):
            fill()
    for rows in pass_rows:
        for _ in _mixer_fast(gq_ref.at[rows], gk_ref.at[rows], gv_ref.at[rows], gb_ref.at[rows],
                             gla_st_ref, gla_norm_ref[...], mixed_ref.at[rows],
                             0, GLA_HEAD_V, GLA_CHUNKS, gla_operands):
            fill()
    for group in groups:
        for piece in group:
            piece()

    @pl.when(jnp.logical_not(jnp.logical_and(gla_ok, hgrn_ok)))
    def _redo_with_pairwise_path():
        @pl.when(jnp.logical_not(gla_ok))
        def _gla_pairwise():
            gla_st_ref[...] = gla_prev_ref[...]
            _mixer_safe(gq_ref, gk_ref, gv_ref, gb_ref, gla_st_ref, gla_norm_ref[...],
                        mixed_ref, 0, GLA_HEAD_V, MIX_TILE // GLA_CHUNKS)

        @pl.when(jnp.logical_not(hgrn_ok))
        def _hgrn_pairwise():
            hgrn_st_ref[...] = hgrn_prev_ref[...]
            _mixer_safe(hq_ref, hk_ref, hv_ref, hb_ref, hgrn_st_ref, hgrn_norm_ref[...],
                        mixed_ref, GLA_VAL_WIDTH, HGRN_HEAD_V, MIX_TILE // HGRN_CHUNKS)

        out_started[0] = False
        for piece in outs:
            piece()
        epilogue()


def _layer_call(x, mem, params, layer, final_norm):
    batch, seq, _ = x.shape
    n_tiles = seq // SEQ_TILE

    def resident(arr, stacked):
        if stacked:
            shape = (None,) + arr.shape[1:]
            index = (layer,) + (0,) * (arr.ndim - 1)
        else:
            shape = arr.shape
            index = (0,) * arr.ndim
        return pl.BlockSpec(shape, lambda b, t: index, pipeline_mode=pl.Buffered(1))

    in_specs = [
        pl.BlockSpec((1, SEQ_TILE, D_MODEL), lambda b, t: (b, t, 0)),
        pl.BlockSpec((1, MEM_LEN, D_MODEL), lambda b, t: (b, 0, 0), pipeline_mode=pl.Buffered(1)),
    ] + [resident(arr, stacked) for arr, stacked in params]
    params = [arr for arr, _ in params]
    key_f32 = pltpu.VMEM((SEQ_TILE, KEY_WIDTH), F32)
    key_bf16 = pltpu.VMEM((SEQ_TILE, KEY_WIDTH), BF16)
    operand = pltpu.VMEM((MIX_TILE, KEY_WIDTH), BF16)
    n_pass = SEQ_TILE // MIX_TILE
    per_pass = pltpu.VMEM((n_pass, MIX_TILE, KEY_WIDTH), BF16)
    per_pass_half = pltpu.VMEM((n_pass, MIX_TILE // 2, KEY_WIDTH), BF16)
    gla_state = pltpu.VMEM((N_HEADS, HEAD_K, GLA_HEAD_V), F32)
    hgrn_state = pltpu.VMEM((N_HEADS, HEAD_K, HGRN_HEAD_V), F32)
    scratch = [
        pltpu.VMEM((SEQ_TILE, D_MODEL), BF16),
        key_f32, key_f32, key_f32,
        key_f32, key_f32, key_f32,
        pltpu.VMEM((SEQ_TILE, GLA_VAL_WIDTH), BF16),
        pltpu.VMEM((SEQ_TILE, HGRN_VAL_WIDTH), BF16),
        key_bf16, key_bf16, key_bf16, key_bf16,
        pltpu.VMEM((SEQ_TILE, D_MIX), F32),
        pltpu.VMEM((SEQ_TILE, D_MIX), F32),
        key_bf16,
        pltpu.VMEM((MEM_LEN, XATTN_WIDTH), BF16),
        pltpu.VMEM((MEM_LEN, XATTN_WIDTH), BF16),
        gla_state, hgrn_state,
        gla_state, hgrn_state,
        operand, operand, operand, operand,
        per_pass, per_pass, per_pass_half, per_pass_half, per_pass, per_pass,
        per_pass_half, per_pass_half,
        pltpu.VMEM((n_pass, N_HEADS, MIX_TILE, MIX_TILE), BF16),
    ]
    return pl.pallas_call(
        functools.partial(_layer_kernel, layer=layer, final_norm=final_norm),
        out_shape=jax.ShapeDtypeStruct(x.shape, x.dtype),
        grid=(batch, n_tiles),
        in_specs=in_specs,
        out_specs=pl.BlockSpec((1, SEQ_TILE, D_MODEL), lambda b, t: (b, t, 0)),
        scratch_shapes=scratch,
        compiler_params=pltpu.CompilerParams(
            dimension_semantics=("arbitrary", "arbitrary"),
            vmem_limit_bytes=VMEM_LIMIT_BYTES),
        name=f"hybrid_layer_{layer}",
    )(x, mem, *params)


def _window_kernel(w_ref, o_ref):
    o_ref[...] = w_ref[0].astype(BF16).T


def _projection_window(w_t, start, width, step):
    depth, _, k = w_t.shape
    assert width % step == 0 and start % 16 == 0
    return pl.pallas_call(
        _window_kernel,
        out_shape=jax.ShapeDtypeStruct((depth, k, width), BF16),
        grid=(depth, width // step),
        in_specs=[pl.BlockSpec((pl.Element(1), pl.Element(step), pl.Element(k)),
                               lambda l, i: (l, pl.multiple_of(start + i * step, 16), 0))],
        out_specs=pl.BlockSpec((None, k, step), lambda l, i: (l, 0, i)),
        compiler_params=pltpu.CompilerParams(dimension_semantics=("arbitrary", "arbitrary")),
        name=f"projection_window_{start}",
    )(w_t)


def _lower_tri_ones(segment):
    idx = jnp.arange(MIX_TILE)
    keep = (idx[None, :] <= idx[:, None]) & ((idx[None, :] // segment) == (idx[:, None] // segment))
    return keep.astype(BF16)


def kernel(x, mem, norm_w, w_in, gla_w_gate_up, gla_b_gate, gla_norm_w, hgrn_lower_bounds,
           hgrn_norm_w, mem_norm_w, w_mem_kv, xattn_norm_w, w_out, final_norm_w):
    assert x.shape[1] % SEQ_TILE == 0 and SEQ_TILE % MIX_TILE == 0
    rows = lambda v: (v.reshape(DEPTH, 1, -1).astype(F32), True)
    w_t = jnp.swapaxes(w_in, 1, 2)
    w_a = _projection_window(w_t, 0, WIN_A_WIDTH, WIN_A_STEP)
    w_b = _projection_window(w_t, WIN_B_START, w_in.shape[2] - WIN_B_START, WIN_B_STEP)
    w_up = jnp.pad(gla_w_gate_up, ((0, 0), (LOWRANK_LEAD, 0), (0, 0))).astype(BF16)
    params = (
        (_lower_tri_ones(MIX_TILE // GLA_CHUNKS), False),
        (_lower_tri_ones(MIX_TILE // HGRN_CHUNKS), False),
        rows(norm_w), (w_a, True), (w_b, True), (w_up, True), rows(gla_b_gate),
        rows(gla_norm_w), (hgrn_lower_bounds.astype(F32), False), rows(hgrn_norm_w),
        rows(mem_norm_w), (w_mem_kv.astype(BF16), True), rows(xattn_norm_w),
        (w_out.astype(BF16), True), (final_norm_w.reshape(1, -1).astype(F32), False),
    )
    for layer in range(DEPTH):
        x = _layer_call(x, mem, params, layer, final_norm=(layer == DEPTH - 1))
    return x
```

```python
import functools

import jax
import jax.numpy as jnp
from jax import lax
from jax.experimental import pallas as pl
from jax.experimental.pallas import tpu as pltpu

F32 = jnp.float32
BF16 = jnp.bfloat16

D_MODEL = 1024
DEPTH = 2
MEM_LEN = 256
N_HEADS = 4
HEAD_K = 128
GLA_HEAD_V = 256
HGRN_HEAD_V = 128
KEY_WIDTH = N_HEADS * HEAD_K
GLA_VAL_WIDTH = N_HEADS * GLA_HEAD_V
HGRN_VAL_WIDTH = N_HEADS * HGRN_HEAD_V
XATTN_WIDTH = N_HEADS * HEAD_K
D_MIX = 2048
GLA_LOWRANK = 16
GLA_GATE_TAU = 16.0
LB_FLOOR = 1e-30
NORM_EPS = 1e-6

LANE = 128
LOWRANK_LEAD = LANE - GLA_LOWRANK
WIN_A_WIDTH = 2 * KEY_WIDTH + GLA_VAL_WIDTH
WIN_B_START = WIN_A_WIDTH - LOWRANK_LEAD
OFF_GQ, OFF_GK, OFF_GV = 0, KEY_WIDTH, 2 * KEY_WIDTH
OFF_LR = 0
OFF_HQ = OFF_LR + LANE
OFF_HF = OFF_HQ + KEY_WIDTH
OFF_HI = OFF_HF + KEY_WIDTH
OFF_XQ = OFF_HI + HGRN_VAL_WIDTH
OFF_GATE = OFF_XQ + XATTN_WIDTH

SEQ_TILE = 512
MIX_TILE = 256
GLA_CHUNKS = 1
HGRN_CHUNKS = 4
DECAY_ROWS = 64
SAFE_BLOCK = 16
FAST_PATH_LIMIT = 60.0

WIN_A_STEP, WIN_B_STEP = 1024, 1408

VMEM_LIMIT_BYTES = 58 * 1024 * 1024


def _dot(a, b):
    return jnp.dot(a, b, preferred_element_type=F32)


def _dot_nt(a, b):
    return lax.dot_general(a, b, (((1,), (1,)), ((), ())), preferred_element_type=F32)


def _dot_tn(a, b):
    return lax.dot_general(a, b, (((0,), (0,)), ((), ())), preferred_element_type=F32)


def _rms(x, w):
    return x * lax.rsqrt(jnp.mean(x * x, axis=-1, keepdims=True) + NORM_EPS) * w


def _store_split(x, hi_ref, lo_ref, rows):
    hi = x.astype(BF16)
    hi_ref[rows, :] = hi
    lo_ref[rows, :] = (x - hi.astype(F32)).astype(BF16)


def _column(row):
    return jnp.broadcast_to(row, (8, row.shape[1])).T[:, 0:1]


def _mixer_fast(q_ref, k_ref, vb_ref, b_ref, st_ref, norm_w, mixed_ref, out_off, dv, n_chunks,
                operand_refs):
    assert n_chunks in (1, 4)
    qt_ref, kt_ref, qo_ref, ko_ref, qi_ref, ks_ref, ql_ref, kl_ref, p_ref = operand_refs
    chunk = MIX_TILE // n_chunks
    half = MIX_TILE // 2
    tot = [b_ref[(c + 1) * chunk - 1:(c + 1) * chunk, :] for c in range(n_chunks)]
    ref = [b_ref[c * chunk + chunk // 2 - 1:c * chunk + chunk // 2, :] for c in range(n_chunks)]
    before = [jnp.zeros_like(tot[0])]
    for c in range(n_chunks):
        before.append(before[-1] + tot[c])
    tile_total = before[n_chunks]
    decay_col = _column(jnp.exp(tile_total))
    if n_chunks == 4:
        e_in = [None] + [jnp.exp(before[c]) for c in range(1, 4)]
        e_out = [jnp.exp(tile_total - before[c + 1]) for c in range(3)] + [None]
        e_tot1 = jnp.exp(tot[1])
        e_tot2 = jnp.exp(tot[2])

    for c in range(n_chunks):
        rows = slice(c * chunk, (c + 1) * chunk)
        for h in range(N_HEADS):
            lanes = slice(h * HEAD_K, (h + 1) * HEAD_K)
            b = b_ref[rows, lanes]
            q = q_ref[rows, lanes]
            k = k_ref[rows, lanes]
            r = ref[c][:, lanes]
            qe = q * jnp.exp(b)
            ke = k * jnp.exp(tot[c][:, lanes] - b)
            qt_ref[rows, lanes] = (q * jnp.exp(b - r)).astype(BF16)
            kt_ref[rows, lanes] = (k * jnp.exp(r - b)).astype(BF16)
            if n_chunks == 1:
                qi_ref[rows, lanes] = qe.astype(BF16)
                ks_ref[rows, lanes] = ke.astype(BF16)
            else:
                pair = slice((c // 2) * chunk, (c // 2 + 1) * chunk)
                if c % 2 == 1:
                    qo_ref[pair, lanes] = qe.astype(BF16)
                else:
                    ko_ref[pair, lanes] = ke.astype(BF16)
                qi = qe if c == 0 else qe * e_in[c][:, lanes]
                ks = ke if c == 3 else ke * e_out[c][:, lanes]
                qi_ref[rows, lanes] = qi.astype(BF16)
                ks_ref[rows, lanes] = ks.astype(BF16)
                if c == 0:
                    kl_ref[0:chunk, lanes] = (ke * e_tot1[:, lanes]).astype(BF16)
                elif c == 1:
                    kl_ref[chunk:half, lanes] = ke.astype(BF16)
                elif c == 2:
                    ql_ref[0:chunk, lanes] = qe.astype(BF16)
                else:
                    ql_ref[chunk:half, lanes] = (qe * e_tot2[:, lanes]).astype(BF16)
    yield

    if n_chunks == 1:
        row = lax.broadcasted_iota(jnp.int32, (MIX_TILE, MIX_TILE), 0)
        col = lax.broadcasted_iota(jnp.int32, (MIX_TILE, MIX_TILE), 1)
        causal = row >= col
        for h in range(N_HEADS):
            lanes = slice(h * HEAD_K, (h + 1) * HEAD_K)
            vl = slice(h * dv, (h + 1) * dv)
            s = _dot_nt(qt_ref[:, lanes], kt_ref[:, lanes])
            p = jnp.where(causal, s, 0.0).astype(BF16)
            st = st_ref[h]
            o = _dot(p, vb_ref[:, vl]) + _dot(qi_ref[:, lanes], st.astype(BF16))
            st_ref[h] = st * decay_col[lanes, :] + _dot_tn(ks_ref[:, lanes], vb_ref[:, vl])
            mixed_ref[:, out_off + h * dv:out_off + (h + 1) * dv] = _rms(o, norm_w)
            if h % 2 == 1:
                yield
        return

    row = lax.broadcasted_iota(jnp.int32, (half, half), 0)
    col = lax.broadcasted_iota(jnp.int32, (half, half), 1)
    same_chunk = ((row >= chunk) == (col >= chunk)) & (row >= col)
    for h in range(N_HEADS):
        lanes = slice(h * HEAD_K, (h + 1) * HEAD_K)
        for hf in range(2):
            rows = slice(hf * half, (hf + 1) * half)
            d = _dot_nt(qt_ref[rows, lanes], kt_ref[rows, lanes])
            p_ref[h, rows, rows] = jnp.where(same_chunk, d, 0.0).astype(BF16)
            pair = slice(hf * chunk, (hf + 1) * chunk)
            lo = hf * half
            p_ref[h, lo + chunk:lo + half, lo:lo + chunk] = _dot_nt(
                qo_ref[pair, lanes], ko_ref[pair, lanes]).astype(BF16)
        p_ref[h, half:, 0:half] = _dot_nt(ql_ref[:, lanes], kl_ref[:, lanes]).astype(BF16)
    yield
    for h in range(N_HEADS):
        lanes = slice(h * HEAD_K, (h + 1) * HEAD_K)
        vl = slice(h * dv, (h + 1) * dv)
        ol = slice(out_off + h * dv, out_off + (h + 1) * dv)
        st = st_ref[h]
        st_b = st.astype(BF16)
        mixed_ref[0:half, ol] = (_dot(p_ref[h, 0:half, 0:half], vb_ref[0:half, vl])
                                 + _dot(qi_ref[0:half, lanes], st_b))
        mixed_ref[half:, ol] = (_dot(p_ref[h, half:, :], vb_ref[:, vl])
                                + _dot(qi_ref[half:, lanes], st_b))
        st_ref[h] = st * decay_col[lanes, :] + _dot_tn(ks_ref[:, lanes], vb_ref[:, vl])
    yield
    for h in range(N_HEADS):
        ol = slice(out_off + h * dv, out_off + (h + 1) * dv)
        mixed_ref[:, ol] = _rms(mixed_ref[:, ol], norm_w)


def _mixer_safe(q_ref, k_ref, vb_ref, b_ref, st_ref, norm_w, mixed_ref, out_off, dv, segment):
    row16 = lax.broadcasted_iota(jnp.int32, (SAFE_BLOCK, 1), 0)

    def block_body(sb, carry):
        s0 = pl.multiple_of(sb * SAFE_BLOCK, SAFE_BLOCK)
        rows_b = pl.ds(s0, SAFE_BLOCK)
        before = pl.multiple_of(jnp.maximum(s0 - 8, 0), 8)
        prev = b_ref[pl.ds(before, 8), :][7:8, :]
        prev = jnp.where(s0 % segment == 0, 0.0, prev)
        for h in range(N_HEADS):
            lanes = slice(h * HEAD_K, (h + 1) * HEAD_K)
            vl = slice(h * dv, (h + 1) * dv)
            bl = b_ref[rows_b, lanes] - prev[:, lanes]
            q = q_ref[rows_b, lanes]
            k = k_ref[rows_b, lanes]
            vb = vb_ref[rows_b, vl]
            v = vb.astype(F32)
            o = jnp.zeros((SAFE_BLOCK, dv), F32)
            for j in range(SAFE_BLOCK):
                w = jnp.exp(jnp.minimum(bl - bl[j:j + 1, :], 0.0))
                sc = jnp.sum(q * k[j:j + 1, :] * w, axis=-1, keepdims=True)
                o = o + jnp.where(row16 >= j, sc, 0.0) * v[j:j + 1, :]
            st = st_ref[h]
            o = o + _dot((q * jnp.exp(bl)).astype(BF16), st.astype(BF16))
            b_end = bl[SAFE_BLOCK - 1:SAFE_BLOCK, :]
            kl = (k * jnp.exp(b_end - bl)).astype(BF16)
            st_ref[h] = st * _column(jnp.exp(b_end)) + _dot_tn(kl, vb)
            mixed_ref[rows_b, out_off + h * dv:out_off + (h + 1) * dv] = _rms(o, norm_w)
        return carry

    lax.fori_loop(0, SEQ_TILE // SAFE_BLOCK, block_body, 0)


def _fast_path_ok(b_ref, n_chunks):
    chunk = MIX_TILE // n_chunks
    worst = jnp.zeros((1, KEY_WIDTH), F32)
    for c in range(n_chunks):
        r = b_ref[c * chunk + chunk // 2 - 1:c * chunk + chunk // 2, :]
        total = b_ref[(c + 1) * chunk - 1:(c + 1) * chunk, :]
        worst = jnp.maximum(worst, jnp.maximum(-r, r - total))
    return jnp.max(worst) < FAST_PATH_LIMIT


def _piece_order(gates, outs, epilogue):
    return [[gates[2]], [gates[3]], [outs[3]],
            [], [], [],
            [outs[2]], [], [],
            [], [outs[0]], [],
            [outs[1], epilogue]]


def _layer_kernel(x_ref, mem_ref, tri_tile_ref, tri_chunk_ref,
                  norm_w_ref, w_a_ref, w_b_ref, w_up_ref, b_gate_ref,
                  gla_norm_ref, lb_ref, hgrn_norm_ref, mem_norm_ref, w_mem_ref, xattn_norm_ref,
                  w_out_ref, final_norm_ref,
                  out_ref,
                  h_ref, gq_ref, gk_ref, gb_ref, hq_ref, hk_ref, hb_ref,
                  gv_ref, hv_ref, ghi_ref, glo_ref, hhi_ref, hlo_ref,
                  mixed_ref, gs_ref, xq_ref, mk_ref, mv_ref,
                  gla_st_ref, hgrn_st_ref, gla_prev_ref, hgrn_prev_ref,
                  g_qt_ref, g_kt_ref, g_qi_ref, g_ks_ref,
                  h_qt_ref, h_kt_ref, h_qo_ref, h_ko_ref, h_qi_ref, h_ks_ref, h_ql_ref, h_kl_ref,
                  p_ref,
                  *, layer, final_norm):
    t = pl.program_id(1)
    n_pass = SEQ_TILE // MIX_TILE
    gla_operands = (g_qt_ref, g_kt_ref, None, None, g_qi_ref, g_ks_ref, None, None, None)
    hgrn_operands = (h_qt_ref, h_kt_ref, h_qo_ref, h_ko_ref, h_qi_ref, h_ks_ref, h_ql_ref,
                     h_kl_ref, p_ref)
    zg_ref, zf_ref = gb_ref, hk_ref

    @pl.when(t == 0)
    def _start_of_sequence():
        gla_st_ref[...] = jnp.zeros_like(gla_st_ref)
        hgrn_st_ref[...] = jnp.zeros_like(hgrn_st_ref)
        m = _rms(mem_ref[0], mem_norm_ref[...]).astype(BF16)
        mk_ref[...] = _dot(m, w_mem_ref[:, :XATTN_WIDTH]).astype(BF16)
        mv_ref[...] = _dot(m, w_mem_ref[:, XATTN_WIDTH:]).astype(BF16)

    h_ref[...] = _rms(x_ref[0], norm_w_ref[...]).astype(BF16)

    def proj_a(off, width):
        return _dot(h_ref[...], w_a_ref[:, off:off + width])

    def proj(off, width):
        return _dot(h_ref[...], w_b_ref[:, off:off + width])

    lr_hq = proj(OFF_LR, LANE + KEY_WIDTH)
    low_rank = lr_hq[:, :LANE].astype(BF16)
    hq_ref[...] = lr_hq[:, LANE:]
    zg_ref[...] = _dot(low_rank, w_up_ref[...]) + b_gate_ref[...]
    zf_ref[...] = proj(OFF_HF, KEY_WIDTH)

    lbw = lb_ref[...]
    lb_e = jnp.exp(lbw - jnp.max(lbw, axis=0, keepdims=True))
    lb_p = lb_e / jnp.sum(lb_e, axis=0, keepdims=True)
    lb = jnp.sum(lb_p[:layer + 1], axis=0, keepdims=True) - lb_p[0:1]
    log_lb = jnp.log(jnp.maximum(lb, LB_FLOOR))
    log_1m_lb = jnp.log1p(-lb)
    one_m_lb = 1.0 - lb

    def decay(j):
        rows = slice(j * DECAY_ROWS, (j + 1) * DECAY_ROWS)
        z = zg_ref[rows, :]
        ls = jnp.minimum(z, 0.0) - jnp.log(1.0 + jnp.exp(-jnp.abs(z)))
        _store_split(ls * (1.0 / GLA_GATE_TAU), ghi_ref, glo_ref, rows)
        z = zf_ref[rows, :]
        e = jnp.exp(-jnp.abs(z))
        u = 1.0 + e
        c = log_1m_lb + (jnp.minimum(z, 0.0) - jnp.log(u))
        log_f = jnp.maximum(log_lb, c) + jnp.log(1.0 + jnp.exp(-jnp.abs(log_lb - c)))
        _store_split(log_f, hhi_ref, hlo_ref, rows)
        hk_ref[rows, :] = one_m_lb * (jnp.where(z >= 0.0, e, 1.0) / u)

    def gq_piece():
        gq_ref[...] = proj_a(OFF_GQ, KEY_WIDTH) * (HEAD_K ** -0.5)

    def gk_piece():
        gk_ref[...] = proj_a(OFF_GK, KEY_WIDTH)

    def gv_piece(c0):
        def emit():
            gv_ref[:, c0:c0 + KEY_WIDTH] = proj_a(OFF_GV + c0, KEY_WIDTH).astype(BF16)
        return emit

    def hv_piece():
        hv_ref[...] = proj(OFF_HI, HGRN_VAL_WIDTH).astype(BF16)

    xoff = GLA_VAL_WIDTH + HGRN_VAL_WIDTH
    gate_block = 512

    def gate_piece(c0):
        def emit():
            g = proj(OFF_GATE + c0, gate_block)
            gs_ref[:, c0:c0 + gate_block] = g * jax.nn.sigmoid(g)
        return emit

    def xq_piece():
        xq_ref[...] = proj(OFF_XQ, XATTN_WIDTH).astype(BF16)

    def xattn_piece(s, hd):
        def emit():
            rows = slice(s * MIX_TILE, (s + 1) * MIX_TILE)
            lanes = slice(hd * HEAD_K, (hd + 1) * HEAD_K)
            sc = _dot_nt(xq_ref[rows, lanes], mk_ref[:, lanes]) * (HEAD_K ** -0.5)
            p = jnp.exp(sc - jnp.max(sc, axis=-1, keepdims=True))
            o = _dot(p.astype(BF16), mv_ref[:, lanes]) / jnp.sum(p, axis=-1, keepdims=True)
            mixed_ref[rows, xoff + hd * HEAD_K:xoff + (hd + 1) * HEAD_K] = _rms(
                o, xattn_norm_ref[...])
        return emit

    out_started = [False]

    def out_piece(c0):
        def emit():
            cols = slice(c0, c0 + gate_block)
            gated = (mixed_ref[:, cols] * gs_ref[:, cols]).astype(BF16)
            part = _dot(gated, w_out_ref[cols, :])
            out_ref[0] = (out_ref[0] if out_started[0] else x_ref[0]) + part
            out_started[0] = True
        return emit

    def epilogue():
        if final_norm:
            out_ref[0] = _rms(out_ref[0], final_norm_ref[...])

    gates = [gate_piece(c0) for c0 in range(0, D_MIX, gate_block)]
    heads = [[xattn_piece(s, hd) for hd in range(N_HEADS)] for s in range(n_pass)]
    outs = [out_piece(c0) for c0 in range(0, D_MIX, gate_block)]

    xq_piece()
    matmul_pieces = [hv_piece, gq_piece, gk_piece, gv_piece(0), gv_piece(KEY_WIDTH),
                     gates[0], gates[1], None]
    vector_pieces = heads[0] + heads[1] + [functools.partial(decay, j)
                                           for j in range(SEQ_TILE // DECAY_ROWS)]
    assert len(vector_pieces) == 2 * len(matmul_pieces)
    for j, piece in enumerate(matmul_pieces):
        if piece is not None:
            piece()
        vector_pieces[2 * j]()
        vector_pieces[2 * j + 1]()

    gla_ok = hgrn_ok = None
    for s in range(n_pass):
        rows = slice(s * MIX_TILE, (s + 1) * MIX_TILE)
        gb_ref[rows, :] = (_dot(tri_tile_ref[...], ghi_ref[rows, :])
                           + _dot(tri_tile_ref[...], glo_ref[rows, :]))
        hb_ref[rows, :] = (_dot(tri_chunk_ref[...], hhi_ref[rows, :])
                           + _dot(tri_chunk_ref[...], hlo_ref[rows, :]))
        g_ok = _fast_path_ok(gb_ref.at[rows], GLA_CHUNKS)
        h_ok = _fast_path_ok(hb_ref.at[rows], HGRN_CHUNKS)
        gla_ok = g_ok if gla_ok is None else jnp.logical_and(gla_ok, g_ok)
        hgrn_ok = h_ok if hgrn_ok is None else jnp.logical_and(hgrn_ok, h_ok)

    groups = iter(_piece_order(gates, outs, epilogue))

    def fill():
        for piece in next(groups, []):
            piece()

    gla_prev_ref[...] = gla_st_ref[...]
    hgrn_prev_ref[...] = hgrn_st_ref[...]
    pass_rows = [slice(s * MIX_TILE, (s + 1) * MIX_TILE) for s in range(n_pass)]
    for s, rows in enumerate(pass_rows):
        for _ in _mixer_fast(hq_ref.at[rows], hk_ref.at[rows], hv_ref.at[rows], hb_ref.at[rows],
                             hgrn_st_ref, hgrn_norm_ref[...], mixed_ref.at[rows],
                             GLA_VAL_WIDTH, HGRN_HEAD_V, HGRN_CHUNKS,
                             tuple(r.at[s] for r in hgrn_operands)):
            fill()
    for rows in pass_rows:
        for _ in _mixer_fast(gq_ref.at[rows], gk_ref.at[rows], gv_ref.at[rows], gb_ref.at[rows],
                             gla_st_ref, gla_norm_ref[...], mixed_ref.at[rows],
                             0, GLA_HEAD_V, GLA_CHUNKS, gla_operands):
            fill()
    for group in groups:
        for piece in group:
            piece()

    @pl.when(jnp.logical_not(jnp.logical_and(gla_ok, hgrn_ok)))
    def _redo_with_pairwise_path():
        @pl.when(jnp.logical_not(gla_ok))
        def _gla_pairwise():
            gla_st_ref[...] = gla_prev_ref[...]
            _mixer_safe(gq_ref, gk_ref, gv_ref, gb_ref, gla_st_ref, gla_norm_ref[...],
                        mixed_ref, 0, GLA_HEAD_V, MIX_TILE // GLA_CHUNKS)

        @pl.when(jnp.logical_not(hgrn_ok))
        def _hgrn_pairwise():
            hgrn_st_ref[...] = hgrn_prev_ref[...]
            _mixer_safe(hq_ref, hk_ref, hv_ref, hb_ref, hgrn_st_ref, hgrn_norm_ref[...],
                        mixed_ref, GLA_VAL_WIDTH, HGRN_HEAD_V, MIX_TILE // HGRN_CHUNKS)

        out_started[0] = False
        for piece in outs:
            piece()
        epilogue()


def _layer_call(x, mem, params, layer, final_norm):
    batch, seq, _ = x.shape
    n_tiles = seq // SEQ_TILE

    def resident(arr, stacked):
        if stacked:
            shape = (None,) + arr.shape[1:]
            index = (layer,) + (0,) * (arr.ndim - 1)
        else:
            shape = arr.shape
            index = (0,) * arr.ndim
        return pl.BlockSpec(shape, lambda b, t: index, pipeline_mode=pl.Buffered(1))

    in_specs = [
        pl.BlockSpec((1, SEQ_TILE, D_MODEL), lambda b, t: (b, t, 0)),
        pl.BlockSpec((1, MEM_LEN, D_MODEL), lambda b, t: (b, 0, 0), pipeline_mode=pl.Buffered(1)),
    ] + [resident(arr, stacked) for arr, stacked in params]
    params = [arr for arr, _ in params]
    key_f32 = pltpu.VMEM((SEQ_TILE, KEY_WIDTH), F32)
    key_bf16 = pltpu.VMEM((SEQ_TILE, KEY_WIDTH), BF16)
    operand = pltpu.VMEM((MIX_TILE, KEY_WIDTH), BF16)
    n_pass = SEQ_TILE // MIX_TILE
    per_pass = pltpu.VMEM((n_pass, MIX_TILE, KEY_WIDTH), BF16)
    per_pass_half = pltpu.VMEM((n_pass, MIX_TILE // 2, KEY_WIDTH), BF16)
    gla_state = pltpu.VMEM((N_HEADS, HEAD_K, GLA_HEAD_V), F32)
    hgrn_state = pltpu.VMEM((N_HEADS, HEAD_K, HGRN_HEAD_V), F32)
    scratch = [
        pltpu.VMEM((SEQ_TILE, D_MODEL), BF16),
        key_f32, key_f32, key_f32,
        key_f32, key_f32, key_f32,
        pltpu.VMEM((SEQ_TILE, GLA_VAL_WIDTH), BF16),
        pltpu.VMEM((SEQ_TILE, HGRN_VAL_WIDTH), BF16),
        key_bf16, key_bf16, key_bf16, key_bf16,
        pltpu.VMEM((SEQ_TILE, D_MIX), F32),
        pltpu.VMEM((SEQ_TILE, D_MIX), F32),
        key_bf16,
        pltpu.VMEM((MEM_LEN, XATTN_WIDTH), BF16),
        pltpu.VMEM((MEM_LEN, XATTN_WIDTH), BF16),
        gla_state, hgrn_state,
        gla_state, hgrn_state,
        operand, operand, operand, operand,
        per_pass, per_pass, per_pass_half, per_pass_half, per_pass, per_pass,
        per_pass_half, per_pass_half,
        pltpu.VMEM((n_pass, N_HEADS, MIX_TILE, MIX_TILE), BF16),
    ]
    return pl.pallas_call(
        functools.partial(_layer_kernel, layer=layer, final_norm=final_norm),
        out_shape=jax.ShapeDtypeStruct(x.shape, x.dtype),
        grid=(batch, n_tiles),
        in_specs=in_specs,
        out_specs=pl.BlockSpec((1, SEQ_TILE, D_MODEL), lambda b, t: (b, t, 0)),
        scratch_shapes=scratch,
        compiler_params=pltpu.CompilerParams(
            dimension_semantics=("arbitrary", "arbitrary"),
            vmem_limit_bytes=VMEM_LIMIT_BYTES),
        name=f"hybrid_layer_{layer}",
    )(x, mem, *params)


def _window_kernel(w_ref, o_ref):
    o_ref[...] = w_ref[0].astype(BF16).T


def _projection_window(w_t, start, width, step):
    depth, _, k = w_t.shape
    assert width % step == 0 and start % 16 == 0
    return pl.pallas_call(
        _window_kernel,
        out_shape=jax.ShapeDtypeStruct((depth, k, width), BF16),
        grid=(depth, width // step),
        in_specs=[pl.BlockSpec((pl.Element(1), pl.Element(step), pl.Element(k)),
                               lambda l, i: (l, pl.multiple_of(start + i * step, 16), 0))],
        out_specs=pl.BlockSpec((None, k, step), lambda l, i: (l, 0, i)),
        compiler_params=pltpu.CompilerParams(dimension_semantics=("arbitrary", "arbitrary")),
        name=f"projection_window_{start}",
    )(w_t)


def _lower_tri_ones(segment):
    idx = jnp.arange(MIX_TILE)
    keep = (idx[None, :] <= idx[:, None]) & ((idx[None, :] // segment) == (idx[:, None] // segment))
    return keep.astype(BF16)


def kernel(x, mem, norm_w, w_in, gla_w_gate_up, gla_b_gate, gla_norm_w, hgrn_lower_bounds,
           hgrn_norm_w, mem_norm_w, w_mem_kv, xattn_norm_w, w_out, final_norm_w):
    assert x.shape[1] % SEQ_TILE == 0 and SEQ_TILE % MIX_TILE == 0
    rows = lambda v: (v.reshape(DEPTH, 1, -1).astype(F32), True)
    w_t = jnp.swapaxes(w_in, 1, 2)
    w_a = _projection_window(w_t, 0, WIN_A_WIDTH, WIN_A_STEP)
    w_b = _projection_window(w_t, WIN_B_START, w_in.shape[2] - WIN_B_START, WIN_B_STEP)
    w_up = jnp.pad(gla_w_gate_up, ((0, 0), (LOWRANK_LEAD, 0), (0, 0))).astype(BF16)
    params = (
        (_lower_tri_ones(MIX_TILE // GLA_CHUNKS), False),
        (_lower_tri_ones(MIX_TILE // HGRN_CHUNKS), False),
        rows(norm_w), (w_a, True), (w_b, True), (w_up, True), rows(gla_b_gate),
        rows(gla_norm_w), (hgrn_lower_bounds.astype(F32), False), rows(hgrn_norm_w),
        rows(mem_norm_w), (w_mem_kv.astype(BF16), True), rows(xattn_norm_w),
        (w_out.astype(BF16), True), (final_norm_w.reshape(1, -1).astype(F32), False),
    )
    for layer in range(DEPTH):
        x = _layer_call(x, mem, params, layer, final_norm=(layer == DEPTH - 1))
    return x
```

```python
import functools

import jax
import jax.numpy as jnp
from jax import lax
from jax.experimental import pallas as pl
from jax.experimental.pallas import tpu as pltpu

F32 = jnp.float32
BF16 = jnp.bfloat16

D_MODEL = 1024
DEPTH = 2
MEM_LEN = 256
N_HEADS = 4
HEAD_K = 128
GLA_HEAD_V = 256
HGRN_HEAD_V = 128
KEY_WIDTH = N_HEADS * HEAD_K
GLA_VAL_WIDTH = N_HEADS * GLA_HEAD_V
HGRN_VAL_WIDTH = N_HEADS * HGRN_HEAD_V
XATTN_WIDTH = N_HEADS * HEAD_K
D_MIX = 2048
GLA_LOWRANK = 16
GLA_GATE_TAU = 16.0
LB_FLOOR = 1e-30
NORM_EPS = 1e-6

LANE = 128
LOWRANK_LEAD = LANE - GLA_LOWRANK
WIN_A_WIDTH = 2 * KEY_WIDTH + GLA_VAL_WIDTH
WIN_B_START = WIN_A_WIDTH - LOWRANK_LEAD
OFF_GQ, OFF_GK, OFF_GV = 0, KEY_WIDTH, 2 * KEY_WIDTH
OFF_LR = 0
OFF_HQ = OFF_LR + LANE
OFF_HF = OFF_HQ + KEY_WIDTH
OFF_HI = OFF_HF + KEY_WIDTH
OFF_XQ = OFF_HI + HGRN_VAL_WIDTH
OFF_GATE = OFF_XQ + XATTN_WIDTH

SEQ_TILE = 512
MIX_TILE = 256
GLA_CHUNKS = 1
HGRN_CHUNKS = 4
DECAY_ROWS = 64
SAFE_BLOCK = 16
FAST_PATH_LIMIT = 60.0

WIN_A_STEP, WIN_B_STEP = 1024, 1408

VMEM_LIMIT_BYTES = 58 * 1024 * 1024


def _dot(a, b):
    return jnp.dot(a, b, preferred_element_type=F32)


def _dot_nt(a, b):
    return lax.dot_general(a, b, (((1,), (1,)), ((), ())), preferred_element_type=F32)


def _dot_tn(a, b):
    return lax.dot_general(a, b, (((0,), (0,)), ((), ())), preferred_element_type=F32)


def _rms(x, w):
    return x * lax.rsqrt(jnp.mean(x * x, axis=-1, keepdims=True) + NORM_EPS) * w


def _store_split(x, hi_ref, lo_ref, rows):
    hi = x.astype(BF16)
    hi_ref[rows, :] = hi
    lo_ref[rows, :] = (x - hi.astype(F32)).astype(BF16)


def _column(row):
    return jnp.broadcast_to(row, (8, row.shape[1])).T[:, 0:1]


def _mixer_fast(q_ref, k_ref, vb_ref, b_ref, st_ref, norm_w, mixed_ref, out_off, dv, n_chunks,
                operand_refs):
    assert n_chunks in (1, 4)
    qt_ref, kt_ref, qo_ref, ko_ref, qi_ref, ks_ref, ql_ref, kl_ref, p_ref = operand_refs
    chunk = MIX_TILE // n_chunks
    half = MIX_TILE // 2
    tot = [b_ref[(c + 1) * chunk - 1:(c + 1) * chunk, :] for c in range(n_chunks)]
    ref = [b_ref[c * chunk + chunk // 2 - 1:c * chunk + chunk // 2, :] for c in range(n_chunks)]
    before = [jnp.zeros_like(tot[0])]
    for c in range(n_chunks):
        before.append(before[-1] + tot[c])
    tile_total = before[n_chunks]
    decay_col = _column(jnp.exp(tile_total))
    if n_chunks == 4:
        e_in = [None] + [jnp.exp(before[c]) for c in range(1, 4)]
        e_out = [jnp.exp(tile_total - before[c + 1]) for c in range(3)] + [None]
        e_tot1 = jnp.exp(tot[1])
        e_tot2 = jnp.exp(tot[2])

    for c in range(n_chunks):
        rows = slice(c * chunk, (c + 1) * chunk)
        for h in range(N_HEADS):
            lanes = slice(h * HEAD_K, (h + 1) * HEAD_K)
            b = b_ref[rows, lanes]
            q = q_ref[rows, lanes]
            k = k_ref[rows, lanes]
            r = ref[c][:, lanes]
            qe = q * jnp.exp(b)
            ke = k * jnp.exp(tot[c][:, lanes] - b)
            qt_ref[rows, lanes] = (q * jnp.exp(b - r)).astype(BF16)
            kt_ref[rows, lanes] = (k * jnp.exp(r - b)).astype(BF16)
            if n_chunks == 1:
                qi_ref[rows, lanes] = qe.astype(BF16)
                ks_ref[rows, lanes] = ke.astype(BF16)
            else:
                pair = slice((c // 2) * chunk, (c // 2 + 1) * chunk)
                if c % 2 == 1:
                    qo_ref[pair, lanes] = qe.astype(BF16)
                else:
                    ko_ref[pair, lanes] = ke.astype(BF16)
                qi = qe if c == 0 else qe * e_in[c][:, lanes]
                ks = ke if c == 3 else ke * e_out[c][:, lanes]
                qi_ref[rows, lanes] = qi.astype(BF16)
                ks_ref[rows, lanes] = ks.astype(BF16)
                if c == 0:
                    kl_ref[0:chunk, lanes] = (ke * e_tot1[:, lanes]).astype(BF16)
                elif c == 1:
                    kl_ref[chunk:half, lanes] = ke.astype(BF16)
                elif c == 2:
                    ql_ref[0:chunk, lanes] = qe.astype(BF16)
                else:
                    ql_ref[chunk:half, lanes] = (qe * e_tot2[:, lanes]).astype(BF16)
    yield

    if n_chunks == 1:
        row = lax.broadcasted_iota(jnp.int32, (MIX_TILE, MIX_TILE), 0)
        col = lax.broadcasted_iota(jnp.int32, (MIX_TILE, MIX_TILE), 1)
        causal = row >= col
        for h in range(N_HEADS):
            lanes = slice(h * HEAD_K, (h + 1) * HEAD_K)
            vl = slice(h * dv, (h + 1) * dv)
            s = _dot_nt(qt_ref[:, lanes], kt_ref[:, lanes])
            p = jnp.where(causal, s, 0.0).astype(BF16)
            st = st_ref[h]
            o = _dot(p, vb_ref[:, vl]) + _dot(qi_ref[:, lanes], st.astype(BF16))
            st_ref[h] = st * decay_col[lanes, :] + _dot_tn(ks_ref[:, lanes], vb_ref[:, vl])
            mixed_ref[:, out_off + h * dv:out_off + (h + 1) * dv] = _rms(o, norm_w)
            if h % 2 == 1:
                yield
        return

    row = lax.broadcasted_iota(jnp.int32, (half, half), 0)
    col = lax.broadcasted_iota(jnp.int32, (half, half), 1)
    same_chunk = ((row >= chunk) == (col >= chunk)) & (row >= col)
    for h in range(N_HEADS):
        lanes = slice(h * HEAD_K, (h + 1) * HEAD_K)
        for hf in range(2):
            rows = slice(hf * half, (hf + 1) * half)
            d = _dot_nt(qt_ref[rows, lanes], kt_ref[rows, lanes])
            p_ref[h, rows, rows] = jnp.where(same_chunk, d, 0.0).astype(BF16)
            pair = slice(hf * chunk, (hf + 1) * chunk)
            lo = hf * half
            p_ref[h, lo + chunk:lo + half, lo:lo + chunk] = _dot_nt(
                qo_ref[pair, lanes], ko_ref[pair, lanes]).astype(BF16)
        p_ref[h, half:, 0:half] = _dot_nt(ql_ref[:, lanes], kl_ref[:, lanes]).astype(BF16)
    yield
    for h in range(N_HEADS):
        lanes = slice(h * HEAD_K, (h + 1) * HEAD_K)
        vl = slice(h * dv, (h + 1) * dv)
        ol = slice(out_off + h * dv, out_off + (h + 1) * dv)
        st = st_ref[h]
        st_b = st.astype(BF16)
        mixed_ref[0:half, ol] = (_dot(p_ref[h, 0:half, 0:half], vb_ref[0:half, vl])
                                 + _dot(qi_ref[0:half, lanes], st_b))
        mixed_ref[half:, ol] = (_dot(p_ref[h, half:, :], vb_ref[:, vl])
                                + _dot(qi_ref[half:, lanes], st_b))
        st_ref[h] = st * decay_col[lanes, :] + _dot_tn(ks_ref[:, lanes], vb_ref[:, vl])
    yield
    for h in range(N_HEADS):
        ol = slice(out_off + h * dv, out_off + (h + 1) * dv)
        mixed_ref[:, ol] = _rms(mixed_ref[:, ol], norm_w)


def _mixer_safe(q_ref, k_ref, vb_ref, b_ref, st_ref, norm_w, mixed_ref, out_off, dv, segment):
    row16 = lax.broadcasted_iota(jnp.int32, (SAFE_BLOCK, 1), 0)

    def block_body(sb, carry):
        s0 = pl.multiple_of(sb * SAFE_BLOCK, SAFE_BLOCK)
        rows_b = pl.ds(s0, SAFE_BLOCK)
        before = pl.multiple_of(jnp.maximum(s0 - 8, 0), 8)
        prev = b_ref[pl.ds(before, 8), :][7:8, :]
        prev = jnp.where(s0 % segment == 0, 0.0, prev)
        for h in range(N_HEADS):
            lanes = slice(h * HEAD_K, (h + 1) * HEAD_K)
            vl = slice(h * dv, (h + 1) * dv)
            bl = b_ref[rows_b, lanes] - prev[:, lanes]
            q = q_ref[rows_b, lanes]
            k = k_ref[rows_b, lanes]
            vb = vb_ref[rows_b, vl]
            v = vb.astype(F32)
            o = jnp.zeros((SAFE_BLOCK, dv), F32)
            for j in range(SAFE_BLOCK):
                w = jnp.exp(jnp.minimum(bl - bl[j:j + 1, :], 0.0))
                sc = jnp.sum(q * k[j:j + 1, :] * w, axis=-1, keepdims=True)
                o = o + jnp.where(row16 >= j, sc, 0.0) * v[j:j + 1, :]
            st = st_ref[h]
            o = o + _dot((q * jnp.exp(bl)).astype(BF16), st.astype(BF16))
            b_end = bl[SAFE_BLOCK - 1:SAFE_BLOCK, :]
            kl = (k * jnp.exp(b_end - bl)).astype(BF16)
            st_ref[h] = st * _column(jnp.exp(b_end)) + _dot_tn(kl, vb)
            mixed_ref[rows_b, out_off + h * dv:out_off + (h + 1) * dv] = _rms(o, norm_w)
        return carry

    lax.fori_loop(0, SEQ_TILE // SAFE_BLOCK, block_body, 0)


def _fast_path_ok(b_ref, n_chunks):
    chunk = MIX_TILE // n_chunks
    worst = jnp.zeros((1, KEY_WIDTH), F32)
    for c in range(n_chunks):
        r = b_ref[c * chunk + chunk // 2 - 1:c * chunk + chunk // 2, :]
        total = b_ref[(c + 1) * chunk - 1:(c + 1) * chunk, :]
        worst = jnp.maximum(worst, jnp.maximum(-r, r - total))
    return jnp.max(worst) < FAST_PATH_LIMIT


def _piece_order(gates, outs, epilogue):
    return [[gates[2]], [gates[3]], [outs[3]],
            [], [], [],
            [outs[2]], [], [],
            [], [outs[0]], [],
            [outs[1], epilogue]]


def _layer_kernel(x_ref, mem_ref, tri_tile_ref, tri_chunk_ref,
                  norm_w_ref, w_a_ref, w_b_ref, w_up_ref, b_gate_ref,
                  gla_norm_ref, lb_ref, hgrn_norm_ref, mem_norm_ref, w_mem_ref, xattn_norm_ref,
                  w_out_ref, final_norm_ref,
                  out_ref,
                  h_ref, gq_ref, gk_ref, gb_ref, hq_ref, hk_ref, hb_ref,
                  gv_ref, hv_ref, ghi_ref, glo_ref, hhi_ref, hlo_ref,
                  mixed_ref, gs_ref, xq_ref, mk_ref, mv_ref,
                  gla_st_ref, hgrn_st_ref, gla_prev_ref, hgrn_prev_ref,
                  g_qt_ref, g_kt_ref, g_qi_ref, g_ks_ref,
                  h_qt_ref, h_kt_ref, h_qo_ref, h_ko_ref, h_qi_ref, h_ks_ref, h_ql_ref, h_kl_ref,
                  p_ref,
                  *, layer, final_norm):
    t = pl.program_id(1)
    n_pass = SEQ_TILE // MIX_TILE
    gla_operands = (g_qt_ref, g_kt_ref, None, None, g_qi_ref, g_ks_ref, None, None, None)
    hgrn_operands = (h_qt_ref, h_kt_ref, h_qo_ref, h_ko_ref, h_qi_ref, h_ks_ref, h_ql_ref,
                     h_kl_ref, p_ref)
    zg_ref, zf_ref = gb_ref, hk_ref

    @pl.when(t == 0)
    def _start_of_sequence():
        gla_st_ref[...] = jnp.zeros_like(gla_st_ref)
        hgrn_st_ref[...] = jnp.zeros_like(hgrn_st_ref)
        m = _rms(mem_ref[0], mem_norm_ref[...]).astype(BF16)
        mk_ref[...] = _dot(m, w_mem_ref[:, :XATTN_WIDTH]).astype(BF16)
        mv_ref[...] = _dot(m, w_mem_ref[:, XATTN_WIDTH:]).astype(BF16)

    h_ref[...] = _rms(x_ref[0], norm_w_ref[...]).astype(BF16)

    def proj_a(off, width):
        return _dot(h_ref[...], w_a_ref[:, off:off + width])

    def proj(off, width):
        return _dot(h_ref[...], w_b_ref[:, off:off + width])

    lr_hq = proj(OFF_LR, LANE + KEY_WIDTH)
    low_rank = lr_hq[:, :LANE].astype(BF16)
    hq_ref[...] = lr_hq[:, LANE:]
    zg_ref[...] = _dot(low_rank, w_up_ref[...]) + b_gate_ref[...]
    zf_ref[...] = proj(OFF_HF, KEY_WIDTH)

    lbw = lb_ref[...]
    lb_e = jnp.exp(lbw - jnp.max(lbw, axis=0, keepdims=True))
    lb_p = lb_e / jnp.sum(lb_e, axis=0, keepdims=True)
    lb = jnp.sum(lb_p[:layer + 1], axis=0, keepdims=True) - lb_p[0:1]
    log_lb = jnp.log(jnp.maximum(lb, LB_FLOOR))
    log_1m_lb = jnp.log1p(-lb)
    one_m_lb = 1.0 - lb

    def decay(j):
        rows = slice(j * DECAY_ROWS, (j + 1) * DECAY_ROWS)
        z = zg_ref[rows, :]
        ls = jnp.minimum(z, 0.0) - jnp.log(1.0 + jnp.exp(-jnp.abs(z)))
        _store_split(ls * (1.0 / GLA_GATE_TAU), ghi_ref, glo_ref, rows)
        z = zf_ref[rows, :]
        e = jnp.exp(-jnp.abs(z))
        u = 1.0 + e
        c = log_1m_lb + (jnp.minimum(z, 0.0) - jnp.log(u))
        log_f = jnp.maximum(log_lb, c) + jnp.log(1.0 + jnp.exp(-jnp.abs(log_lb - c)))
        _store_split(log_f, hhi_ref, hlo_ref, rows)
        hk_ref[rows, :] = one_m_lb * (jnp.where(z >= 0.0, e, 1.0) / u)

    def gq_piece():
        gq_ref[...] = proj_a(OFF_GQ, KEY_WIDTH) * (HEAD_K ** -0.5)

    def gk_piece():
        gk_ref[...] = proj_a(OFF_GK, KEY_WIDTH)

    def gv_piece(c0):
        def emit():
            gv_ref[:, c0:c0 + KEY_WIDTH] = proj_a(OFF_GV + c0, KEY_WIDTH).astype(BF16)
        return emit

    def hv_piece():
        hv_ref[...] = proj(OFF_HI, HGRN_VAL_WIDTH).astype(BF16)

    xoff = GLA_VAL_WIDTH + HGRN_VAL_WIDTH
    gate_block = 512

    def gate_piece(c0):
        def emit():
            g = proj(OFF_GATE + c0, gate_block)
            gs_ref[:, c0:c0 + gate_block] = g * jax.nn.sigmoid(g)
        return emit

    def xq_piece():
        xq_ref[...] = proj(OFF_XQ, XATTN_WIDTH).astype(BF16)

    def xattn_piece(s, hd):
        def emit():
            rows = slice(s * MIX_TILE, (s + 1) * MIX_TILE)
            lanes = slice(hd * HEAD_K, (hd + 1) * HEAD_K)
            sc = _dot_nt(xq_ref[rows, lanes], mk_ref[:, lanes]) * (HEAD_K ** -0.5)
            p = jnp.exp(sc - jnp.max(sc, axis=-1, keepdims=True))
            o = _dot(p.astype(BF16), mv_ref[:, lanes]) / jnp.sum(p, axis=-1, keepdims=True)
            mixed_ref[rows, xoff + hd * HEAD_K:xoff + (hd + 1) * HEAD_K] = _rms(
                o, xattn_norm_ref[...])
        return emit

    out_started = [False]

    def out_piece(c0):
        def emit():
            cols = slice(c0, c0 + gate_block)
            gated = (mixed_ref[:, cols] * gs_ref[:, cols]).astype(BF16)
            part = _dot(gated, w_out_ref[cols, :])
            out_ref[0] = (out_ref[0] if out_started[0] else x_ref[0]) + part
            out_started[0] = True
        return emit

    def epilogue():
        if final_norm:
            out_ref[0] = _rms(out_ref[0], final_norm_ref[...])

    gates = [gate_piece(c0) for c0 in range(0, D_MIX, gate_block)]
    heads = [[xattn_piece(s, hd) for hd in range(N_HEADS)] for s in range(n_pass)]
    outs = [out_piece(c0) for c0 in range(0, D_MIX, gate_block)]

    xq_piece()
    matmul_pieces = [hv_piece, gq_piece, gk_piece, gv_piece(0), gv_piece(KEY_WIDTH),
                     gates[0], gates[1], None]
    vector_pieces = heads[0] + heads[1] + [functools.partial(decay, j)
                                           for j in range(SEQ_TILE // DECAY_ROWS)]
    assert len(vector_pieces) == 2 * len(matmul_pieces)
    for j, piece in enumerate(matmul_pieces):
        if piece is not None:
            piece()
        vector_pieces[2 * j]()
        vector_pieces[2 * j + 1]()

    gla_ok = hgrn_ok = None
    for s in range(n_pass):
        rows = slice(s * MIX_TILE, (s + 1) * MIX_TILE)
        gb_ref[rows, :] = (_dot(tri_tile_ref[...], ghi_ref[rows, :])
                           + _dot(tri_tile_ref[...], glo_ref[rows, :]))
        hb_ref[rows, :] = (_dot(tri_chunk_ref[...], hhi_ref[rows, :])
                           + _dot(tri_chunk_ref[...], hlo_ref[rows, :]))
        g_ok = _fast_path_ok(gb_ref.at[rows], GLA_CHUNKS)
        h_ok = _fast_path_ok(hb_ref.at[rows], HGRN_CHUNKS)
        gla_ok = g_ok if gla_ok is None else jnp.logical_and(gla_ok, g_ok)
        hgrn_ok = h_ok if hgrn_ok is None else jnp.logical_and(hgrn_ok, h_ok)

    groups = iter(_piece_order(gates, outs, epilogue))

    def fill():
        for piece in next(groups, []):
            piece()

    gla_prev_ref[...] = gla_st_ref[...]
    hgrn_prev_ref[...] = hgrn_st_ref[...]
    pass_rows = [slice(s * MIX_TILE, (s + 1) * MIX_TILE) for s in range(n_pass)]
    for s, rows in enumerate(pass_rows):
        for _ in _mixer_fast(hq_ref.at[rows], hk_ref.at[rows], hv_ref.at[rows], hb_ref.at[rows],
                             hgrn_st_ref, hgrn_norm_ref[...], mixed_ref.at[rows],
                             GLA_VAL_WIDTH, HGRN_HEAD_V, HGRN_CHUNKS,
                             tuple(r.at[s] for r in hgrn_operands)):
            fill()
    for rows in pass_rows:
        for _ in _mixer_fast(gq_ref.at[rows], gk_ref.at[rows], gv_ref.at[rows], gb_ref.at[rows],
                             gla_st_ref, gla_norm_ref[...], mixed_ref.at[rows],
                             0, GLA_HEAD_V, GLA_CHUNKS, gla_operands):
            fill()
    for group in groups:
        for piece in group:
            piece()

    @pl.when(jnp.logical_not(jnp.logical_and(gla_ok, hgrn_ok)))
    def _redo_with_pairwise_path():
        @pl.when(jnp.logical_not(gla_ok))
        def _gla_pairwise():
            gla_st_ref[...] = gla_prev_ref[...]
            _mixer_safe(gq_ref, gk_ref, gv_ref, gb_ref, gla_st_ref, gla_norm_ref[...],
                        mixed_ref, 0, GLA_HEAD_V, MIX_TILE // GLA_CHUNKS)

        @pl.when(jnp.logical_not(hgrn_ok))
        def _hgrn_pairwise():
            hgrn_st_ref[...] = hgrn_prev_ref[...]
            _mixer_safe(hq_ref, hk_ref, hv_ref, hb_ref, hgrn_st_ref, hgrn_norm_ref[...],
                        mixed_ref, GLA_VAL_WIDTH, HGRN_HEAD_V, MIX_TILE // HGRN_CHUNKS)

        out_started[0] = False
        for piece in outs:
            piece()
        epilogue()


def _layer_call(x, mem, params, layer, final_norm):
    batch, seq, _ = x.shape
    n_tiles = seq // SEQ_TILE

    def resident(arr, stacked):
        if stacked:
            shape = (None,) + arr.shape[1:]
            index = (layer,) + (0,) * (arr.ndim - 1)
        else:
            shape = arr.shape
            index = (0,) * arr.ndim
        return pl.BlockSpec(shape, lambda b, t: index, pipeline_mode=pl.Buffered(1))

    in_specs = [
        pl.BlockSpec((1, SEQ_TILE, D_MODEL), lambda b, t: (b, t, 0)),
        pl.BlockSpec((1, MEM_LEN, D_MODEL), lambda b, t: (b, 0, 0), pipeline_mode=pl.Buffered(1)),
    ] + [resident(arr, stacked) for arr, stacked in params]
    params = [arr for arr, _ in params]
    key_f32 = pltpu.VMEM((SEQ_TILE, KEY_WIDTH), F32)
    key_bf16 = pltpu.VMEM((SEQ_TILE, KEY_WIDTH), BF16)
    operand = pltpu.VMEM((MIX_TILE, KEY_WIDTH), BF16)
    n_pass = SEQ_TILE // MIX_TILE
    per_pass = pltpu.VMEM((n_pass, MIX_TILE, KEY_WIDTH), BF16)
    per_pass_half = pltpu.VMEM((n_pass, MIX_TILE // 2, KEY_WIDTH), BF16)
    gla_state = pltpu.VMEM((N_HEADS, HEAD_K, GLA_HEAD_V), F32)
    hgrn_state = pltpu.VMEM((N_HEADS, HEAD_K, HGRN_HEAD_V), F32)
    scratch = [
        pltpu.VMEM((SEQ_TILE, D_MODEL), BF16),
        key_f32, key_f32, key_f32,
        key_f32, key_f32, key_f32,
        pltpu.VMEM((SEQ_TILE, GLA_VAL_WIDTH), BF16),
        pltpu.VMEM((SEQ_TILE, HGRN_VAL_WIDTH), BF16),
        key_bf16, key_bf16, key_bf16, key_bf16,
        pltpu.VMEM((SEQ_TILE, D_MIX), F32),
        pltpu.VMEM((SEQ_TILE, D_MIX), F32),
        key_bf16,
        pltpu.VMEM((MEM_LEN, XATTN_WIDTH), BF16),
        pltpu.VMEM((MEM_LEN, XATTN_WIDTH), BF16),
        gla_state, hgrn_state,
        gla_state, hgrn_state,
        operand, operand, operand, operand,
        per_pass, per_pass, per_pass_half, per_pass_half, per_pass, per_pass,
        per_pass_half, per_pass_half,
        pltpu.VMEM((n_pass, N_HEADS, MIX_TILE, MIX_TILE), BF16),
    ]
    return pl.pallas_call(
        functools.partial(_layer_kernel, layer=layer, final_norm=final_norm),
        out_shape=jax.ShapeDtypeStruct(x.shape, x.dtype),
        grid=(batch, n_tiles),
        in_specs=in_specs,
        out_specs=pl.BlockSpec((1, SEQ_TILE, D_MODEL), lambda b, t: (b, t, 0)),
        scratch_shapes=scratch,
        compiler_params=pltpu.CompilerParams(
            dimension_semantics=("parallel", "arbitrary"),
            vmem_limit_bytes=VMEM_LIMIT_BYTES),
        name=f"hybrid_layer_{layer}",
    )(x, mem, *params)


def _window_kernel(w_ref, o_ref):
    o_ref[...] = w_ref[0].astype(BF16).T


def _projection_window(w_t, start, width, step):
    depth, _, k = w_t.shape
    assert width % step == 0 and start % 16 == 0
    return pl.pallas_call(
        _window_kernel,
        out_shape=jax.ShapeDtypeStruct((depth, k, width), BF16),
        grid=(depth, width // step),
        in_specs=[pl.BlockSpec((pl.Element(1), pl.Element(step), pl.Element(k)),
                               lambda l, i: (l, pl.multiple_of(start + i * step, 16), 0))],
        out_specs=pl.BlockSpec((None, k, step), lambda l, i: (l, 0, i)),
        compiler_params=pltpu.CompilerParams(dimension_semantics=("arbitrary", "arbitrary")),
        name=f"projection_window_{start}",
    )(w_t)


def _lower_tri_ones(segment):
    idx = jnp.arange(MIX_TILE)
    keep = (idx[None, :] <= idx[:, None]) & ((idx[None, :] // segment) == (idx[:, None] // segment))
    return keep.astype(BF16)


def kernel(x, mem, norm_w, w_in, gla_w_gate_up, gla_b_gate, gla_norm_w, hgrn_lower_bounds,
           hgrn_norm_w, mem_norm_w, w_mem_kv, xattn_norm_w, w_out, final_norm_w):
    assert x.shape[1] % SEQ_TILE == 0 and SEQ_TILE % MIX_TILE == 0
    rows = lambda v: (v.reshape(DEPTH, 1, -1).astype(F32), True)
    w_t = jnp.swapaxes(w_in, 1, 2)
    w_a = _projection_window(w_t, 0, WIN_A_WIDTH, WIN_A_STEP)
    w_b = _projection_window(w_t, WIN_B_START, w_in.shape[2] - WIN_B_START, WIN_B_STEP)
    w_up = jnp.pad(gla_w_gate_up, ((0, 0), (LOWRANK_LEAD, 0), (0, 0))).astype(BF16)
    params = (
        (_lower_tri_ones(MIX_TILE // GLA_CHUNKS), False),
        (_lower_tri_ones(MIX_TILE // HGRN_CHUNKS), False),
        rows(norm_w), (w_a, True), (w_b, True), (w_up, True), rows(gla_b_gate),
        rows(gla_norm_w), (hgrn_lower_bounds.astype(F32), False), rows(hgrn_norm_w),
        rows(mem_norm_w), (w_mem_kv.astype(BF16), True), rows(xattn_norm_w),
        (w_out.astype(BF16), True), (final_norm_w.reshape(1, -1).astype(F32), False),
    )
    for layer in range(DEPTH):
        x = _layer_call(x, mem, params, layer, final_norm=(layer == DEPTH - 1))
    return x
```
